```python
import jax, jax.numpy as jnp
from jax import lax
import numpy as np

D_MODEL = 2048
BATCH = 16
SEQ = 256
DEPTH = 4
DEC_BATCH = 4
DEC_SEQ = 1024
PAST_LEN = 256

GRID_W = 64
HD = 128
EPS = 1e-6
NEG = -1e30
ROPE_THETA = 10000.0
CHUNK = 64
QBLK = 128
H_A = 8
DK_A = 128
DV_A = 128
W_A = H_A * DV_A
CONV_K = 5
H_B = 8
HKV_B = 2
W_B = H_B * HD
WIN = 128
H_C = 4
DK_C = 128
DV_C = 256
QK_C = H_C * DK_C
W_C = H_C * DV_C
GLA_RANK = 16
GLA_TAU = 16.0
H_D = 8
W_D = H_D * HD
NB_H = 8
NB_W = 16
NB_CBLK = 16
NB_CSPAN = 32

N_EVEN = (DEPTH + 1) // 2
N_ODD = DEPTH // 2
PA_EVEN = 4 * W_A + 4 * H_A
P_EVEN = PA_EVEN + 2 * W_B + 2 * HKV_B * HD
PC_ODD = 2 * QK_C + 2 * W_C + 2 * GLA_RANK
P_ODD = PC_ODD + 4 * W_D
MIX_EVEN = W_A + W_B
MIX_ODD = W_C + W_D
F32 = jnp.float32

kernel_name = 'hybrid_flow_backbone_step'


def rmsnorm(x, w):
    xf = x.astype(F32)
    y = xf * lax.rsqrt(jnp.mean(xf * xf, -1, keepdims=True) + EPS)
    return (y * w.astype(F32)).astype(x.dtype)


def l2norm(x):
    xf = x.astype(F32)
    return xf * lax.rsqrt(jnp.sum(xf * xf, -1, keepdims=True) + EPS)


def adaln(x, cond, norm_w, w_ada, b_ada):
    mod = jax.nn.silu(cond) @ w_ada + b_ada
    shift, scale, gate = jnp.split(mod, 3, axis=-1)
    return rmsnorm(x, norm_w) * (1 + scale) + shift, gate


def rope_axis(x, pos):
    half = x.shape[-1] // 2
    freq = ROPE_THETA ** (-jnp.arange(half, dtype=F32) / half)
    ang = pos.astype(F32)[:, None] * freq[None, :]
    cos, sin = jnp.cos(ang)[:, None, :], jnp.sin(ang)[:, None, :]
    x1, x2 = x[..., :half].astype(F32), x[..., half:].astype(F32)
    return jnp.concatenate([x1 * cos - x2 * sin, x2 * cos + x1 * sin], -1).astype(x.dtype)


def rope2d(x):
    t = jnp.arange(x.shape[1])
    h = x.shape[-1] // 2
    return jnp.concatenate([rope_axis(x[..., :h], t // GRID_W), rope_axis(x[..., h:], t % GRID_W)], -1)


def short_conv(x, w):
    pad = CONV_K // 2
    y = lax.conv_general_dilated(x, w[:, None, :].astype(x.dtype), window_strides=(1,),
                                 padding=[(pad, pad)], dimension_numbers=('NWC', 'WIO', 'NWC'),
                                 feature_group_count=x.shape[-1])
    return jax.nn.silu(y)


def attn_probs(s, sink):
    if sink is None:
        return jax.nn.softmax(s, -1)
    sk = sink.astype(F32).reshape(s.shape[1], s.shape[2])[None, :, :, None, None]
    m = jnp.maximum(jnp.max(s, -1, keepdims=True), sk)
    e = jnp.exp(s - m)
    return e / (jnp.sum(e, -1, keepdims=True) + jnp.exp(sk - m))


def dense_attn(q, k, v, sink):
    B_, L, H, D = q.shape
    Hkv = k.shape[2]
    G = H // Hkv
    qb = q.reshape(B_, L // QBLK, QBLK, Hkv, G, D).transpose(1, 0, 2, 3, 4, 5)
    scale = D ** -0.5

    def blk(qi):
        s = jnp.einsum('bqkgd,bskd->bkgqs', qi, k).astype(F32) * scale
        p = attn_probs(s, sink).astype(v.dtype)
        return jnp.einsum('bkgqs,bskd->bqkgd', p, v)

    o = lax.map(blk, qb)
    return o.transpose(1, 0, 2, 3, 4, 5).reshape(B_, L, H * D)


def windowed_attn(q, k, v, kc, vc, sink):
    B_, L, H, D = q.shape
    Hkv = k.shape[2]
    G = H // Hkv
    nq = L // QBLK
    span = QBLK + 2 * WIN
    pad = ((0, 0), (WIN, WIN), (0, 0), (0, 0))
    kp, vp = jnp.pad(k, pad), jnp.pad(v, pad)
    qb = q.reshape(B_, nq, QBLK, Hkv, G, D).transpose(1, 0, 2, 3, 4, 5)
    scale = D ** -0.5

    def blk(args):
        i, qi = args
        kw = lax.dynamic_slice_in_dim(kp, i * QBLK, span, axis=1)
        vw = lax.dynamic_slice_in_dim(vp, i * QBLK, span, axis=1)
        qpos = i * QBLK + jnp.arange(QBLK)
        kpos = i * QBLK - WIN + jnp.arange(span)
        ok = (jnp.abs(qpos[:, None] - kpos[None, :]) <= WIN) & (kpos >= 0)[None, :] & (kpos < L)[None, :]
        s_w = jnp.where(ok, jnp.einsum('bqkgd,bskd->bkgqs', qi, kw).astype(F32) * scale, NEG)
        s_c = jnp.einsum('bqkgd,bskd->bkgqs', qi, kc).astype(F32) * scale
        p = attn_probs(jnp.concatenate([s_w, s_c], -1), sink).astype(v.dtype)
        return (jnp.einsum('bkgqs,bskd->bqkgd', p[..., :span], vw)
                + jnp.einsum('bkgqs,bskd->bqkgd', p[..., span:], vc))

    o = lax.map(blk, (jnp.arange(nq), qb))
    return o.transpose(1, 0, 2, 3, 4, 5).reshape(B_, L, H * D)


def neighbourhood_attn(q, k, v, kc, vc, rpb):
    B_, L, H, D = q.shape
    rows = L // GRID_W
    kh = min(NB_H, rows)
    ncb = GRID_W // NB_CBLK
    qcol = np.arange(GRID_W).reshape(ncb, NB_CBLK)
    cstart = np.clip(qcol - NB_W // 2, 0, GRID_W - NB_W)
    cs0 = np.clip(np.arange(ncb) * NB_CBLK - NB_W // 2, 0, GRID_W - NB_CSPAN)
    kcol = cs0[:, None] + np.arange(NB_CSPAN)
    col_ok = (kcol[:, None, :] >= cstart[:, :, None]) & (kcol[:, None, :] < cstart[:, :, None] + NB_W)
    dc_idx = np.clip(kcol[:, None, :] - qcol[:, :, None], -(NB_W - 1), NB_W - 1) + NB_W - 1
    scale = D ** -0.5
    kg = k.reshape(B_, rows, GRID_W, H, D)
    vg = v.reshape(B_, rows, GRID_W, H, D)
    qg = q.reshape(B_, rows, ncb, NB_CBLK, H, D).transpose(1, 0, 2, 3, 4, 5)
    nk = kh * NB_CSPAN

    def row_blk(args):
        r, qr = args
        rs = jnp.clip(r - kh // 2, 0, rows - kh)
        kb = lax.dynamic_slice_in_dim(kg, rs, kh, axis=1)[:, :, kcol]
        vb = lax.dynamic_slice_in_dim(vg, rs, kh, axis=1)[:, :, kcol]
        dr_idx = rs + jnp.arange(kh) - r + NB_H - 1
        bias = rpb[:, dr_idx][:, :, dc_idx].transpose(0, 2, 3, 1, 4).astype(F32)
        s_n = jnp.einsum('bjqhd,bkjshd->bhjqks', qr, kb).astype(F32) * scale + bias[None]
        s_n = jnp.where(col_ok[:, :, None, :], s_n, NEG).reshape(B_, H, ncb, NB_CBLK, nk)
        s_c = jnp.einsum('bjqhd,bshd->bhjqs', qr, kc).astype(F32) * scale
        p = jax.nn.softmax(jnp.concatenate([s_n, s_c], -1), -1).astype(v.dtype)
        pn = p[..., :nk].reshape(B_, H, ncb, NB_CBLK, kh, NB_CSPAN)
        return (jnp.einsum('bhjqks,bkjshd->bjqhd', pn, vb)
                + jnp.einsum('bhjqs,bshd->bjqhd', p[..., nk:], vc))

    o = lax.map(row_blk, (jnp.arange(rows), qg))
    return o.transpose(1, 0, 2, 3, 4, 5).reshape(B_, L, H * D)


def delta_chunked(q, k, v, g, beta, s0):
    B_, H, L, _ = q.shape
    n = L // CHUNK
    ck = lambda t: t.reshape(B_, H, n, CHUNK, *t.shape[3:])
    q, k, v, g, beta = ck(q), ck(k), ck(v), ck(g), ck(beta)
    gc = jnp.cumsum(g, -1)
    tri = jnp.tril(jnp.ones((CHUNK, CHUNK), bool))
    strict = jnp.tril(jnp.ones((CHUNK, CHUNK), bool), -1)
    decay = jnp.exp(jnp.where(tri, gc[..., :, None] - gc[..., None, :], NEG))
    kb = k * beta[..., None]
    lmat = jnp.where(strict, jnp.einsum('bhncd,bhnsd->bhncs', kb, k) * decay, 0.0)
    tmat = lmat + jnp.eye(CHUNK, dtype=F32)
    u = lax.linalg.triangular_solve(tmat, v * beta[..., None], left_side=True, lower=True, unit_diagonal=True)
    w = lax.linalg.triangular_solve(tmat, kb * jnp.exp(gc)[..., None], left_side=True, lower=True, unit_diagonal=True)
    attn = jnp.where(tri, jnp.einsum('bhncd,bhnsd->bhncs', q, k) * decay, 0.0)
    mv = lambda t: jnp.moveaxis(t, 2, 0)

    def step(s, xs):
        qi, ki, ui, wi, gi, ai = xs
        vn = ui - jnp.einsum('bhcd,bhde->bhce', wi, s)
        o = jnp.einsum('bhcd,bhde->bhce', qi * jnp.exp(gi)[..., None], s) + jnp.einsum('bhcs,bhse->bhce', ai, vn)
        gl = gi[..., -1]
        s = s * jnp.exp(gl)[..., None, None] + jnp.einsum('bhcd,bhce->bhde', ki * jnp.exp(gl[..., None] - gi)[..., None], vn)
        return s, o

    s, o = lax.scan(step, s0, (mv(q), mv(k), mv(u), mv(w), mv(gc), mv(attn)))
    return jnp.moveaxis(o, 0, 2).reshape(B_, H, L, -1), s


def gla_chunked(q, k, v, gk, s0):
    B_, H, L, _ = q.shape
    n = L // CHUNK
    chunks = lambda t: jnp.moveaxis(t.reshape(B_, H, n, CHUNK, t.shape[-1]), 2, 0)
    tri = jnp.tril(jnp.ones((CHUNK, CHUNK), bool))

    def step(s, xs):
        qi, ki, vi, gi = xs
        b = jnp.cumsum(gi, axis=2)
        inter = jnp.einsum('bhcd,bhde->bhce', qi * jnp.exp(b), s)
        diff = jnp.where(tri[:, :, None], b[:, :, :, None, :] - b[:, :, None, :, :], NEG)
        a = jnp.einsum('bhcd,bhsd,bhcsd->bhcs', qi, ki, jnp.exp(diff))
        intra = jnp.einsum('bhcs,bhse->bhce', a, vi)
        bl = b[:, :, -1]
        s = s * jnp.exp(bl)[..., None] + jnp.einsum('bhcd,bhce->bhde', ki * jnp.exp(bl[:, :, None] - b), vi)
        return s, inter + intra

    s, o = lax.scan(step, s0, (chunks(q), chunks(k), chunks(v), chunks(gk)))
    return jnp.moveaxis(o, 0, 2).reshape(B_, H, L, -1), s


def flip_seq(t):
    return jnp.flip(t, axis=2)


def mixer_delta(pa, s0, conv_w, a_log, dt_bias, onorm):
    B_, L, _ = pa.shape
    qkv = short_conv(pa[..., :3 * W_A], conv_w)
    z = pa[..., 3 * W_A:4 * W_A]
    heads = lambda t, d: t.astype(F32).reshape(B_, L, H_A, d).transpose(0, 2, 1, 3)
    q = l2norm(heads(qkv[..., :W_A], DK_A)) * DK_A ** -0.5
    k = l2norm(heads(qkv[..., W_A:2 * W_A], DK_A))
    v = heads(qkv[..., 2 * W_A:], DV_A)
    dirs = lambda t: t.astype(F32).reshape(B_, L, 2, H_A).transpose(2, 0, 3, 1)
    beta = jax.nn.sigmoid(dirs(pa[..., 4 * W_A:4 * W_A + 2 * H_A]))
    g = -jnp.exp(a_log.astype(F32))[:, None, :, None] * jax.nn.softplus(
        dirs(pa[..., 4 * W_A + 2 * H_A:]) + dt_bias.astype(F32)[:, None, :, None])
    s0 = s0.astype(F32)
    o_f, s_f = delta_chunked(q, k, v, g[0], beta[0], s0[:, 0])
    o_b, s_b = delta_chunked(flip_seq(q), flip_seq(k), flip_seq(v), flip_seq(g[1]), flip_seq(beta[1]), s0[:, 1])
    o = (o_f + flip_seq(o_b)).transpose(0, 2, 1, 3)
    o = rmsnorm(o, onorm).reshape(B_, L, W_A).astype(pa.dtype) * jax.nn.silu(z)
    return o, jnp.stack([s_f, s_b], 1).astype(pa.dtype)


def mixer_gla(pc, s0, w_glr, b_glr, onorm):
    B_, L, _ = pc.shape
    heads = lambda t, d: t.astype(F32).reshape(B_, L, H_C, d).transpose(0, 2, 1, 3)
    q = heads(pc[..., :QK_C], DK_C) * DK_C ** -0.5
    k = heads(pc[..., QK_C:2 * QK_C], DK_C)
    v = heads(pc[..., 2 * QK_C:2 * QK_C + W_C], DV_C)
    z = pc[..., 2 * QK_C + W_C:2 * QK_C + 2 * W_C]
    lr = pc[..., 2 * QK_C + 2 * W_C:].astype(F32).reshape(B_, L, 2, GLA_RANK)
    gk = jax.nn.log_sigmoid(jnp.einsum('blnr,nrd->nbld', lr, w_glr.astype(F32))
                            + b_glr.astype(F32)[:, None, None, :]) / GLA_TAU
    gk = gk.reshape(2, B_, L, H_C, DK_C).transpose(0, 1, 3, 2, 4)
    s0 = s0.astype(F32)
    o_f, s_f = gla_chunked(q, k, v, gk[0], s0[:, 0])
    o_b, s_b = gla_chunked(flip_seq(q), flip_seq(k), flip_seq(v), flip_seq(gk[1]), s0[:, 1])
    o = (o_f + flip_seq(o_b)).transpose(0, 2, 1, 3)
    o = rmsnorm(o, onorm).reshape(B_, L, W_C).astype(pc.dtype) * jax.nn.silu(z)
    return o, jnp.stack([s_f, s_b], 1).astype(pc.dtype)


def even_layer(h, w_in, conv_w, a_log, dt_bias, onorm, sink, s0, ctx_kv):
    B_, L, _ = h.shape
    p = h @ w_in
    o_a, st = mixer_delta(p[..., :PA_EVEN], s0, conv_w, a_log, dt_bias, onorm)
    pb = p[..., PA_EVEN:]
    q = pb[..., :W_B].reshape(B_, L, H_B, HD)
    k = pb[..., W_B:W_B + HKV_B * HD].reshape(B_, L, HKV_B, HD)
    v = pb[..., W_B + HKV_B * HD:W_B + 2 * HKV_B * HD].reshape(B_, L, HKV_B, HD)
    z = pb[..., W_B + 2 * HKV_B * HD:]
    if ctx_kv is None:
        o_b = dense_attn(q, k, v, sink)
        kv = jnp.stack([k, v], 1)
    else:
        o_b = windowed_attn(rope2d(q), rope2d(k), v, ctx_kv[:, 0], ctx_kv[:, 1], sink)
        kv = None
    o_b = o_b * jax.nn.silu(z)
    return jnp.concatenate([o_a, o_b], -1), st, kv


def odd_layer(h, w_in, w_glr, b_glr, onorm, rpb, s0, ctx_kv):
    B_, L, _ = h.shape
    p = h @ w_in
    o_c, st = mixer_gla(p[..., :PC_ODD], s0, w_glr, b_glr, onorm)
    pd = p[..., PC_ODD:]
    q = pd[..., :W_D].reshape(B_, L, H_D, HD)
    k = pd[..., W_D:2 * W_D].reshape(B_, L, H_D, HD)
    v = pd[..., 2 * W_D:3 * W_D].reshape(B_, L, H_D, HD)
    z = pd[..., 3 * W_D:]
    if ctx_kv is None:
        o_d = dense_attn(q, k, v, None)
        kv = jnp.stack([k, v], 1)
    else:
        o_d = neighbourhood_attn(q, k, v, ctx_kv[:, 0], ctx_kv[:, 1], rpb)
        kv = None
    o_d = o_d * jax.nn.silu(z)
    return jnp.concatenate([o_c, o_d], -1), st, kv


def setup_inputs(seed: int = 0) -> dict:
    key = jax.random.key(seed)
    ks = jax.random.split(key, 25)
    nrm = lambda i, shape, s: jax.random.normal(ks[i], shape, F32) * s
    dt = jnp.exp(jax.random.uniform(ks[14], (N_EVEN, 2, H_A), F32, np.log(1e-3), np.log(1e-1)))
    return {
        'x_prompt': nrm(0, (BATCH, SEQ, D_MODEL), 1.0),
        'x_sample': nrm(1, (DEC_BATCH, DEC_SEQ, D_MODEL), 1.0),
        'state_delta': nrm(2, (DEC_BATCH, N_EVEN, 2, H_A, DK_A, DV_A), 0.1),
        'cache_kv_win': nrm(3, (DEC_BATCH, N_EVEN, 2, PAST_LEN, HKV_B, HD), 1.0),
        'state_gla': nrm(4, (DEC_BATCH, N_ODD, 2, H_C, DK_C, DV_C), 0.1),
        'cache_kv_nbr': nrm(5, (DEC_BATCH, N_ODD, 2, PAST_LEN, H_D, HD), 1.0),
        'c': nrm(6, (DEC_BATCH, D_MODEL), 1.0),
        'c_ctx': nrm(7, (D_MODEL,), 1.0),
        'norm_w': 1.0 + nrm(8, (DEPTH, D_MODEL), 0.02),
        'w_ada': nrm(9, (DEPTH, D_MODEL, 3 * D_MODEL), 0.5 * D_MODEL ** -0.5),
        'b_ada': nrm(10, (DEPTH, 3 * D_MODEL), 0.02),
        'w_in_even': nrm(11, (N_EVEN, D_MODEL, P_EVEN), D_MODEL ** -0.5),
        'conv_a': nrm(12, (N_EVEN, CONV_K, 3 * W_A), CONV_K ** -0.5),
        'a_log_a': jnp.log(jax.random.uniform(ks[13], (N_EVEN, 2, H_A), F32, 1.0, 16.0)),
        'dt_bias_a': jnp.log(jnp.expm1(dt)),
        'onorm_a': 1.0 + nrm(15, (N_EVEN, DV_A), 0.02),
        'sink_b': nrm(16, (N_EVEN, H_B), 1.0),
        'w_out_even': nrm(17, (N_EVEN, MIX_EVEN, D_MODEL), MIX_EVEN ** -0.5),
        'w_in_odd': nrm(18, (N_ODD, D_MODEL, P_ODD), D_MODEL ** -0.5),
        'w_glr_c': nrm(19, (N_ODD, 2, GLA_RANK, QK_C), GLA_RANK ** -0.5),
        'b_glr_c': nrm(20, (N_ODD, 2, QK_C), 0.1),
        'onorm_c': 1.0 + nrm(21, (N_ODD, DV_C), 0.02),
        'rpb_d': nrm(22, (N_ODD, H_D, 2 * NB_H - 1, 2 * NB_W - 1), 0.2),
        'w_out_odd': nrm(23, (N_ODD, MIX_ODD, D_MODEL), MIX_ODD ** -0.5),
        'final_norm_w': 1.0 + nrm(24, (D_MODEL,), 0.02),
    }


def reference(x_prompt, x_sample, state_delta, cache_kv_win, state_gla, cache_kv_nbr, c, c_ctx,
              norm_w, w_ada, b_ada, w_in_even, conv_a, a_log_a, dt_bias_a, onorm_a, sink_b, w_out_even,
              w_in_odd, w_glr_c, b_glr_c, onorm_c, rpb_d, w_out_odd, final_norm_w):
    xc, xl = x_prompt, x_sample
    cc = c_ctx[None, None, :]
    cl = c[:, None, :]
    bc = x_prompt.shape[0]
    new_delta, new_kvw, new_gla, new_kvn = [], [], [], []
    for li in range(DEPTH):
        hc, gate_c = adaln(xc, cc, norm_w[li], w_ada[li], b_ada[li])
        hl, gate_l = adaln(xl, cl, norm_w[li], w_ada[li], b_ada[li])
        if li % 2 == 0:
            e = li // 2
            prm = (w_in_even[e], conv_a[e], a_log_a[e], dt_bias_a[e], onorm_a[e], sink_b[e])
            s_zero = jnp.zeros((bc, 2, H_A, DK_A, DV_A), F32)
            oc, st, kv = even_layer(hc, *prm, s_zero, None)
            ol, _, _ = even_layer(hl, *prm, state_delta[:, e], cache_kv_win[:, e])
            new_delta.append(st)
            new_kvw.append(kv)
            w_out = w_out_even[e]
        else:
            o_i = li // 2
            prm = (w_in_odd[o_i], w_glr_c[o_i], b_glr_c[o_i], onorm_c[o_i], rpb_d[o_i])
            s_zero = jnp.zeros((bc, 2, H_C, DK_C, DV_C), F32)
            oc, st, kv = odd_layer(hc, *prm, s_zero, None)
            ol, _, _ = odd_layer(hl, *prm, state_gla[:, o_i], cache_kv_nbr[:, o_i])
            new_gla.append(st)
            new_kvn.append(kv)
            w_out = w_out_odd[o_i]
        xc = xc + gate_c * (oc @ w_out)
        xl = xl + gate_l * (ol @ w_out)
    y_prompt = rmsnorm(xc, final_norm_w)
    y_sample = rmsnorm(xl, final_norm_w)
    return (y_prompt, y_sample, jnp.stack(new_delta, 1), jnp.stack(new_kvw, 1),
            jnp.stack(new_gla, 1), jnp.stack(new_kvn, 1))
```

```python
import functools

import numpy as np
import jax
import jax.numpy as jnp
from jax import lax
from jax.experimental import pallas as pl
from jax.experimental.pallas import tpu as pltpu

F32 = jnp.float32
BF16 = jnp.bfloat16
HIGHEST = lax.Precision.HIGHEST

D_MODEL = 2048
BATCH = 16
SEQ = 256
DEPTH = 4
DEC_BATCH = 4
DEC_SEQ = 1024
PAST_LEN = 256
GRID_W = 64
HD = 128
EPS = 1e-6
NEG = -1e30
ROPE_THETA = 10000.0
CHUNK = 64
H_A = 8
W_A = H_A * HD
CONV_K = 5
H_B = 8
HKV_B = 2
W_B = H_B * HD
WIN = 128
QBLK = 128
H_C = 4
DK_C = 128
DV_C = 256
QK_C = H_C * DK_C
W_C = H_C * DV_C
GLA_RANK = 16
GLA_TAU = 16.0
H_D = 8
W_D = H_D * HD
NB_H = 8
NB_W = 16
N_EVEN = (DEPTH + 1) // 2
N_ODD = DEPTH // 2
PA_EVEN = 4 * W_A + 4 * H_A
PC_ODD = 2 * QK_C + 2 * W_C + 2 * GLA_RANK

N_CTX = BATCH * SEQ
N_LAT = DEC_BATCH * DEC_SEQ
N_TOK = N_CTX + N_LAT
N_COND = 8
LANE = 128
TM_IN = 512
TN_IN = 512
TM_OUT = 256
TN_ADA = 512
VMEM_LIMIT = 56 * 1024 * 1024


def _cparams(sem):
    return pltpu.CompilerParams(dimension_semantics=sem, vmem_limit_bytes=VMEM_LIMIT)


def _sigmoid(x):
    return 1.0 / (1.0 + jnp.exp(-x))


def _silu(x):
    return x * _sigmoid(x)


def _dot(a, b):
    return jnp.dot(a.astype(BF16), b.astype(BF16), preferred_element_type=F32)


def _dot_nt(a, b):
    return lax.dot_general(a.astype(BF16), b.astype(BF16), (((1,), (1,)), ((), ())),
                           preferred_element_type=F32)


def _dot_tn(a, b):
    return lax.dot_general(a.astype(BF16), b.astype(BF16), (((0,), (0,)), ((), ())),
                           preferred_element_type=F32)


def _dot_hi(a, b):
    return jnp.dot(a, b, precision=HIGHEST, preferred_element_type=F32)


def _ada_kernel(c_ref, w_ref, b_ref, o_ref):
    o_ref[...] = _dot(_silu(c_ref[...]), w_ref[...]) + b_ref[...]


def _ada_mod(cond, w_ada, b_ada):
    n3 = 3 * D_MODEL
    return pl.pallas_call(
        _ada_kernel,
        grid=(DEPTH, n3 // TN_ADA),
        in_specs=[
            pl.BlockSpec((N_COND, D_MODEL), lambda l, j: (0, 0)),
            pl.BlockSpec((None, D_MODEL, TN_ADA), lambda l, j: (l, 0, j)),
            pl.BlockSpec((None, 1, TN_ADA), lambda l, j: (l, 0, j)),
        ],
        out_specs=pl.BlockSpec((None, N_COND, TN_ADA), lambda l, j: (l, 0, j)),
        out_shape=jax.ShapeDtypeStruct((DEPTH, N_COND, n3), F32),
        compiler_params=_cparams(("arbitrary", "arbitrary")),
        name="ada_mod",
    )(cond, w_ada, b_ada.reshape(DEPTH, 1, n3))


def _cond_row(i, tm):
    nct = N_CTX // tm
    per = DEC_SEQ // tm
    return jnp.where(i < nct, 0, 1 + (i - nct) // per)


def _inproj_kernel(x_ref, nw_ref, shift_ref, scale_ref, w_ref, ws_ref, o_ref, os_ref, h_ref):
    @pl.when(pl.program_id(1) == 0)
    def _():
        x = x_ref[...]
        y = x * lax.rsqrt(jnp.mean(x * x, axis=-1, keepdims=True) + EPS) * nw_ref[...]
        h = (y * (1.0 + scale_ref[...]) + shift_ref[...]).astype(BF16)
        h_ref[...] = h
        os_ref[...] = jnp.dot(h, ws_ref[...], preferred_element_type=F32)

    o_ref[...] = jnp.dot(h_ref[...], w_ref[...], preferred_element_type=F32)


def _in_proj(x, norm_w, mod, w_main, w_small):
    n = w_main.shape[1]
    row = functools.partial(_cond_row, tm=TM_IN)
    return pl.pallas_call(
        _inproj_kernel,
        grid=(N_TOK // TM_IN, n // TN_IN),
        in_specs=[
            pl.BlockSpec((TM_IN, D_MODEL), lambda i, j: (i, 0)),
            pl.BlockSpec((1, D_MODEL), lambda i, j: (0, 0)),
            pl.BlockSpec((None, 1, D_MODEL), lambda i, j: (row(i), 0, 0)),
            pl.BlockSpec((None, 1, D_MODEL), lambda i, j: (row(i), 0, 1)),
            pl.BlockSpec((D_MODEL, TN_IN), lambda i, j: (0, j)),
            pl.BlockSpec((D_MODEL, LANE), lambda i, j: (0, 0)),
        ],
        out_specs=[
            pl.BlockSpec((TM_IN, TN_IN), lambda i, j: (i, j)),
            pl.BlockSpec((TM_IN, LANE), lambda i, j: (i, 0)),
        ],
        out_shape=[jax.ShapeDtypeStruct((N_TOK, n), F32),
                   jax.ShapeDtypeStruct((N_TOK, LANE), F32)],
        scratch_shapes=[pltpu.VMEM((TM_IN, D_MODEL), BF16)],
        compiler_params=_cparams(("arbitrary", "arbitrary")),
        name="in_proj",
    )(x, norm_w.reshape(1, D_MODEL), mod, mod, w_main, w_small)


def _outproj_kernel(oa_ref, ob_ref, wa_ref, wb_ref, x_ref, g_ref, *rest, final):
    acc = _dot(oa_ref[...], wa_ref[...]) + _dot(ob_ref[...], wb_ref[...])
    xn = x_ref[...] + g_ref[...] * acc
    if final:
        fw_ref, y_ref = rest
        y_ref[...] = xn * lax.rsqrt(jnp.mean(xn * xn, axis=-1, keepdims=True) + EPS) * fw_ref[...]
    else:
        (y_ref,) = rest
        y_ref[...] = xn


def _out_proj(o_a, o_b, w_a, w_b, x, mod, final_w=None):
    ka, kb = o_a.shape[1], o_b.shape[1]
    row = functools.partial(_cond_row, tm=TM_OUT)
    final = final_w is not None
    in_specs = [
        pl.BlockSpec((TM_OUT, ka), lambda i: (i, 0)),
        pl.BlockSpec((TM_OUT, kb), lambda i: (i, 0)),
        pl.BlockSpec((ka, D_MODEL), lambda i: (0, 0)),
        pl.BlockSpec((kb, D_MODEL), lambda i: (0, 0)),
        pl.BlockSpec((TM_OUT, D_MODEL), lambda i: (i, 0)),
        pl.BlockSpec((None, 1, D_MODEL), lambda i: (row(i), 0, 2)),
    ]
    args = [o_a, o_b, w_a, w_b, x, mod]
    if final:
        in_specs.append(pl.BlockSpec((1, D_MODEL), lambda i: (0, 0)))
        args.append(final_w.reshape(1, D_MODEL))
    return pl.pallas_call(
        functools.partial(_outproj_kernel, final=final),
        grid=(N_TOK // TM_OUT,),
        in_specs=in_specs,
        out_specs=pl.BlockSpec((TM_OUT, D_MODEL), lambda i: (i, 0)),
        out_shape=jax.ShapeDtypeStruct((N_TOK, D_MODEL), F32),
        compiler_params=_cparams(("arbitrary",)),
        name="out_proj_final" if final else "out_proj",
    )(*args)


def _attend(q, ks, vs, masks, biases, sink, scale):
    ss = []
    for k, m, bias in zip(ks, masks, biases):
        s = _dot_nt(q, k) * scale
        if bias is not None:
            s = s + bias
        if m is not None:
            s = jnp.where(m, s, NEG)
        ss.append(s)
    mx = functools.reduce(jnp.maximum, [jnp.max(s, axis=-1, keepdims=True) for s in ss])
    if sink is not None:
        mx = jnp.maximum(mx, sink)
    es = [jnp.exp(s - mx) for s in ss]
    den = functools.reduce(jnp.add, [jnp.sum(e, axis=-1, keepdims=True) for e in es])
    if sink is not None:
        den = den + jnp.exp(sink - mx)
    o = functools.reduce(jnp.add, [_dot(e, v) for e, v in zip(es, vs)])
    return o / den


def _head(ref, h, rows=None):
    if rows is None:
        return ref[:, h * HD:(h + 1) * HD]
    return ref[rows, h * HD:(h + 1) * HD]


def _ctx_attn_kernel(*refs, heads, kv_heads, use_sink):
    if use_sink:
        q_ref, k_ref, v_ref, z_ref, sink_ref, o_ref = refs
    else:
        q_ref, k_ref, v_ref, z_ref, o_ref = refs
    g = heads // kv_heads
    n = q_ref.shape[0]
    scale = HD ** -0.5
    for j in range(kv_heads):
        q = jnp.concatenate([_head(q_ref, j * g + t) for t in range(g)], axis=0)
        sink = None
        if use_sink:
            sink = jnp.concatenate([jnp.full((n, 1), sink_ref[j * g + t], F32) for t in range(g)], axis=0)
        o = _attend(q, [_head(k_ref, j)], [_head(v_ref, j)], [None], [None], sink, scale)
        for t in range(g):
            h = j * g + t
            o_ref[:, h * HD:(h + 1) * HD] = o[t * n:(t + 1) * n] * _silu(_head(z_ref, h))


def _ctx_attn(p, q_col, k_col, v_col, z_col, heads, kv_heads, sink):
    wq, wkv = heads * HD, kv_heads * HD
    use_sink = sink is not None
    in_specs = [
        pl.BlockSpec((SEQ, wq), lambda b: (b, q_col)),
        pl.BlockSpec((SEQ, wkv), lambda b: (b, k_col)),
        pl.BlockSpec((SEQ, wkv), lambda b: (b, v_col)),
        pl.BlockSpec((SEQ, wq), lambda b: (b, z_col)),
    ]
    args = [p, p, p, p]
    if use_sink:
        in_specs.append(pl.BlockSpec(memory_space=pltpu.SMEM))
        args.append(sink)
    return pl.pallas_call(
        functools.partial(_ctx_attn_kernel, heads=heads, kv_heads=kv_heads, use_sink=use_sink),
        grid=(BATCH,),
        in_specs=in_specs,
        out_specs=pl.BlockSpec((SEQ, wq), lambda b: (b, 0)),
        out_shape=jax.ShapeDtypeStruct((N_CTX, wq), F32),
        compiler_params=_cparams(("arbitrary",)),
        name="ctx_attn_sink" if use_sink else "ctx_attn",
    )(*args)


def _rope_tables():
    half = HD // 4
    freq = (ROPE_THETA ** (-np.arange(half, dtype=np.float32) / half)).astype(np.float32)
    t = np.arange(DEC_SEQ)
    ang_r = (t // GRID_W).astype(np.float32)[:, None] * freq[None, :]
    ang_c = (t % GRID_W).astype(np.float32)[:, None] * freq[None, :]
    cos = np.concatenate([np.cos(ang_r)] * 2 + [np.cos(ang_c)] * 2, axis=1).astype(np.float32)
    sin_r, sin_c, zero = np.sin(ang_r), np.sin(ang_c), np.zeros_like(ang_r)
    s_up = np.concatenate([-sin_r, zero, -sin_c, zero], axis=1).astype(np.float32)
    s_dn = np.concatenate([zero, sin_r, zero, sin_c], axis=1).astype(np.float32)
    return jnp.asarray(cos), jnp.asarray(s_up), jnp.asarray(s_dn)


def _rope(x, cos, s_up, s_dn):
    return x * cos + pltpu.roll(x, HD - HD // 4, 1) * s_up + pltpu.roll(x, HD // 4, 1) * s_dn


def _win_attn_kernel(q_ref, k_ref, v_ref, kc_ref, vc_ref, z_ref, cos_ref, sup_ref, sdn_ref, sink_ref, o_ref):
    i = pl.program_id(1)
    g = H_B // HKV_B
    span = QBLK + 2 * WIN
    scale = HD ** -0.5
    start = pl.multiple_of(jnp.clip(i * QBLK - WIN, 0, DEC_SEQ - span), QBLK)
    qrows = pl.ds(pl.multiple_of(i * QBLK, QBLK), QBLK)
    krows = pl.ds(start, span)
    cq, uq, dq = cos_ref[qrows, :], sup_ref[qrows, :], sdn_ref[qrows, :]
    ck, uk, dk = cos_ref[krows, :], sup_ref[krows, :], sdn_ref[krows, :]
    qpos = i * QBLK + lax.broadcasted_iota(jnp.int32, (g * QBLK, span), 0) % QBLK
    kpos = start + lax.broadcasted_iota(jnp.int32, (g * QBLK, span), 1)
    band = jnp.abs(qpos - kpos) <= WIN
    for j in range(HKV_B):
        q = jnp.concatenate([_rope(_head(q_ref, j * g + t), cq, uq, dq) for t in range(g)], axis=0)
        kw = _rope(_head(k_ref, j, krows), ck, uk, dk)
        vw = _head(v_ref, j, krows)
        sink = jnp.concatenate([jnp.full((QBLK, 1), sink_ref[j * g + t], F32) for t in range(g)], axis=0)
        o = _attend(q, [kw, _head(kc_ref, j)], [vw, _head(vc_ref, j)], [band, None], [None, None], sink, scale)
        for t in range(g):
            h = j * g + t
            o_ref[:, h * HD:(h + 1) * HD] = o[t * QBLK:(t + 1) * QBLK] * _silu(_head(z_ref, h))


def _win_attn(p, cache_k, cache_v, sink, q_col, k_col, v_col, z_col):
    wq, wkv = H_B * HD, HKV_B * HD
    nq = DEC_SEQ // QBLK
    lat = N_CTX // DEC_SEQ
    cos, s_up, s_dn = _rope_tables()
    full = pl.BlockSpec((DEC_SEQ, HD), lambda b, i: (0, 0))
    return pl.pallas_call(
        _win_attn_kernel,
        grid=(DEC_BATCH, nq),
        in_specs=[
            pl.BlockSpec((QBLK, wq), lambda b, i: (N_CTX // QBLK + b * nq + i, q_col)),
            pl.BlockSpec((DEC_SEQ, wkv), lambda b, i: (lat + b, k_col)),
            pl.BlockSpec((DEC_SEQ, wkv), lambda b, i: (lat + b, v_col)),
            pl.BlockSpec((None, PAST_LEN, wkv), lambda b, i: (b, 0, 0)),
            pl.BlockSpec((None, PAST_LEN, wkv), lambda b, i: (b, 0, 0)),
            pl.BlockSpec((QBLK, wq), lambda b, i: (N_CTX // QBLK + b * nq + i, z_col)),
            full, full, full,
            pl.BlockSpec(memory_space=pltpu.SMEM),
        ],
        out_specs=pl.BlockSpec((QBLK, wq), lambda b, i: (b * nq + i, 0)),
        out_shape=jax.ShapeDtypeStruct((N_LAT, wq), F32),
        compiler_params=_cparams(("arbitrary", "arbitrary")),
        name="win_attn",
    )(p, p, p, cache_k, cache_v, p, cos, s_up, s_dn, sink)


def _nbr_onehot():
    qc = np.arange(GRID_W)[:, None]
    kc = np.arange(GRID_W)[None, :]
    idx = np.clip(kc - qc, -(NB_W - 1), NB_W - 1) + NB_W - 1
    e = (np.arange(2 * NB_W)[:, None, None] == idx[None]).astype(np.float32)
    return jnp.asarray(e.reshape(2 * NB_W, GRID_W * GRID_W))


def _bias_expand_kernel(r_ref, e_ref, o_ref):
    o_ref[...] = _dot_hi(r_ref[...], e_ref[...])


def _nbr_bias_table(rpb):
    rows = H_D * (2 * NB_H - 1)
    r = jnp.pad(rpb.reshape(rows, 2 * NB_W - 1), ((0, 128 - rows), (0, 1)))
    t = pl.pallas_call(
        _bias_expand_kernel,
        out_shape=jax.ShapeDtypeStruct((128, GRID_W * GRID_W), F32),
        name="nbr_bias_expand",
    )(r, _nbr_onehot())
    t = t[:rows].reshape(H_D, 2 * NB_H - 1, GRID_W, GRID_W)
    return jnp.concatenate([t[:, :-1], t[:, 1:]], axis=-1)


def _nbr_attn_kernel(q_ref, k_ref, v_ref, kc_ref, vc_ref, z_ref, t_ref, o_ref):
    r = pl.program_id(1)
    rows = DEC_SEQ // GRID_W
    rs = jnp.clip(r - NB_H // 2, 0, rows - NB_H)
    dr0 = rs - r + NB_H - 1
    nk = NB_H * GRID_W
    krows = pl.ds(pl.multiple_of(rs * GRID_W, GRID_W), nk)
    scale = HD ** -0.5
    qc = lax.broadcasted_iota(jnp.int32, (GRID_W, nk), 0)
    kc = lax.broadcasted_iota(jnp.int32, (GRID_W, nk), 1) % GRID_W
    cstart = jnp.clip(qc - NB_W // 2, 0, GRID_W - NB_W)
    ok = (kc >= cstart) & (kc < cstart + NB_W)
    for h in range(H_D):
        bias = jnp.concatenate([t_ref[h, dr0 + 2 * m] for m in range(nk // LANE)], axis=1)
        o = _attend(_head(q_ref, h), [_head(k_ref, h, krows), _head(kc_ref, h)],
                    [_head(v_ref, h, krows), _head(vc_ref, h)], [ok, None], [bias, None], None, scale)
        o_ref[:, h * HD:(h + 1) * HD] = o * _silu(_head(z_ref, h))


def _nbr_attn(p, cache_k, cache_v, table, q_col, k_col, v_col, z_col):
    rows = DEC_SEQ // GRID_W
    lat = N_CTX // DEC_SEQ
    return pl.pallas_call(
        _nbr_attn_kernel,
        grid=(DEC_BATCH, rows),
        in_specs=[
            pl.BlockSpec((GRID_W, W_D), lambda b, r: (N_CTX // GRID_W + b * rows + r, q_col)),
            pl.BlockSpec((DEC_SEQ, W_D), lambda b, r: (lat + b, k_col)),
            pl.BlockSpec((DEC_SEQ, W_D), lambda b, r: (lat + b, v_col)),
            pl.BlockSpec((None, PAST_LEN, W_D), lambda b, r: (b, 0, 0)),
            pl.BlockSpec((None, PAST_LEN, W_D), lambda b, r: (b, 0, 0)),
            pl.BlockSpec((GRID_W, W_D), lambda b, r: (N_CTX // GRID_W + b * rows + r, z_col)),
            pl.BlockSpec(table.shape, lambda b, r: (0, 0, 0, 0)),
        ],
        out_specs=pl.BlockSpec((GRID_W, W_D), lambda b, r: (b * rows + r, 0)),
        out_shape=jax.ShapeDtypeStruct((N_LAT, W_D), F32),
        compiler_params=_cparams(("arbitrary", "arbitrary")),
        name="nbr_attn",
    )(p, p, p, cache_k, cache_v, p, table)


def _chunk_masks(reverse):
    c = lax.broadcasted_iota(jnp.int32, (CHUNK, CHUNK), 0)
    j = lax.broadcasted_iota(jnp.int32, (CHUNK, CHUNK), 1)
    if reverse:
        return j >= c, j <= c, j > c, j == c
    return j <= c, j >= c, j < c, j == c


def _lane_col(x, idx):
    lane = lax.broadcasted_iota(jnp.int32, x.shape, 1)
    return jnp.sum(jnp.where(lane == idx, x, 0.0), axis=-1, keepdims=True)


def _short_conv(x_ref, w_ref, pad_ref, n):
    pad = CONV_K // 2
    zeros = jnp.zeros((8, HD), F32)
    pad_ref[0:8, :] = zeros
    pad_ref[n + 8:n + 16, :] = zeros
    pad_ref[8:n + 8, :] = x_ref[...]
    y = functools.reduce(jnp.add, [pad_ref[8 - pad + t:8 - pad + t + n, :] * w_ref[t:t + 1, :]
                                   for t in range(CONV_K)])
    return _silu(y)


def _l2norm(x):
    return x * lax.rsqrt(jnp.sum(x * x, axis=-1, keepdims=True) + EPS)


def _delta_chunk(q, k, v, g, beta, s, reverse):
    tri, tri_t, strict, eye = _chunk_masks(reverse)
    gb = jnp.broadcast_to(g, (CHUNK, HD))
    gc = _dot_hi(jnp.where(tri, 1.0, 0.0), gb)
    gc_row = _dot_hi(jnp.ones((CHUNK, CHUNK), F32), jnp.where(tri_t, gb[:, :CHUNK], 0.0))
    decay = jnp.exp(jnp.where(tri, gc[:, :CHUNK] - gc_row, NEG))
    kb = k * beta
    lmat = jnp.where(strict, _dot_nt(kb, k) * decay, 0.0)
    tinv = jnp.where(eye, 1.0, 0.0) - lmat
    lp = lmat
    for _ in range(5):
        lp = _dot_hi(lp, lp)
        tinv = tinv + _dot_hi(lp, tinv)
    u = _dot_hi(tinv, v * beta)
    w = _dot_hi(tinv, kb * jnp.exp(gc))
    attn = jnp.where(tri, _dot_nt(q, k) * decay, 0.0)
    vn = u - _dot(w, s)
    o = _dot(q * jnp.exp(gc), s) + _dot(attn, vn)
    last = 0 if reverse else CHUNK - 1
    gl = gc[last:last + 1, :]
    s_new = s * jnp.exp(gl) + _dot_tn(k * jnp.exp(gl - gc), vn)
    return o, s_new


def _delta_kernel(*refs, n, has_s0):
    if has_s0:
        (qp_ref, kp_ref, vp_ref, z_ref, sm_ref, cq_ref, ck_ref, cv_ref, al_ref, dt_ref, on_ref, s0_ref,
         o_ref, so_ref, q_s, k_s, v_s, pad_s, acc_s, st_s) = refs
    else:
        (qp_ref, kp_ref, vp_ref, z_ref, sm_ref, cq_ref, ck_ref, cv_ref, al_ref, dt_ref, on_ref,
         o_ref, so_ref, q_s, k_s, v_s, pad_s, acc_s, st_s) = refs
    h = pl.program_id(1)
    nc = n // CHUNK
    q_s[...] = _l2norm(_short_conv(qp_ref, cq_ref, pad_s, n)) * HD ** -0.5
    k_s[...] = _l2norm(_short_conv(kp_ref, ck_ref, pad_s, n))
    v_s[...] = _short_conv(vp_ref, cv_ref, pad_s, n)
    sm = sm_ref[...]
    x = sm + dt_ref[...]
    softplus = jnp.maximum(x, 0.0) + jnp.log(1.0 + jnp.exp(-jnp.abs(x)))
    gates = -jnp.exp(al_ref[...]) * softplus
    betas = _sigmoid(sm)
    lane = lax.broadcasted_iota(jnp.int32, (n, HD), 1)
    cols = (jnp.where(lane == 0, _lane_col(betas, h), 0.0) + jnp.where(lane == 1, _lane_col(betas, H_A + h), 0.0)
            + jnp.where(lane == 2, _lane_col(gates, 2 * H_A + h), 0.0)
            + jnp.where(lane == 3, _lane_col(gates, 3 * H_A + h), 0.0))
    pad_s[0:n, :] = cols
    acc_s[...] = jnp.zeros((n, HD), F32)
    if has_s0:
        st_s[...] = s0_ref[...]
    else:
        st_s[...] = jnp.zeros((2, HD, HD), F32)

    def body(i, carry):
        for d in range(2):
            c = (nc - 1 - i) if d else i
            rows = pl.ds(pl.multiple_of(c * CHUNK, CHUNK), CHUNK)
            gcols = pad_s[rows, :]
            o, s_new = _delta_chunk(q_s[rows, :], k_s[rows, :], v_s[rows, :],
                                    gcols[:, 2 + d:3 + d], gcols[:, d:d + 1], st_s[d], bool(d))
            st_s[d] = s_new
            acc_s[rows, :] += o
        return carry

    lax.fori_loop(0, nc, body, 0)
    o = acc_s[...]
    o = o * lax.rsqrt(jnp.mean(o * o, axis=-1, keepdims=True) + EPS) * on_ref[...]
    o_ref[...] = o * _silu(z_ref[...])
    so_ref[...] = st_s[...]


def _delta_mixer(p, small, conv_w, a_row, dt_row, onorm, s0, seq, nb, row0):
    r0 = row0 // seq
    has_s0 = s0 is not None
    col = lambda off: (lambda b, h: (r0 + b, off + h))
    in_specs = [
        pl.BlockSpec((seq, HD), col(0)),
        pl.BlockSpec((seq, HD), col(H_A)),
        pl.BlockSpec((seq, HD), col(2 * H_A)),
        pl.BlockSpec((seq, HD), col(3 * H_A)),
        pl.BlockSpec((seq, LANE), lambda b, h: (r0 + b, 0)),
        pl.BlockSpec((CONV_K, HD), lambda b, h: (0, h)),
        pl.BlockSpec((CONV_K, HD), lambda b, h: (0, H_A + h)),
        pl.BlockSpec((CONV_K, HD), lambda b, h: (0, 2 * H_A + h)),
        pl.BlockSpec((1, LANE), lambda b, h: (0, 0)),
        pl.BlockSpec((1, LANE), lambda b, h: (0, 0)),
        pl.BlockSpec((1, HD), lambda b, h: (0, 0)),
    ]
    args = [p, p, p, p, small, conv_w, conv_w, conv_w, a_row, dt_row, onorm.reshape(1, HD)]
    if has_s0:
        in_specs.append(pl.BlockSpec((None, 2, None, HD, HD), lambda b, h: (b, 0, h, 0, 0)))
        args.append(s0)
    return pl.pallas_call(
        functools.partial(_delta_kernel, n=seq, has_s0=has_s0),
        grid=(nb, H_A),
        in_specs=in_specs,
        out_specs=[
            pl.BlockSpec((seq, HD), lambda b, h: (b, h)),
            pl.BlockSpec((None, 2, None, HD, HD), lambda b, h: (b, 0, h, 0, 0)),
        ],
        out_shape=[jax.ShapeDtypeStruct((nb * seq, W_A), F32),
                   jax.ShapeDtypeStruct((nb, 2, H_A, HD, HD), F32)],
        scratch_shapes=[pltpu.VMEM((seq, HD), F32), pltpu.VMEM((seq, HD), F32), pltpu.VMEM((seq, HD), F32),
                        pltpu.VMEM((seq + 16, HD), F32), pltpu.VMEM((seq, HD), F32),
                        pltpu.VMEM((2, HD, HD), F32)],
        compiler_params=_cparams(("arbitrary", "arbitrary")),
        name="delta_lat" if has_s0 else "delta_ctx",
    )(*args)


_GLA_LEVELS = (32, 16, 8, 4, 2, 1)


def _gla_consts(reverse):
    r = np.arange(CHUNK)
    flip = (lambda a: a[::-1, ::-1]) if reverse else (lambda a: a)
    tri = flip(r[None, :] <= r[:, None])
    dq, dk, hi, lo, pair = [], [], [], [], []
    for m in _GLA_LEVELS:
        mid = (r // (2 * m)) * (2 * m) + m
        is_hi = r >= mid
        dq.append(flip(is_hi[:, None] & (r[None, :] >= mid[:, None]) & (r[None, :] <= r[:, None])))
        dk.append(flip(~is_hi[:, None] & (r[None, :] > r[:, None]) & (r[None, :] < mid[:, None])))
        hi.append(flip(is_hi[:, None]))
        lo.append(flip(~is_hi[:, None]))
        pair.append(flip((r[:, None] // (2 * m) == r[None, :] // (2 * m)) & is_hi[:, None] & ~is_hi[None, :]))
    out = [tri, np.concatenate(dq), np.concatenate(dk), np.concatenate(hi), np.concatenate(lo), np.stack(pair)]
    return [jnp.asarray(np.ascontiguousarray(a).astype(np.float32)) for a in out]


def _gla_chunk(q, k, v, gk, s, consts, reverse):
    tri_ref, dq_ref, dk_ref, hi_ref, lo_ref, pair_ref = consts
    b = _dot_hi(tri_ref[...], gk)
    xq = _dot_hi(dq_ref[...], gk)
    xk = _dot_hi(dk_ref[...], gk)
    a = jnp.where(_chunk_masks(reverse)[3], jnp.sum(q * k, axis=-1, keepdims=True), 0.0)
    for lv in range(len(_GLA_LEVELS)):
        rows = slice(lv * CHUNK, (lv + 1) * CHUNK)
        ql = q * jnp.exp(xq[rows]) * hi_ref[rows, :]
        kl = k * jnp.exp(xk[rows]) * lo_ref[rows, :]
        a = a + _dot_nt(ql, kl) * pair_ref[lv]
    o = _dot(q * jnp.exp(b), s) + _dot(a, v)
    last = 0 if reverse else CHUNK - 1
    bl = b[last:last + 1, :]
    r = lax.broadcasted_iota(jnp.int32, (DK_C, DK_C), 0)
    c = lax.broadcasted_iota(jnp.int32, (DK_C, DK_C), 1)
    dec = jnp.sum(jnp.where(r == c, jnp.broadcast_to(jnp.exp(bl), (DK_C, DK_C)), 0.0), axis=-1, keepdims=True)
    s_new = s * dec + _dot_tn(k * jnp.exp(bl - b), v)
    return o, s_new


def _gla_kernel(*refs, n, has_s0):
    consts_f, consts_b = refs[8:14], refs[14:20]
    q_ref, k_ref, v_ref, z_ref, sm_ref, wg_ref, bg_ref, on_ref = refs[:8]
    if has_s0:
        s0_ref, o_ref, so_ref, gk_s, acc_s, st_s = refs[20:]
    else:
        o_ref, so_ref, gk_s, acc_s, st_s = refs[20:]
    nc = n // CHUNK
    sm = sm_ref[...]
    for d in range(2):
        x = _dot_hi(sm, wg_ref[d]) + bg_ref[d]
        gk_s[d] = (jnp.minimum(x, 0.0) - jnp.log(1.0 + jnp.exp(-jnp.abs(x)))) / GLA_TAU
    acc_s[...] = jnp.zeros((n, DV_C), F32)
    if has_s0:
        st_s[...] = s0_ref[...]
    else:
        st_s[...] = jnp.zeros((2, DK_C, DV_C), F32)

    def body(i, carry):
        for d in range(2):
            c = (nc - 1 - i) if d else i
            rows = pl.ds(pl.multiple_of(c * CHUNK, CHUNK), CHUNK)
            o, s_new = _gla_chunk(q_ref[rows, :] * DK_C ** -0.5, k_ref[rows, :], v_ref[rows, :], gk_s[d, rows, :],
                                  st_s[d], consts_b if d else consts_f, bool(d))
            st_s[d] = s_new
            acc_s[rows, :] += o
        return carry

    lax.fori_loop(0, nc, body, 0)
    o = acc_s[...]
    o = o * lax.rsqrt(jnp.mean(o * o, axis=-1, keepdims=True) + EPS) * on_ref[...]
    o_ref[...] = o * _silu(z_ref[...])
    so_ref[...] = st_s[...]


def _gla_mixer(p, small, w_gate, b_gate, onorm, s0, seq, nb, row0):
    r0 = row0 // seq
    has_s0 = s0 is not None
    consts = _gla_consts(False) + _gla_consts(True)
    const_specs = [pl.BlockSpec(a.shape, (lambda b, h, nd=a.ndim: (0,) * nd)) for a in consts]
    in_specs = [
        pl.BlockSpec((seq, DK_C), lambda b, h: (r0 + b, h)),
        pl.BlockSpec((seq, DK_C), lambda b, h: (r0 + b, QK_C // DK_C + h)),
        pl.BlockSpec((seq, DV_C), lambda b, h: (r0 + b, 2 * QK_C // DV_C + h)),
        pl.BlockSpec((seq, DV_C), lambda b, h: (r0 + b, (2 * QK_C + W_C) // DV_C + h)),
        pl.BlockSpec((seq, LANE), lambda b, h: (r0 + b, 0)),
        pl.BlockSpec((2, LANE, DK_C), lambda b, h: (0, 0, h)),
        pl.BlockSpec((2, 1, DK_C), lambda b, h: (0, 0, h)),
        pl.BlockSpec((1, DV_C), lambda b, h: (0, 0)),
    ] + const_specs
    args = [p, p, p, p, small, w_gate, b_gate.reshape(2, 1, QK_C), onorm.reshape(1, DV_C)] + consts
    if has_s0:
        in_specs.append(pl.BlockSpec((None, 2, None, DK_C, DV_C), lambda b, h: (b, 0, h, 0, 0)))
        args.append(s0)
    return pl.pallas_call(
        functools.partial(_gla_kernel, n=seq, has_s0=has_s0),
        grid=(nb, H_C),
        in_specs=in_specs,
        out_specs=[
            pl.BlockSpec((seq, DV_C), lambda b, h: (b, h)),
            pl.BlockSpec((None, 2, None, DK_C, DV_C), lambda b, h: (b, 0, h, 0, 0)),
        ],
        out_shape=[jax.ShapeDtypeStruct((nb * seq, W_C), F32),
                   jax.ShapeDtypeStruct((nb, 2, H_C, DK_C, DV_C), F32)],
        scratch_shapes=[pltpu.VMEM((2, seq, DK_C), F32), pltpu.VMEM((seq, DV_C), F32),
                        pltpu.VMEM((2, DK_C, DV_C), F32)],
        compiler_params=_cparams(("arbitrary", "arbitrary")),
        name="gla_lat" if has_s0 else "gla_ctx",
    )(*args)


_EV_QB = 4 * W_A
_EV_ZB = _EV_QB + W_B
_EV_KB = _EV_ZB + W_B
_EV_VB = _EV_KB + HKV_B * HD
_OD_QD = 2 * QK_C + 2 * W_C
_OD_KD = _OD_QD + W_D
_OD_VD = _OD_KD + W_D
_OD_ZD = _OD_VD + W_D


def _regroup(w, pieces, small):
    main = jnp.concatenate([w[:, lo:hi] for lo, hi in pieces], axis=1).astype(BF16)
    lo, hi = small
    return main, jnp.pad(w[:, lo:hi], ((0, 0), (0, LANE - (hi - lo)))).astype(BF16)


def _lane_row(v, offset):
    return jnp.pad(v.reshape(1, -1), ((0, 0), (offset, LANE - offset - v.size)))


def _even_layer(x, e, mod, norm_w, w_in, conv_a, a_log, dt_bias, onorm, sink, w_out, state_delta, cache_kv,
                final_w):
    qb = PA_EVEN
    kb = qb + W_B
    vb = kb + HKV_B * HD
    zb = vb + HKV_B * HD
    w_main, w_small = _regroup(w_in, [(0, 4 * W_A), (qb, kb), (zb, zb + W_B), (kb, vb), (vb, zb)],
                               (4 * W_A, PA_EVEN))
    p, small = _in_proj(x, norm_w, mod, w_main, w_small)
    a_row = _lane_row(a_log, 2 * H_A)
    dt_row = _lane_row(dt_bias, 2 * H_A)
    oa_c, st = _delta_mixer(p, small, conv_a, a_row, dt_row, onorm, None, SEQ, BATCH, 0)
    oa_l, _ = _delta_mixer(p, small, conv_a, a_row, dt_row, onorm, state_delta[:, e], DEC_SEQ, DEC_BATCH, N_CTX)
    ob_c = _ctx_attn(p, _EV_QB // W_B, _EV_KB // (HKV_B * HD), _EV_VB // (HKV_B * HD), _EV_ZB // W_B,
                     H_B, HKV_B, sink)
    ck = cache_kv[:, e, 0].reshape(DEC_BATCH, PAST_LEN, HKV_B * HD)
    cv = cache_kv[:, e, 1].reshape(DEC_BATCH, PAST_LEN, HKV_B * HD)
    ob_l = _win_attn(p, ck, cv, sink, _EV_QB // W_B, _EV_KB // (HKV_B * HD), _EV_VB // (HKV_B * HD),
                     _EV_ZB // W_B)
    o_a = jnp.concatenate([oa_c, oa_l], axis=0)
    o_b = jnp.concatenate([ob_c, ob_l], axis=0)
    wo = w_out.astype(BF16)
    x = _out_proj(o_a, o_b, wo[:W_A], wo[W_A:], x, mod, final_w)
    pc = p[:N_CTX]
    k = pc[:, _EV_KB:_EV_VB].reshape(BATCH, SEQ, HKV_B, HD)
    v = pc[:, _EV_VB:_EV_VB + HKV_B * HD].reshape(BATCH, SEQ, HKV_B, HD)
    return x, st, jnp.stack([k, v], axis=1)


def _odd_layer(x, o_i, mod, norm_w, w_in, w_glr, b_glr, onorm, rpb, w_out, state_gla, cache_kv, final_w):
    lo = 2 * QK_C + 2 * W_C
    w_main, w_small = _regroup(w_in, [(0, lo), (PC_ODD, PC_ODD + 4 * W_D)], (lo, PC_ODD))
    p, small = _in_proj(x, norm_w, mod, w_main, w_small)
    w_gate = jnp.stack([jnp.pad(w_glr[0], ((0, LANE - GLA_RANK), (0, 0))),
                        jnp.pad(w_glr[1], ((GLA_RANK, LANE - 2 * GLA_RANK), (0, 0)))])
    oc_c, st = _gla_mixer(p, small, w_gate, b_glr, onorm, None, SEQ, BATCH, 0)
    oc_l, _ = _gla_mixer(p, small, w_gate, b_glr, onorm, state_gla[:, o_i], DEC_SEQ, DEC_BATCH, N_CTX)
    od_c = _ctx_attn(p, _OD_QD // W_D, _OD_KD // W_D, _OD_VD // W_D, _OD_ZD // W_D, H_D, H_D, None)
    ck = cache_kv[:, o_i, 0].reshape(DEC_BATCH, PAST_LEN, W_D)
    cv = cache_kv[:, o_i, 1].reshape(DEC_BATCH, PAST_LEN, W_D)
    od_l = _nbr_attn(p, ck, cv, _nbr_bias_table(rpb), _OD_QD // W_D, _OD_KD // W_D, _OD_VD // W_D, _OD_ZD // W_D)
    o_c = jnp.concatenate([oc_c, oc_l], axis=0)
    o_d = jnp.concatenate([od_c, od_l], axis=0)
    wo = w_out.astype(BF16)
    x = _out_proj(o_c, o_d, wo[:W_C], wo[W_C:], x, mod, final_w)
    pc = p[:N_CTX]
    k = pc[:, _OD_KD:_OD_VD].reshape(BATCH, SEQ, H_D, HD)
    v = pc[:, _OD_VD:_OD_ZD].reshape(BATCH, SEQ, H_D, HD)
    return x, st, jnp.stack([k, v], axis=1)


def kernel(x_prompt, x_sample, state_delta, cache_kv_win, state_gla, cache_kv_nbr, c, c_ctx, norm_w, w_ada, b_ada, w_in_even, conv_a, a_log_a, dt_bias_a, onorm_a, sink_b, w_out_even, w_in_odd, w_glr_c, b_glr_c, onorm_c, rpb_d, w_out_odd, final_norm_w):
    x = jnp.concatenate([x_prompt.reshape(N_CTX, D_MODEL), x_sample.reshape(N_LAT, D_MODEL)], axis=0)
    cond = jnp.concatenate([c_ctx[None, :], c, jnp.zeros((N_COND - 1 - DEC_BATCH, D_MODEL), F32)], axis=0)
    mods = _ada_mod(cond, w_ada, b_ada).reshape(DEPTH, N_COND, 1, 3 * D_MODEL)
    new_delta, new_kvw, new_gla, new_kvn = [], [], [], []
    for li in range(DEPTH):
        final_w = final_norm_w if li == DEPTH - 1 else None
        if li % 2 == 0:
            e = li // 2
            x, st, kv = _even_layer(x, e, mods[li], norm_w[li], w_in_even[e], conv_a[e], a_log_a[e], dt_bias_a[e],
                                    onorm_a[e], sink_b[e], w_out_even[e], state_delta, cache_kv_win, final_w)
            new_delta.append(st)
            new_kvw.append(kv)
        else:
            o_i = li // 2
            x, st, kv = _odd_layer(x, o_i, mods[li], norm_w[li], w_in_odd[o_i], w_glr_c[o_i], b_glr_c[o_i],
                                   onorm_c[o_i], rpb_d[o_i], w_out_odd[o_i], state_gla, cache_kv_nbr, final_w)
            new_gla.append(st)
            new_kvn.append(kv)
    y_prompt = x[:N_CTX].reshape(BATCH, SEQ, D_MODEL)
    y_sample = x[N_CTX:].reshape(DEC_BATCH, DEC_SEQ, D_MODEL)
    return (y_prompt, y_sample, jnp.stack(new_delta, 1), jnp.stack(new_kvw, 1),
            jnp.stack(new_gla, 1), jnp.stack(new_kvn, 1))
```

```python
import functools

import numpy as np
import jax
import jax.numpy as jnp
from jax import lax
from jax.experimental import pallas as pl
from jax.experimental.pallas import tpu as pltpu

F32 = jnp.float32
BF16 = jnp.bfloat16
HIGHEST = lax.Precision.HIGHEST

D_MODEL = 2048
BATCH = 16
SEQ = 256
DEPTH = 4
DEC_BATCH = 4
DEC_SEQ = 1024
PAST_LEN = 256
GRID_W = 64
HD = 128
EPS = 1e-6
NEG = -1e30
ROPE_THETA = 10000.0
CHUNK = 64
H_A = 8
W_A = H_A * HD
CONV_K = 5
H_B = 8
HKV_B = 2
W_B = H_B * HD
WIN = 128
QBLK = 128
H_C = 4
DK_C = 128
DV_C = 256
QK_C = H_C * DK_C
W_C = H_C * DV_C
GLA_RANK = 16
GLA_TAU = 16.0
H_D = 8
W_D = H_D * HD
NB_H = 8
NB_W = 16
N_EVEN = (DEPTH + 1) // 2
N_ODD = DEPTH // 2
PA_EVEN = 4 * W_A + 4 * H_A
PC_ODD = 2 * QK_C + 2 * W_C + 2 * GLA_RANK

N_CTX = BATCH * SEQ
N_LAT = DEC_BATCH * DEC_SEQ
N_TOK = N_CTX + N_LAT
N_COND = 8
LANE = 128
TM_IN = 512
TN_IN = 512
TM_OUT = 256
TN_ADA = 512
VMEM_LIMIT = 56 * 1024 * 1024


def _cparams(sem):
    return pltpu.CompilerParams(dimension_semantics=sem, vmem_limit_bytes=VMEM_LIMIT)


def _sigmoid(x):
    return 1.0 / (1.0 + jnp.exp(-x))


def _silu(x):
    return x * _sigmoid(x)


def _dot(a, b):
    return jnp.dot(a.astype(BF16), b.astype(BF16), preferred_element_type=F32)


def _dot_nt(a, b):
    return lax.dot_general(a.astype(BF16), b.astype(BF16), (((1,), (1,)), ((), ())),
                           preferred_element_type=F32)


def _dot_tn(a, b):
    return lax.dot_general(a.astype(BF16), b.astype(BF16), (((0,), (0,)), ((), ())),
                           preferred_element_type=F32)


def _dot_hi(a, b):
    return jnp.dot(a, b, precision=HIGHEST, preferred_element_type=F32)


def _ada_kernel(c_ref, w_ref, b_ref, o_ref):
    o_ref[...] = _dot(_silu(c_ref[...]), w_ref[...]) + b_ref[...]


def _ada_mod(cond, w_ada, b_ada):
    n3 = 3 * D_MODEL
    return pl.pallas_call(
        _ada_kernel,
        grid=(DEPTH, n3 // TN_ADA),
        in_specs=[
            pl.BlockSpec((N_COND, D_MODEL), lambda l, j: (0, 0)),
            pl.BlockSpec((None, D_MODEL, TN_ADA), lambda l, j: (l, 0, j)),
            pl.BlockSpec((None, 1, TN_ADA), lambda l, j: (l, 0, j)),
        ],
        out_specs=pl.BlockSpec((None, N_COND, TN_ADA), lambda l, j: (l, 0, j)),
        out_shape=jax.ShapeDtypeStruct((DEPTH, N_COND, n3), F32),
        compiler_params=_cparams(("arbitrary", "arbitrary")),
        name="ada_mod",
    )(cond, w_ada, b_ada.reshape(DEPTH, 1, n3))


def _cond_row(i, tm):
    nct = N_CTX // tm
    per = DEC_SEQ // tm
    return jnp.where(i < nct, 0, 1 + (i - nct) // per)


def _inproj_kernel(x_ref, nw_ref, shift_ref, scale_ref, w_ref, ws_ref, o_ref, os_ref, h_ref):
    @pl.when(pl.program_id(1) == 0)
    def _():
        x = x_ref[...]
        y = x * lax.rsqrt(jnp.mean(x * x, axis=-1, keepdims=True) + EPS) * nw_ref[...]
        h = (y * (1.0 + scale_ref[...]) + shift_ref[...]).astype(BF16)
        h_ref[...] = h
        os_ref[...] = jnp.dot(h, ws_ref[...], preferred_element_type=F32)

    o_ref[...] = jnp.dot(h_ref[...], w_ref[...], preferred_element_type=F32)


def _in_proj(x, norm_w, mod, w_main, w_small):
    n = w_main.shape[1]
    row = functools.partial(_cond_row, tm=TM_IN)
    return pl.pallas_call(
        _inproj_kernel,
        grid=(N_TOK // TM_IN, n // TN_IN),
        in_specs=[
            pl.BlockSpec((TM_IN, D_MODEL), lambda i, j: (i, 0)),
            pl.BlockSpec((1, D_MODEL), lambda i, j: (0, 0)),
            pl.BlockSpec((None, 1, D_MODEL), lambda i, j: (row(i), 0, 0)),
            pl.BlockSpec((None, 1, D_MODEL), lambda i, j: (row(i), 0, 1)),
            pl.BlockSpec((D_MODEL, TN_IN), lambda i, j: (0, j)),
            pl.BlockSpec((D_MODEL, LANE), lambda i, j: (0, 0)),
        ],
        out_specs=[
            pl.BlockSpec((TM_IN, TN_IN), lambda i, j: (i, j)),
            pl.BlockSpec((TM_IN, LANE), lambda i, j: (i, 0)),
        ],
        out_shape=[jax.ShapeDtypeStruct((N_TOK, n), F32),
                   jax.ShapeDtypeStruct((N_TOK, LANE), F32)],
        scratch_shapes=[pltpu.VMEM((TM_IN, D_MODEL), BF16)],
        compiler_params=_cparams(("arbitrary", "arbitrary")),
        name="in_proj",
    )(x, norm_w.reshape(1, D_MODEL), mod, mod, w_main, w_small)


def _outproj_kernel(oa_ref, ob_ref, wa_ref, wb_ref, x_ref, g_ref, *rest, final):
    acc = _dot(oa_ref[...], wa_ref[...]) + _dot(ob_ref[...], wb_ref[...])
    xn = x_ref[...] + g_ref[...] * acc
    if final:
        fw_ref, y_ref = rest
        y_ref[...] = xn * lax.rsqrt(jnp.mean(xn * xn, axis=-1, keepdims=True) + EPS) * fw_ref[...]
    else:
        (y_ref,) = rest
        y_ref[...] = xn


def _out_proj(o_a, o_b, w_a, w_b, x, mod, final_w=None):
    ka, kb = o_a.shape[1], o_b.shape[1]
    row = functools.partial(_cond_row, tm=TM_OUT)
    final = final_w is not None
    in_specs = [
        pl.BlockSpec((TM_OUT, ka), lambda i: (i, 0)),
        pl.BlockSpec((TM_OUT, kb), lambda i: (i, 0)),
        pl.BlockSpec((ka, D_MODEL), lambda i: (0, 0)),
        pl.BlockSpec((kb, D_MODEL), lambda i: (0, 0)),
        pl.BlockSpec((TM_OUT, D_MODEL), lambda i: (i, 0)),
        pl.BlockSpec((None, 1, D_MODEL), lambda i: (row(i), 0, 2)),
    ]
    args = [o_a, o_b, w_a, w_b, x, mod]
    if final:
        in_specs.append(pl.BlockSpec((1, D_MODEL), lambda i: (0, 0)))
        args.append(final_w.reshape(1, D_MODEL))
    return pl.pallas_call(
        functools.partial(_outproj_kernel, final=final),
        grid=(N_TOK // TM_OUT,),
        in_specs=in_specs,
        out_specs=pl.BlockSpec((TM_OUT, D_MODEL), lambda i: (i, 0)),
        out_shape=jax.ShapeDtypeStruct((N_TOK, D_MODEL), F32),
        compiler_params=_cparams(("arbitrary",)),
        name="out_proj_final" if final else "out_proj",
    )(*args)


def _attend(q, ks, vs, masks, biases, sink, scale):
    ss = []
    for k, m, bias in zip(ks, masks, biases):
        s = _dot_nt(q, k) * scale
        if bias is not None:
            s = s + bias
        if m is not None:
            s = jnp.where(m, s, NEG)
        ss.append(s)
    mx = functools.reduce(jnp.maximum, [jnp.max(s, axis=-1, keepdims=True) for s in ss])
    if sink is not None:
        mx = jnp.maximum(mx, sink)
    es = [jnp.exp(s - mx) for s in ss]
    den = functools.reduce(jnp.add, [jnp.sum(e, axis=-1, keepdims=True) for e in es])
    if sink is not None:
        den = den + jnp.exp(sink - mx)
    o = functools.reduce(jnp.add, [_dot(e, v) for e, v in zip(es, vs)])
    return o / den


def _head(ref, h, rows=None):
    if rows is None:
        return ref[:, h * HD:(h + 1) * HD]
    return ref[rows, h * HD:(h + 1) * HD]


def _ctx_attn_kernel(*refs, heads, kv_heads, use_sink):
    if use_sink:
        q_ref, k_ref, v_ref, z_ref, sink_ref, o_ref = refs
    else:
        q_ref, k_ref, v_ref, z_ref, o_ref = refs
    g = heads // kv_heads
    n = q_ref.shape[0]
    scale = HD ** -0.5
    for j in range(kv_heads):
        q = jnp.concatenate([_head(q_ref, j * g + t) for t in range(g)], axis=0)
        sink = None
        if use_sink:
            sink = jnp.concatenate([jnp.full((n, 1), sink_ref[j * g + t], F32) for t in range(g)], axis=0)
        o = _attend(q, [_head(k_ref, j)], [_head(v_ref, j)], [None], [None], sink, scale)
        for t in range(g):
            h = j * g + t
            o_ref[:, h * HD:(h + 1) * HD] = o[t * n:(t + 1) * n] * _silu(_head(z_ref, h))


def _ctx_attn(p, q_col, k_col, v_col, z_col, heads, kv_heads, sink):
    wq, wkv = heads * HD, kv_heads * HD
    use_sink = sink is not None
    in_specs = [
        pl.BlockSpec((SEQ, wq), lambda b: (b, q_col)),
        pl.BlockSpec((SEQ, wkv), lambda b: (b, k_col)),
        pl.BlockSpec((SEQ, wkv), lambda b: (b, v_col)),
        pl.BlockSpec((SEQ, wq), lambda b: (b, z_col)),
    ]
    args = [p, p, p, p]
    if use_sink:
        in_specs.append(pl.BlockSpec(memory_space=pltpu.SMEM))
        args.append(sink)
    return pl.pallas_call(
        functools.partial(_ctx_attn_kernel, heads=heads, kv_heads=kv_heads, use_sink=use_sink),
        grid=(BATCH,),
        in_specs=in_specs,
        out_specs=pl.BlockSpec((SEQ, wq), lambda b: (b, 0)),
        out_shape=jax.ShapeDtypeStruct((N_CTX, wq), F32),
        compiler_params=_cparams(("arbitrary",)),
        name="ctx_attn_sink" if use_sink else "ctx_attn",
    )(*args)


def _rope_tables():
    half = HD // 4
    freq = (ROPE_THETA ** (-np.arange(half, dtype=np.float32) / half)).astype(np.float32)
    t = np.arange(DEC_SEQ)
    ang_r = (t // GRID_W).astype(np.float32)[:, None] * freq[None, :]
    ang_c = (t % GRID_W).astype(np.float32)[:, None] * freq[None, :]
    cos = np.concatenate([np.cos(ang_r)] * 2 + [np.cos(ang_c)] * 2, axis=1).astype(np.float32)
    sin_r, sin_c, zero = np.sin(ang_r), np.sin(ang_c), np.zeros_like(ang_r)
    s_up = np.concatenate([-sin_r, zero, -sin_c, zero], axis=1).astype(np.float32)
    s_dn = np.concatenate([zero, sin_r, zero, sin_c], axis=1).astype(np.float32)
    return jnp.asarray(cos), jnp.asarray(s_up), jnp.asarray(s_dn)


def _rope(x, cos, s_up, s_dn):
    return x * cos + pltpu.roll(x, HD - HD // 4, 1) * s_up + pltpu.roll(x, HD // 4, 1) * s_dn


def _win_attn_kernel(q_ref, k_ref, v_ref, kc_ref, vc_ref, z_ref, cos_ref, sup_ref, sdn_ref, sink_ref, o_ref):
    i = pl.program_id(1)
    g = H_B // HKV_B
    span = QBLK + 2 * WIN
    scale = HD ** -0.5
    start = pl.multiple_of(jnp.clip(i * QBLK - WIN, 0, DEC_SEQ - span), QBLK)
    qrows = pl.ds(pl.multiple_of(i * QBLK, QBLK), QBLK)
    krows = pl.ds(start, span)
    cq, uq, dq = cos_ref[qrows, :], sup_ref[qrows, :], sdn_ref[qrows, :]
    ck, uk, dk = cos_ref[krows, :], sup_ref[krows, :], sdn_ref[krows, :]
    qpos = i * QBLK + lax.broadcasted_iota(jnp.int32, (g * QBLK, span), 0) % QBLK
    kpos = start + lax.broadcasted_iota(jnp.int32, (g * QBLK, span), 1)
    band = jnp.abs(qpos - kpos) <= WIN
    for j in range(HKV_B):
        q = jnp.concatenate([_rope(_head(q_ref, j * g + t), cq, uq, dq) for t in range(g)], axis=0)
        kw = _rope(_head(k_ref, j, krows), ck, uk, dk)
        vw = _head(v_ref, j, krows)
        sink = jnp.concatenate([jnp.full((QBLK, 1), sink_ref[j * g + t], F32) for t in range(g)], axis=0)
        o = _attend(q, [kw, _head(kc_ref, j)], [vw, _head(vc_ref, j)], [band, None], [None, None], sink, scale)
        for t in range(g):
            h = j * g + t
            o_ref[:, h * HD:(h + 1) * HD] = o[t * QBLK:(t + 1) * QBLK] * _silu(_head(z_ref, h))


def _win_attn(p, cache_k, cache_v, sink, q_col, k_col, v_col, z_col):
    wq, wkv = H_B * HD, HKV_B * HD
    nq = DEC_SEQ // QBLK
    lat = N_CTX // DEC_SEQ
    cos, s_up, s_dn = _rope_tables()
    full = pl.BlockSpec((DEC_SEQ, HD), lambda b, i: (0, 0))
    return pl.pallas_call(
        _win_attn_kernel,
        grid=(DEC_BATCH, nq),
        in_specs=[
            pl.BlockSpec((QBLK, wq), lambda b, i: (N_CTX // QBLK + b * nq + i, q_col)),
            pl.BlockSpec((DEC_SEQ, wkv), lambda b, i: (lat + b, k_col)),
            pl.BlockSpec((DEC_SEQ, wkv), lambda b, i: (lat + b, v_col)),
            pl.BlockSpec((None, PAST_LEN, wkv), lambda b, i: (b, 0, 0)),
            pl.BlockSpec((None, PAST_LEN, wkv), lambda b, i: (b, 0, 0)),
            pl.BlockSpec((QBLK, wq), lambda b, i: (N_CTX // QBLK + b * nq + i, z_col)),
            full, full, full,
            pl.BlockSpec(memory_space=pltpu.SMEM),
        ],
        out_specs=pl.BlockSpec((QBLK, wq), lambda b, i: (b * nq + i, 0)),
        out_shape=jax.ShapeDtypeStruct((N_LAT, wq), F32),
        compiler_params=_cparams(("arbitrary", "arbitrary")),
        name="win_attn",
    )(p, p, p, cache_k, cache_v, p, cos, s_up, s_dn, sink)


def _nbr_onehot():
    qc = np.arange(GRID_W)[:, None]
    kc = np.arange(GRID_W)[None, :]
    idx = np.clip(kc - qc, -(NB_W - 1), NB_W - 1) + NB_W - 1
    e = (np.arange(2 * NB_W)[:, None, None] == idx[None]).astype(np.float32)
    return jnp.asarray(e.reshape(2 * NB_W, GRID_W * GRID_W))


def _bias_expand_kernel(r_ref, e_ref, o_ref):
    o_ref[...] = _dot_hi(r_ref[...], e_ref[...])


def _nbr_bias_table(rpb):
    rows = H_D * (2 * NB_H - 1)
    r = jnp.pad(rpb.reshape(rows, 2 * NB_W - 1), ((0, 128 - rows), (0, 1)))
    t = pl.pallas_call(
        _bias_expand_kernel,
        out_shape=jax.ShapeDtypeStruct((128, GRID_W * GRID_W), F32),
        name="nbr_bias_expand",
    )(r, _nbr_onehot())
    t = t[:rows].reshape(H_D, 2 * NB_H - 1, GRID_W, GRID_W)
    return jnp.concatenate([t[:, :-1], t[:, 1:]], axis=-1)


def _nbr_attn_kernel(q_ref, k_ref, v_ref, kc_ref, vc_ref, z_ref, t_ref, o_ref):
    r = pl.program_id(1)
    rows = DEC_SEQ // GRID_W
    rs = jnp.clip(r - NB_H // 2, 0, rows - NB_H)
    dr0 = rs - r + NB_H - 1
    nk = NB_H * GRID_W
    krows = pl.ds(pl.multiple_of(rs * GRID_W, GRID_W), nk)
    scale = HD ** -0.5
    qc = lax.broadcasted_iota(jnp.int32, (GRID_W, nk), 0)
    kc = lax.broadcasted_iota(jnp.int32, (GRID_W, nk), 1) % GRID_W
    cstart = jnp.clip(qc - NB_W // 2, 0, GRID_W - NB_W)
    ok = (kc >= cstart) & (kc < cstart + NB_W)
    for h in range(H_D):
        bias = jnp.concatenate([t_ref[h, dr0 + 2 * m] for m in range(nk // LANE)], axis=1)
        o = _attend(_head(q_ref, h), [_head(k_ref, h, krows), _head(kc_ref, h)],
                    [_head(v_ref, h, krows), _head(vc_ref, h)], [ok, None], [bias, None], None, scale)
        o_ref[:, h * HD:(h + 1) * HD] = o * _silu(_head(z_ref, h))


def _nbr_attn(p, cache_k, cache_v, table, q_col, k_col, v_col, z_col):
    rows = DEC_SEQ // GRID_W
    lat = N_CTX // DEC_SEQ
    return pl.pallas_call(
        _nbr_attn_kernel,
        grid=(DEC_BATCH, rows),
        in_specs=[
            pl.BlockSpec((GRID_W, W_D), lambda b, r: (N_CTX // GRID_W + b * rows + r, q_col)),
            pl.BlockSpec((DEC_SEQ, W_D), lambda b, r: (lat + b, k_col)),
            pl.BlockSpec((DEC_SEQ, W_D), lambda b, r: (lat + b, v_col)),
            pl.BlockSpec((None, PAST_LEN, W_D), lambda b, r: (b, 0, 0)),
            pl.BlockSpec((None, PAST_LEN, W_D), lambda b, r: (b, 0, 0)),
            pl.BlockSpec((GRID_W, W_D), lambda b, r: (N_CTX // GRID_W + b * rows + r, z_col)),
            pl.BlockSpec(table.shape, lambda b, r: (0, 0, 0, 0)),
        ],
        out_specs=pl.BlockSpec((GRID_W, W_D), lambda b, r: (b * rows + r, 0)),
        out_shape=jax.ShapeDtypeStruct((N_LAT, W_D), F32),
        compiler_params=_cparams(("arbitrary", "arbitrary")),
        name="nbr_attn",
    )(p, p, p, cache_k, cache_v, p, table)


def _chunk_masks(reverse):
    c = lax.broadcasted_iota(jnp.int32, (CHUNK, CHUNK), 0)
    j = lax.broadcasted_iota(jnp.int32, (CHUNK, CHUNK), 1)
    if reverse:
        return j >= c, j <= c, j > c, j == c
    return j <= c, j >= c, j < c, j == c


def _lane_col(x, idx):
    lane = lax.broadcasted_iota(jnp.int32, x.shape, 1)
    return jnp.sum(jnp.where(lane == idx, x, 0.0), axis=-1, keepdims=True)


def _short_conv(x_ref, w_ref, pad_ref, n):
    pad = CONV_K // 2
    zeros = jnp.zeros((8, HD), F32)
    pad_ref[0:8, :] = zeros
    pad_ref[n + 8:n + 16, :] = zeros
    pad_ref[8:n + 8, :] = x_ref[...]
    y = functools.reduce(jnp.add, [pad_ref[8 - pad + t:8 - pad + t + n, :] * w_ref[t:t + 1, :]
                                   for t in range(CONV_K)])
    return _silu(y)


def _l2norm(x):
    return x * lax.rsqrt(jnp.sum(x * x, axis=-1, keepdims=True) + EPS)


def _split2(x):
    hi = x.astype(BF16)
    return hi, (x - hi.astype(F32)).astype(BF16)


def _dots_x2(pairs):
    ops = []
    for a, b in pairs:
        a_hi, a_lo = _split2(a)
        b_hi, b_lo = _split2(b)
        ops.append((jnp.concatenate([a_hi, a_lo], axis=0), jnp.concatenate([b_hi, b_lo], axis=1)))
    rs = [jnp.dot(a2, b2, preferred_element_type=F32) for a2, b2 in ops]
    out = []
    for (a, b), r in zip(pairs, rs):
        m, n = a.shape[0], b.shape[1]
        out.append((r[m:, :n] + r[:m, n:] + r[m:, n:]) + r[:m, :n])
    return out


def _pair_masks():
    c = lax.broadcasted_iota(jnp.int32, (CHUNK, HD), 0)
    l = lax.broadcasted_iota(jnp.int32, (CHUNK, HD), 1)
    left = l < CHUNK
    j = l % CHUNK
    ahead = jnp.where(left, j - c, c - j)
    return left, ahead <= 0, ahead < 0, j == c


def _block_diag(x, left):
    return jnp.concatenate([jnp.where(left, x, 0.0), jnp.where(left, 0.0, x)], axis=0)


def _inv_unit_triangular_pairs(lmats, left, eye):
    mps = [-x for x in lmats]
    ps = [jnp.where(eye, 1.0, 0.0) + m for m in mps]
    mps = _dots_x2([(m, _block_diag(m, left)) for m in mps])
    for _ in range(4):
        rs = _dots_x2([(jnp.concatenate([p, m], axis=0), _block_diag(m, left)) for p, m in zip(ps, mps)])
        ps = [p + r[:CHUNK] for p, r in zip(ps, rs)]
        mps = [r[CHUNK:] for r in rs]
    rs = _dots_x2([(p, _block_diag(m, left)) for p, m in zip(ps, mps)])
    return [p + r for p, r in zip(ps, rs)]


def _chunk_cumsum(x, reverse):
    n = x.shape[0]
    pos = lax.broadcasted_iota(jnp.int32, x.shape, 0) % CHUNK
    k = 1
    while k < CHUNK:
        if reverse:
            x = x + jnp.where(pos < CHUNK - k, pltpu.roll(x, n - k, 0), 0.0)
        else:
            x = x + jnp.where(pos >= k, pltpu.roll(x, k, 0), 0.0)
        k *= 2
    return x


def _delta_prep(items):
    left, tri, strict, eye = _pair_masks()
    zeros = jnp.zeros((CHUNK, HD), F32)
    pre = []
    for q, k, v, cols in items:
        b_f, b_b, g_f, g_b = (cols[:, t:t + 1] for t in range(4))
        gsel = jnp.where(left, g_f, g_b)
        g_row = jnp.sum(jnp.where(eye, gsel, 0.0), axis=0, keepdims=True)
        decay = jnp.exp(jnp.where(tri, gsel - g_row, NEG))
        kb_f, kb_b = k * b_f, k * b_b
        lhs = jnp.concatenate([jnp.concatenate([kb_f, kb_b], axis=1), jnp.concatenate([q, q], axis=1)], axis=0)
        rhs = jnp.concatenate([jnp.concatenate([k, zeros], axis=1), jnp.concatenate([zeros, k], axis=1)], axis=0)
        pre.append((decay, kb_f, kb_b, g_f, g_b, b_f, b_b, lhs, rhs))
    kqs = [_dot_nt(x[7], x[8]) for x in pre]
    lmats = [jnp.where(strict, kq[:CHUNK] * x[0], 0.0) for kq, x in zip(kqs, pre)]
    attns = [jnp.where(tri, kq[CHUNK:] * x[0], 0.0) for kq, x in zip(kqs, pre)]
    tinvs = _inv_unit_triangular_pairs(lmats, left, eye)
    egs, rhs = [], []
    for (q, k, v, cols), x in zip(items, pre):
        _, kb_f, kb_b, g_f, g_b, b_f, b_b = x[:7]
        eg_f = jnp.exp(jnp.broadcast_to(g_f, (CHUNK, HD)))
        eg_b = jnp.exp(jnp.broadcast_to(g_b, (CHUNK, HD)))
        egs.append((eg_f, eg_b))
        rhs.append(jnp.concatenate([jnp.concatenate([v * b_f, kb_f * eg_f, zeros, zeros], axis=1),
                                    jnp.concatenate([zeros, zeros, v * b_b, kb_b * eg_b], axis=1)], axis=0))
    uws = _dots_x2(list(zip(tinvs, rhs)))
    out = []
    for (q, k, v, cols), x, attn, uw, (eg_f, eg_b) in zip(items, pre, attns, uws, egs):
        g_f, g_b = x[3], x[4]
        gl_f = jnp.broadcast_to(g_f[CHUNK - 1:CHUNK, :], (1, HD))
        gl_b = jnp.broadcast_to(g_b[0:1, :], (1, HD))
        kg_t = jnp.concatenate([k * jnp.exp(gl_f - g_f), k * jnp.exp(gl_b - g_b)], axis=1).T
        out.append(dict(u=[uw[:, :HD], uw[:, 2 * HD:3 * HD]], w=[uw[:, HD:2 * HD], uw[:, 3 * HD:]],
                        attn=[attn[:, :CHUNK], attn[:, CHUNK:]], qg=[q * eg_f, q * eg_b],
                        kg_t=[kg_t[:HD], kg_t[HD:]], eg=[jnp.exp(gl_f), jnp.exp(gl_b)]))
    return out


def _delta_kernel(*refs, n, hb, cg, has_s0):
    if has_s0:
        (qp_ref, kp_ref, vp_ref, z_ref, sm_ref, cq_ref, ck_ref, cv_ref, al_ref, dt_ref, on_ref, s0_ref,
         o_ref, so_ref, q_s, k_s, v_s, pad_s, col_s, u_s, w_s, at_s, qg_s, kg_s, eg_s, acc_s, st_s) = refs
    else:
        (qp_ref, kp_ref, vp_ref, z_ref, sm_ref, cq_ref, ck_ref, cv_ref, al_ref, dt_ref, on_ref,
         o_ref, so_ref, q_s, k_s, v_s, pad_s, col_s, u_s, w_s, at_s, qg_s, kg_s, eg_s, acc_s, st_s) = refs
    h0 = pl.program_id(1) * hb
    nc = n // CHUNK
    sm = sm_ref[...]
    x = sm + dt_ref[...]
    softplus = jnp.maximum(x, 0.0) + jnp.log(1.0 + jnp.exp(-jnp.abs(x)))
    gates = -jnp.exp(al_ref[...]) * softplus
    betas = _sigmoid(sm)
    cum_f = _chunk_cumsum(gates, False)
    cum_b = _chunk_cumsum(gates, True)
    lane = lax.broadcasted_iota(jnp.int32, (n, HD), 1)
    for j in range(hb):
        cols = slice(j * HD, (j + 1) * HD)
        q_s[j] = _l2norm(_short_conv(qp_ref.at[:, cols], cq_ref.at[:, cols], pad_s, n)) * HD ** -0.5
        k_s[j] = _l2norm(_short_conv(kp_ref.at[:, cols], ck_ref.at[:, cols], pad_s, n))
        v_s[j] = _short_conv(vp_ref.at[:, cols], cv_ref.at[:, cols], pad_s, n)
        col_s[j] = jnp.where(lane == 0, _lane_col(betas, h0 + j),
                             jnp.where(lane == 1, _lane_col(betas, H_A + h0 + j),
                                       jnp.where(lane == 2, _lane_col(cum_f, 2 * H_A + h0 + j),
                                                 _lane_col(cum_b, 3 * H_A + h0 + j))))
    acc_s[...] = jnp.zeros((hb, n, HD), F32)
    if has_s0:
        st_s[...] = s0_ref[...]
    else:
        st_s[...] = jnp.zeros((2, hb, HD, HD), F32)

    def prep(ci, carry):
        where = [(j, ci * cg + t) for j in range(hb) for t in range(cg)]
        rows = [pl.ds(pl.multiple_of(c * CHUNK, CHUNK), CHUNK) for _, c in where]
        outs = _delta_prep([(q_s[j, r, :], k_s[j, r, :], v_s[j, r, :], col_s[j, r, :])
                            for (j, _), r in zip(where, rows)])
        for (j, c), r, o in zip(where, rows, outs):
            for d in range(2):
                u_s[d, j, r, :] = o["u"][d]
                w_s[d, j, r, :] = o["w"][d].astype(BF16)
                at_s[d, j, r, :] = o["attn"][d].astype(BF16)
                qg_s[d, j, r, :] = o["qg"][d].astype(BF16)
                kg_s[d, j, c] = o["kg_t"][d].astype(BF16)
                eg_s[d, j, c] = jnp.broadcast_to(o["eg"][d], (8, HD))
        return carry

    lax.fori_loop(0, nc // cg, prep, 0)

    def scan(i, carry):
        chains = [(d, j, (nc - 1 - i) if d else i) for d in range(2) for j in range(hb)]
        rows = [pl.ds(pl.multiple_of(c * CHUNK, CHUNK), CHUNK) for _, _, c in chains]
        ss = [st_s[d, j] for d, j, _ in chains]
        sbs = [s.astype(BF16) for s in ss]
        ws = [jnp.dot(w_s[d, j, r, :], sb, preferred_element_type=F32) for (d, j, _), r, sb in zip(chains, rows, sbs)]
        vns = [(u_s[d, j, r, :] - w).astype(BF16) for (d, j, _), r, w in zip(chains, rows, ws)]
        for (d, j, c), r, s, sb, vn in zip(chains, rows, ss, sbs, vns):
            st_s[d, j] = s * eg_s[d, j, c, 0:1, :] + jnp.dot(kg_s[d, j, c], vn, preferred_element_type=F32)
        for (d, j, c), r, sb, vn in zip(chains, rows, sbs, vns):
            acc_s[j, r, :] += (jnp.dot(qg_s[d, j, r, :], sb, preferred_element_type=F32)
                               + jnp.dot(at_s[d, j, r, :], vn, preferred_element_type=F32))
        return carry

    lax.fori_loop(0, nc, scan, 0)
    for j in range(hb):
        cols = slice(j * HD, (j + 1) * HD)
        o = acc_s[j]
        o = o * lax.rsqrt(jnp.mean(o * o, axis=-1, keepdims=True) + EPS) * on_ref[...]
        o_ref[:, cols] = o * _silu(z_ref[:, cols])
    so_ref[...] = st_s[...]


DELTA_HEADS = 2
DELTA_CHUNKS = 2


def _delta_mixer(p, small, conv_w, a_row, dt_row, onorm, s0, seq, nb, row0):
    r0 = row0 // seq
    has_s0 = s0 is not None
    hb = DELTA_HEADS
    wb = hb * HD
    ng = H_A // hb
    nc = seq // CHUNK
    col = lambda off: (lambda b, h: (r0 + b, off + h))
    in_specs = [
        pl.BlockSpec((seq, wb), col(0)),
        pl.BlockSpec((seq, wb), col(ng)),
        pl.BlockSpec((seq, wb), col(2 * ng)),
        pl.BlockSpec((seq, wb), col(3 * ng)),
        pl.BlockSpec((seq, LANE), lambda b, h: (r0 + b, 0)),
        pl.BlockSpec((CONV_K, wb), lambda b, h: (0, h)),
        pl.BlockSpec((CONV_K, wb), lambda b, h: (0, ng + h)),
        pl.BlockSpec((CONV_K, wb), lambda b, h: (0, 2 * ng + h)),
        pl.BlockSpec((1, LANE), lambda b, h: (0, 0)),
        pl.BlockSpec((1, LANE), lambda b, h: (0, 0)),
        pl.BlockSpec((1, HD), lambda b, h: (0, 0)),
    ]
    args = [p, p, p, p, small, conv_w, conv_w, conv_w, a_row, dt_row, onorm.reshape(1, HD)]
    if has_s0:
        in_specs.append(pl.BlockSpec((None, 2, hb, HD, HD), lambda b, h: (b, 0, h, 0, 0)))
        args.append(s0)
    return pl.pallas_call(
        functools.partial(_delta_kernel, n=seq, hb=hb, cg=DELTA_CHUNKS, has_s0=has_s0),
        grid=(nb, ng),
        in_specs=in_specs,
        out_specs=[
            pl.BlockSpec((seq, wb), lambda b, h: (b, h)),
            pl.BlockSpec((None, 2, hb, HD, HD), lambda b, h: (b, 0, h, 0, 0)),
        ],
        out_shape=[jax.ShapeDtypeStruct((nb * seq, W_A), F32),
                   jax.ShapeDtypeStruct((nb, 2, H_A, HD, HD), F32)],
        scratch_shapes=[
            pltpu.VMEM((hb, seq, HD), F32), pltpu.VMEM((hb, seq, HD), F32), pltpu.VMEM((hb, seq, HD), F32),
            pltpu.VMEM((seq + 16, HD), F32), pltpu.VMEM((hb, seq, HD), F32),
            pltpu.VMEM((2, hb, seq, HD), F32), pltpu.VMEM((2, hb, seq, HD), BF16),
            pltpu.VMEM((2, hb, seq, CHUNK), BF16), pltpu.VMEM((2, hb, seq, HD), BF16),
            pltpu.VMEM((2, hb, nc, HD, CHUNK), BF16), pltpu.VMEM((2, hb, nc, 8, HD), F32),
            pltpu.VMEM((hb, seq, HD), F32), pltpu.VMEM((2, hb, HD, HD), F32)],
        compiler_params=_cparams(("arbitrary", "arbitrary")),
        name="delta_lat" if has_s0 else "delta_ctx",
    )(*args)


_GLA_LEVELS = (32, 16, 8, 4, 2, 1)


def _gla_consts(reverse):
    r = np.arange(CHUNK)
    flip = (lambda a: a[::-1, ::-1]) if reverse else (lambda a: a)
    sel, hi, pair = [], [], []
    for m in _GLA_LEVELS:
        mid = (r // (2 * m)) * (2 * m) + m
        is_hi = r >= mid
        sel.append(flip(r[None, :] == mid[:, None] - 1))
        hi.append(flip(is_hi[:, None]))
        pair.append(flip((r[:, None] // (2 * m) == r[None, :] // (2 * m)) & is_hi[:, None] & ~is_hi[None, :]))
    out = [np.concatenate(sel), np.concatenate(hi), np.stack(pair)]
    out = [jnp.asarray(np.ascontiguousarray(a).astype(np.float32)) for a in out]
    return [out[0].astype(BF16), out[1], out[2]]


def _split3(x):
    hi = x.astype(BF16)
    r1 = x - hi.astype(F32)
    mid = r1.astype(BF16)
    return hi, mid, (r1 - mid.astype(F32)).astype(BF16)


def _gla_chunk(q, k, v, b, s, consts, reverse):
    sel_ref, hi_ref, pair_ref = consts
    nl = len(_GLA_LEVELS)
    r3 = jnp.dot(sel_ref[...], jnp.concatenate(_split3(b), axis=1), preferred_element_type=F32)
    ref = (r3[:, 2 * DK_C:] + r3[:, DK_C:2 * DK_C]) + r3[:, :DK_C]
    a = jnp.where(_chunk_masks(reverse)[3], jnp.sum(q * k, axis=-1, keepdims=True), 0.0)
    for lv in range(nl):
        rows = slice(lv * CHUNK, (lv + 1) * CHUNK)
        hi = hi_ref[rows, :] > 0.5
        t = b - ref[rows]
        e = jnp.exp(jnp.where(hi, t, -t))
        a = a + _dot_nt(jnp.where(hi, q * e, 0.0), jnp.where(hi, 0.0, k * e)) * pair_ref[lv]
    o = _dot(q * jnp.exp(b), s) + _dot(a, v)
    last = 0 if reverse else CHUNK - 1
    bl = b[last:last + 1, :]
    r = lax.broadcasted_iota(jnp.int32, (DK_C, DK_C), 0)
    c = lax.broadcasted_iota(jnp.int32, (DK_C, DK_C), 1)
    dec = jnp.sum(jnp.where(r == c, jnp.broadcast_to(jnp.exp(bl), (DK_C, DK_C)), 0.0), axis=-1, keepdims=True)
    s_new = s * dec + _dot_tn(k * jnp.exp(bl - b), v)
    return o, s_new


def _gla_kernel(*refs, n, has_s0):
    consts_f, consts_b = refs[8:11], refs[11:14]
    q_ref, k_ref, v_ref, z_ref, sm_ref, wg_ref, bg_ref, on_ref = refs[:8]
    if has_s0:
        s0_ref, o_ref, so_ref, gk_s, acc_s, st_s = refs[14:]
    else:
        o_ref, so_ref, gk_s, acc_s, st_s = refs[14:]
    nc = n // CHUNK
    sm = sm_ref[...]
    for d in range(2):
        x = _dot_hi(sm, wg_ref[d]) + bg_ref[d]
        gk = (jnp.minimum(x, 0.0) - jnp.log(1.0 + jnp.exp(-jnp.abs(x)))) / GLA_TAU
        gk_s[d] = _chunk_cumsum(gk, bool(d))
    acc_s[...] = jnp.zeros((n, DV_C), F32)
    if has_s0:
        st_s[...] = s0_ref[...]
    else:
        st_s[...] = jnp.zeros((2, DK_C, DV_C), F32)

    def body(i, carry):
        for d in range(2):
            c = (nc - 1 - i) if d else i
            rows = pl.ds(pl.multiple_of(c * CHUNK, CHUNK), CHUNK)
            o, s_new = _gla_chunk(q_ref[rows, :] * DK_C ** -0.5, k_ref[rows, :], v_ref[rows, :], gk_s[d, rows, :],
                                  st_s[d], consts_b if d else consts_f, bool(d))
            st_s[d] = s_new
            acc_s[rows, :] += o
        return carry

    lax.fori_loop(0, nc, body, 0, unroll=2)
    o = acc_s[...]
    o = o * lax.rsqrt(jnp.mean(o * o, axis=-1, keepdims=True) + EPS) * on_ref[...]
    o_ref[...] = o * _silu(z_ref[...])
    so_ref[...] = st_s[...]


def _gla_mixer(p, small, w_gate, b_gate, onorm, s0, seq, nb, row0):
    r0 = row0 // seq
    has_s0 = s0 is not None
    consts = _gla_consts(False) + _gla_consts(True)
    const_specs = [pl.BlockSpec(a.shape, (lambda b, h, nd=a.ndim: (0,) * nd)) for a in consts]
    in_specs = [
        pl.BlockSpec((seq, DK_C), lambda b, h: (r0 + b, h)),
        pl.BlockSpec((seq, DK_C), lambda b, h: (r0 + b, QK_C // DK_C + h)),
        pl.BlockSpec((seq, DV_C), lambda b, h: (r0 + b, 2 * QK_C // DV_C + h)),
        pl.BlockSpec((seq, DV_C), lambda b, h: (r0 + b, (2 * QK_C + W_C) // DV_C + h)),
        pl.BlockSpec((seq, LANE), lambda b, h: (r0 + b, 0)),
        pl.BlockSpec((2, LANE, DK_C), lambda b, h: (0, 0, h)),
        pl.BlockSpec((2, 1, DK_C), lambda b, h: (0, 0, h)),
        pl.BlockSpec((1, DV_C), lambda b, h: (0, 0)),
    ] + const_specs
    args = [p, p, p, p, small, w_gate, b_gate.reshape(2, 1, QK_C), onorm.reshape(1, DV_C)] + consts
    if has_s0:
        in_specs.append(pl.BlockSpec((None, 2, None, DK_C, DV_C), lambda b, h: (b, 0, h, 0, 0)))
        args.append(s0)
    return pl.pallas_call(
        functools.partial(_gla_kernel, n=seq, has_s0=has_s0),
        grid=(nb, H_C),
        in_specs=in_specs,
        out_specs=[
            pl.BlockSpec((seq, DV_C), lambda b, h: (b, h)),
            pl.BlockSpec((None, 2, None, DK_C, DV_C), lambda b, h: (b, 0, h, 0, 0)),
        ],
        out_shape=[jax.ShapeDtypeStruct((nb * seq, W_C), F32),
                   jax.ShapeDtypeStruct((nb, 2, H_C, DK_C, DV_C), F32)],
        scratch_shapes=[pltpu.VMEM((2, seq, DK_C), F32), pltpu.VMEM((seq, DV_C), F32),
                        pltpu.VMEM((2, DK_C, DV_C), F32)],
        compiler_params=_cparams(("arbitrary", "arbitrary")),
        name="gla_lat" if has_s0 else "gla_ctx",
    )(*args)


_EV_QB = 4 * W_A
_EV_ZB = _EV_QB + W_B
_EV_KB = _EV_ZB + W_B
_EV_VB = _EV_KB + HKV_B * HD
_OD_QD = 2 * QK_C + 2 * W_C
_OD_KD = _OD_QD + W_D
_OD_VD = _OD_KD + W_D
_OD_ZD = _OD_VD + W_D


def _regroup(w, pieces, small):
    main = jnp.concatenate([w[:, lo:hi] for lo, hi in pieces], axis=1).astype(BF16)
    lo, hi = small
    return main, jnp.pad(w[:, lo:hi], ((0, 0), (0, LANE - (hi - lo)))).astype(BF16)


def _lane_row(v, offset):
    return jnp.pad(v.reshape(1, -1), ((0, 0), (offset, LANE - offset - v.size)))


def _even_layer(x, e, mod, norm_w, w_in, conv_a, a_log, dt_bias, onorm, sink, w_out, state_delta, cache_kv,
                final_w):
    qb = PA_EVEN
    kb = qb + W_B
    vb = kb + HKV_B * HD
    zb = vb + HKV_B * HD
    w_main, w_small = _regroup(w_in, [(0, 4 * W_A), (qb, kb), (zb, zb + W_B), (kb, vb), (vb, zb)],
                               (4 * W_A, PA_EVEN))
    p, small = _in_proj(x, norm_w, mod, w_main, w_small)
    a_row = _lane_row(a_log, 2 * H_A)
    dt_row = _lane_row(dt_bias, 2 * H_A)
    oa_c, st = _delta_mixer(p, small, conv_a, a_row, dt_row, onorm, None, SEQ, BATCH, 0)
    oa_l, _ = _delta_mixer(p, small, conv_a, a_row, dt_row, onorm, state_delta[:, e], DEC_SEQ, DEC_BATCH, N_CTX)
    ob_c = _ctx_attn(p, _EV_QB // W_B, _EV_KB // (HKV_B * HD), _EV_VB // (HKV_B * HD), _EV_ZB // W_B,
                     H_B, HKV_B, sink)
    ck = cache_kv[:, e, 0].reshape(DEC_BATCH, PAST_LEN, HKV_B * HD)
    cv = cache_kv[:, e, 1].reshape(DEC_BATCH, PAST_LEN, HKV_B * HD)
    ob_l = _win_attn(p, ck, cv, sink, _EV_QB // W_B, _EV_KB // (HKV_B * HD), _EV_VB // (HKV_B * HD),
                     _EV_ZB // W_B)
    o_a = jnp.concatenate([oa_c, oa_l], axis=0)
    o_b = jnp.concatenate([ob_c, ob_l], axis=0)
    wo = w_out.astype(BF16)
    x = _out_proj(o_a, o_b, wo[:W_A], wo[W_A:], x, mod, final_w)
    pc = p[:N_CTX]
    k = pc[:, _EV_KB:_EV_VB].reshape(BATCH, SEQ, HKV_B, HD)
    v = pc[:, _EV_VB:_EV_VB + HKV_B * HD].reshape(BATCH, SEQ, HKV_B, HD)
    return x, st, jnp.stack([k, v], axis=1)


def _odd_layer(x, o_i, mod, norm_w, w_in, w_glr, b_glr, onorm, rpb, w_out, state_gla, cache_kv, final_w):
    lo = 2 * QK_C + 2 * W_C
    w_main, w_small = _regroup(w_in, [(0, lo), (PC_ODD, PC_ODD + 4 * W_D)], (lo, PC_ODD))
    p, small = _in_proj(x, norm_w, mod, w_main, w_small)
    w_gate = jnp.stack([jnp.pad(w_glr[0], ((0, LANE - GLA_RANK), (0, 0))),
                        jnp.pad(w_glr[1], ((GLA_RANK, LANE - 2 * GLA_RANK), (0, 0)))])
    oc_c, st = _gla_mixer(p, small, w_gate, b_glr, onorm, None, SEQ, BATCH, 0)
    oc_l, _ = _gla_mixer(p, small, w_gate, b_glr, onorm, state_gla[:, o_i], DEC_SEQ, DEC_BATCH, N_CTX)
    od_c = _ctx_attn(p, _OD_QD // W_D, _OD_KD // W_D, _OD_VD // W_D, _OD_ZD // W_D, H_D, H_D, None)
    ck = cache_kv[:, o_i, 0].reshape(DEC_BATCH, PAST_LEN, W_D)
    cv = cache_kv[:, o_i, 1].reshape(DEC_BATCH, PAST_LEN, W_D)
    od_l = _nbr_attn(p, ck, cv, _nbr_bias_table(rpb), _OD_QD // W_D, _OD_KD // W_D, _OD_VD // W_D, _OD_ZD // W_D)
    o_c = jnp.concatenate([oc_c, oc_l], axis=0)
    o_d = jnp.concatenate([od_c, od_l], axis=0)
    wo = w_out.astype(BF16)
    x = _out_proj(o_c, o_d, wo[:W_C], wo[W_C:], x, mod, final_w)
    pc = p[:N_CTX]
    k = pc[:, _OD_KD:_OD_VD].reshape(BATCH, SEQ, H_D, HD)
    v = pc[:, _OD_VD:_OD_ZD].reshape(BATCH, SEQ, H_D, HD)
    return x, st, jnp.stack([k, v], axis=1)


def kernel(x_prompt, x_sample, state_delta, cache_kv_win, state_gla, cache_kv_nbr, c, c_ctx, norm_w, w_ada, b_ada, w_in_even, conv_a, a_log_a, dt_bias_a, onorm_a, sink_b, w_out_even, w_in_odd, w_glr_c, b_glr_c, onorm_c, rpb_d, w_out_odd, final_norm_w):
    x = jnp.concatenate([x_prompt.reshape(N_CTX, D_MODEL), x_sample.reshape(N_LAT, D_MODEL)], axis=0)
    cond = jnp.concatenate([c_ctx[None, :], c, jnp.zeros((N_COND - 1 - DEC_BATCH, D_MODEL), F32)], axis=0)
    mods = _ada_mod(cond, w_ada, b_ada).reshape(DEPTH, N_COND, 1, 3 * D_MODEL)
    new_delta, new_kvw, new_gla, new_kvn = [], [], [], []
    for li in range(DEPTH):
        final_w = final_norm_w if li == DEPTH - 1 else None
        if li % 2 == 0:
            e = li // 2
            x, st, kv = _even_layer(x, e, mods[li], norm_w[li], w_in_even[e], conv_a[e], a_log_a[e], dt_bias_a[e],
                                    onorm_a[e], sink_b[e], w_out_even[e], state_delta, cache_kv_win, final_w)
            new_delta.append(st)
            new_kvw.append(kv)
        else:
            o_i = li // 2
            x, st, kv = _odd_layer(x, o_i, mods[li], norm_w[li], w_in_odd[o_i], w_glr_c[o_i], b_glr_c[o_i],
                                   onorm_c[o_i], rpb_d[o_i], w_out_odd[o_i], state_gla, cache_kv_nbr, final_w)
            new_gla.append(st)
            new_kvn.append(kv)
    y_prompt = x[:N_CTX].reshape(BATCH, SEQ, D_MODEL)
    y_sample = x[N_CTX:].reshape(DEC_BATCH, DEC_SEQ, D_MODEL)
    return (y_prompt, y_sample, jnp.stack(new_delta, 1), jnp.stack(new_kvw, 1),
            jnp.stack(new_gla, 1), jnp.stack(new_kvn, 1))
```

```python
import functools

import numpy as np
import jax
import jax.numpy as jnp
from jax import lax
from jax.experimental import pallas as pl
from jax.experimental.pallas import tpu as pltpu

F32 = jnp.float32
BF16 = jnp.bfloat16
HIGHEST = lax.Precision.HIGHEST

D_MODEL = 2048
BATCH = 16
SEQ = 256
DEPTH = 4
DEC_BATCH = 4
DEC_SEQ = 1024
PAST_LEN = 256
GRID_W = 64
HD = 128
EPS = 1e-6
NEG = -1e30
ROPE_THETA = 10000.0
CHUNK = 64
H_A = 8
W_A = H_A * HD
CONV_K = 5
H_B = 8
HKV_B = 2
W_B = H_B * HD
WIN = 128
QBLK = 128
H_C = 4
DK_C = 128
DV_C = 256
QK_C = H_C * DK_C
W_C = H_C * DV_C
GLA_RANK = 16
GLA_TAU = 16.0
H_D = 8
W_D = H_D * HD
NB_H = 8
NB_W = 16
N_EVEN = (DEPTH + 1) // 2
N_ODD = DEPTH // 2
PA_EVEN = 4 * W_A + 4 * H_A
P_EVEN = PA_EVEN + 2 * W_B + 2 * HKV_B * HD
PC_ODD = 2 * QK_C + 2 * W_C + 2 * GLA_RANK
P_ODD = PC_ODD + 4 * W_D

N_CTX = BATCH * SEQ
N_LAT = DEC_BATCH * DEC_SEQ
N_GRP = N_CTX
assert N_LAT == N_GRP
N_COND = 8
LANE = 128
TM_IN = 1024
TN_IN = 512
GATE_COLS = 32
TM_OUT = 256
TN_ADA = 512
VMEM_LIMIT = 56 * 1024 * 1024


def _cparams(sem):
    return pltpu.CompilerParams(dimension_semantics=sem, vmem_limit_bytes=VMEM_LIMIT)


def _sigmoid(x):
    return 1.0 / (1.0 + jnp.exp(-x))


def _silu(x):
    return x * _sigmoid(x)


def _dot(a, b):
    return jnp.dot(a.astype(BF16), b.astype(BF16), preferred_element_type=F32)


def _dot_nt(a, b):
    return lax.dot_general(a.astype(BF16), b.astype(BF16), (((1,), (1,)), ((), ())),
                           preferred_element_type=F32)


def _dot_tn(a, b):
    return lax.dot_general(a.astype(BF16), b.astype(BF16), (((0,), (0,)), ((), ())),
                           preferred_element_type=F32)


def _dot_hi(a, b):
    return jnp.dot(a, b, precision=HIGHEST, preferred_element_type=F32)


def _ada_kernel(c_ref, w_ref, b_ref, o_ref):
    o_ref[...] = _dot(_silu(c_ref[...]), w_ref[...]) + b_ref[...]


def _ada_mod(cond, w_ada, b_ada):
    n3 = 3 * D_MODEL
    return pl.pallas_call(
        _ada_kernel,
        grid=(DEPTH, n3 // TN_ADA),
        in_specs=[
            pl.BlockSpec((N_COND, D_MODEL), lambda l, j: (0, 0)),
            pl.BlockSpec((None, D_MODEL, TN_ADA), lambda l, j: (l, 0, j)),
            pl.BlockSpec((None, 1, TN_ADA), lambda l, j: (l, 0, j)),
        ],
        out_specs=pl.BlockSpec((None, N_COND, TN_ADA), lambda l, j: (l, 0, j)),
        out_shape=jax.ShapeDtypeStruct((DEPTH, N_COND, n3), F32),
        compiler_params=_cparams(("arbitrary", "arbitrary")),
        name="ada_mod",
    )(cond, w_ada, b_ada.reshape(DEPTH, 1, n3))


def _cond_row(i, tm, latent):
    return 1 + i // (DEC_SEQ // tm) if latent else 0


def _inproj_kernel(src_ref, x_ref, nw_ref, shift_ref, scale_ref, wa_ref, wb_ref, ws_ref, o_ref, os_ref,
                   h_ref, w_ref, *, n_aligned):
    j, i = pl.program_id(0), pl.program_id(1)
    rows = pl.ds(pl.multiple_of(i * TM_IN, TM_IN), TM_IN)

    @pl.when(j == 0)
    def _():
        x = x_ref[...]
        y = x * lax.rsqrt(jnp.mean(x * x, axis=-1, keepdims=True) + EPS) * nw_ref[...]
        h = (y * (1.0 + scale_ref[...]) + shift_ref[...]).astype(BF16)
        h_ref[rows, :] = h
        os_ref[...] = jnp.dot(h, ws_ref[...].astype(BF16), preferred_element_type=F32)

    @pl.when((i == 0) & (j < n_aligned))
    def _():
        w_ref[...] = wa_ref[...].astype(BF16)

    @pl.when((i == 0) & (j >= n_aligned))
    def _():
        lane = lax.broadcasted_iota(jnp.int32, (256, LANE), 1)
        for r in range(0, D_MODEL, 256):
            a = pltpu.roll(wa_ref[r:r + 256, :], TN_IN - GATE_COLS, 1)
            b = pltpu.roll(wb_ref[r:r + 256, :], LANE - GATE_COLS, 1)
            tail = jnp.where(lane >= LANE - GATE_COLS, b, a[:, TN_IN - LANE:])
            w_ref[r:r + 256, :] = jnp.concatenate([a[:, :TN_IN - LANE], tail], axis=1).astype(BF16)

    o_ref[...] = jnp.dot(h_ref[rows, :], w_ref[...], preferred_element_type=F32)


def _in_proj(x, norm_w, mod, w_in, src_tiles, n_aligned, gate_col, latent):
    nj, ni = len(src_tiles), N_GRP // TM_IN
    sub = TN_IN // LANE
    row = functools.partial(_cond_row, tm=TM_IN, latent=latent)
    tok = lambda j, i, s: (jnp.where(j == 0, i, ni - 1), 0)
    grid_spec = pltpu.PrefetchScalarGridSpec(
        num_scalar_prefetch=1,
        grid=(nj, ni),
        in_specs=[
            pl.BlockSpec((TM_IN, D_MODEL), tok),
            pl.BlockSpec((1, D_MODEL), lambda j, i, s: (0, 0)),
            pl.BlockSpec((None, 1, D_MODEL), lambda j, i, s: (row(jnp.where(j == 0, i, ni - 1)), 0, 0)),
            pl.BlockSpec((None, 1, D_MODEL), lambda j, i, s: (row(jnp.where(j == 0, i, ni - 1)), 0, 1)),
            pl.BlockSpec((D_MODEL, TN_IN), lambda j, i, s: (0, s[j])),
            pl.BlockSpec((D_MODEL, LANE), lambda j, i, s: (0, (s[j] + 1) * sub)),
            pl.BlockSpec((D_MODEL, LANE), lambda j, i, s: (0, gate_col // LANE)),
        ],
        out_specs=[
            pl.BlockSpec((TM_IN, TN_IN), lambda j, i, s: (i, j)),
            pl.BlockSpec((TM_IN, LANE), tok),
        ],
        scratch_shapes=[pltpu.VMEM((N_GRP, D_MODEL), BF16), pltpu.VMEM((D_MODEL, TN_IN), BF16)],
    )
    return pl.pallas_call(
        functools.partial(_inproj_kernel, n_aligned=n_aligned),
        grid_spec=grid_spec,
        out_shape=[jax.ShapeDtypeStruct((N_GRP, nj * TN_IN), F32),
                   jax.ShapeDtypeStruct((N_GRP, LANE), F32)],
        compiler_params=_cparams(("arbitrary", "arbitrary")),
        name="in_proj_lat" if latent else "in_proj_ctx",
    )(jnp.asarray(src_tiles, jnp.int32), x, norm_w.reshape(1, D_MODEL), mod, mod, w_in, w_in, w_in)


def _outproj_kernel(oa_ref, ob_ref, wa_ref, wb_ref, x_ref, g_ref, *rest, final):
    acc = _dot(oa_ref[...], wa_ref[...]) + _dot(ob_ref[...], wb_ref[...])
    xn = x_ref[...] + g_ref[...] * acc
    if final:
        fw_ref, y_ref = rest
        y_ref[...] = xn * lax.rsqrt(jnp.mean(xn * xn, axis=-1, keepdims=True) + EPS) * fw_ref[...]
    else:
        (y_ref,) = rest
        y_ref[...] = xn


def _out_proj(o_a, o_b, w_a, w_b, x, mod, latent, final_w=None):
    ka, kb = o_a.shape[1], o_b.shape[1]
    row = functools.partial(_cond_row, tm=TM_OUT, latent=latent)
    final = final_w is not None
    in_specs = [
        pl.BlockSpec((TM_OUT, ka), lambda i: (i, 0)),
        pl.BlockSpec((TM_OUT, kb), lambda i: (i, 0)),
        pl.BlockSpec((ka, D_MODEL), lambda i: (0, 0)),
        pl.BlockSpec((kb, D_MODEL), lambda i: (0, 0)),
        pl.BlockSpec((TM_OUT, D_MODEL), lambda i: (i, 0)),
        pl.BlockSpec((None, 1, D_MODEL), lambda i: (row(i), 0, 2)),
    ]
    args = [o_a, o_b, w_a, w_b, x, mod]
    if final:
        in_specs.append(pl.BlockSpec((1, D_MODEL), lambda i: (0, 0)))
        args.append(final_w.reshape(1, D_MODEL))
    return pl.pallas_call(
        functools.partial(_outproj_kernel, final=final),
        grid=(N_GRP // TM_OUT,),
        in_specs=in_specs,
        out_specs=pl.BlockSpec((TM_OUT, D_MODEL), lambda i: (i, 0)),
        out_shape=jax.ShapeDtypeStruct((N_GRP, D_MODEL), F32),
        compiler_params=_cparams(("arbitrary",)),
        name="out_proj_final" if final else "out_proj",
    )(*args)


def _attend(problems, scale):
    scores = [[_dot_nt(q, k) for k in ks] for q, ks, _, _, _, _ in problems]
    outs = []
    parts = []
    for (q, ks, vs, masks, biases, sink), raw in zip(problems, scores):
        ss = []
        for s, m, bias in zip(raw, masks, biases):
            s = s * scale
            if bias is not None:
                s = s + bias
            if m is not None:
                s = jnp.where(m, s, NEG)
            ss.append(s)
        mx = functools.reduce(jnp.maximum, [jnp.max(s, axis=-1, keepdims=True) for s in ss])
        if sink is not None:
            mx = jnp.maximum(mx, sink)
        es = [jnp.exp(s - mx) for s in ss]
        den = functools.reduce(jnp.add, [jnp.sum(e, axis=-1, keepdims=True) for e in es])
        if sink is not None:
            den = den + jnp.exp(sink - mx)
        parts.append((es, den))
    pvs = [[_dot(e, v) for e, v in zip(es, vs)] for (es, _), (_, _, vs, _, _, _) in zip(parts, problems)]
    for pv, (_, den) in zip(pvs, parts):
        outs.append(functools.reduce(jnp.add, pv) / den)
    return outs


def _head(ref, h, rows=None):
    if rows is None:
        return ref[:, h * HD:(h + 1) * HD]
    return ref[rows, h * HD:(h + 1) * HD]


def _ctx_attn_kernel(*refs, heads, kv_heads, use_sink):
    if use_sink:
        q_ref, k_ref, v_ref, z_ref, sink_ref, o_ref, kv_ref = refs
    else:
        q_ref, k_ref, v_ref, z_ref, o_ref, kv_ref = refs
    g = heads // kv_heads
    n = q_ref.shape[0]
    problems = []
    for j in range(kv_heads):
        q = jnp.concatenate([_head(q_ref, j * g + t) for t in range(g)], axis=0)
        sink = None
        if use_sink:
            sink = jnp.concatenate([jnp.full((n, 1), sink_ref[j * g + t], F32) for t in range(g)], axis=0)
        problems.append((q, [_head(k_ref, j)], [_head(v_ref, j)], [None], [None], sink))
    outs = _attend(problems, HD ** -0.5)
    for j, o in enumerate(outs):
        for t in range(g):
            h = j * g + t
            o_ref[:, h * HD:(h + 1) * HD] = o[t * n:(t + 1) * n] * _silu(_head(z_ref, h))
    kv_ref[0] = k_ref[...]
    kv_ref[1] = v_ref[...]


def _ctx_attn(p, q_col, k_col, v_col, z_col, heads, kv_heads, sink):
    wq, wkv = heads * HD, kv_heads * HD
    use_sink = sink is not None
    in_specs = [
        pl.BlockSpec((SEQ, wq), lambda b: (b, q_col)),
        pl.BlockSpec((SEQ, wkv), lambda b: (b, k_col)),
        pl.BlockSpec((SEQ, wkv), lambda b: (b, v_col)),
        pl.BlockSpec((SEQ, wq), lambda b: (b, z_col)),
    ]
    args = [p, p, p, p]
    if use_sink:
        in_specs.append(pl.BlockSpec(memory_space=pltpu.SMEM))
        args.append(sink)
    return pl.pallas_call(
        functools.partial(_ctx_attn_kernel, heads=heads, kv_heads=kv_heads, use_sink=use_sink),
        grid=(BATCH,),
        in_specs=in_specs,
        out_specs=[pl.BlockSpec((SEQ, wq), lambda b: (b, 0)),
                   pl.BlockSpec((None, 2, SEQ, wkv), lambda b: (b, 0, 0, 0))],
        out_shape=[jax.ShapeDtypeStruct((N_CTX, wq), F32),
                   jax.ShapeDtypeStruct((BATCH, 2, SEQ, wkv), F32)],
        compiler_params=_cparams(("arbitrary",)),
        name="ctx_attn_sink" if use_sink else "ctx_attn",
    )(*args)


def _rope_tables():
    half = HD // 4
    freq = (ROPE_THETA ** (-np.arange(half, dtype=np.float32) / half)).astype(np.float32)
    t = np.arange(DEC_SEQ)
    ang_r = (t // GRID_W).astype(np.float32)[:, None] * freq[None, :]
    ang_c = (t % GRID_W).astype(np.float32)[:, None] * freq[None, :]
    cos = np.concatenate([np.cos(ang_r)] * 2 + [np.cos(ang_c)] * 2, axis=1).astype(np.float32)
    sin_r, sin_c, zero = np.sin(ang_r), np.sin(ang_c), np.zeros_like(ang_r)
    s_up = np.concatenate([-sin_r, zero, -sin_c, zero], axis=1).astype(np.float32)
    s_dn = np.concatenate([zero, sin_r, zero, sin_c], axis=1).astype(np.float32)
    return jnp.asarray(cos), jnp.asarray(s_up), jnp.asarray(s_dn)


def _rope(x, cos, s_up, s_dn):
    return x * cos + pltpu.roll(x, HD - HD // 4, 1) * s_up + pltpu.roll(x, HD // 4, 1) * s_dn


def _win_attn_kernel(q_ref, k_ref, v_ref, kc_ref, vc_ref, z_ref, cos_ref, sup_ref, sdn_ref, sink_ref, o_ref):
    i = pl.program_id(1)
    g = H_B // HKV_B
    span = QBLK + 2 * WIN
    start = pl.multiple_of(jnp.clip(i * QBLK - WIN, 0, DEC_SEQ - span), QBLK)
    qrows = pl.ds(pl.multiple_of(i * QBLK, QBLK), QBLK)
    krows = pl.ds(start, span)
    cq, uq, dq = cos_ref[qrows, :], sup_ref[qrows, :], sdn_ref[qrows, :]
    ck, uk, dk = cos_ref[krows, :], sup_ref[krows, :], sdn_ref[krows, :]
    qpos = i * QBLK + lax.broadcasted_iota(jnp.int32, (g * QBLK, span), 0) % QBLK
    kpos = start + lax.broadcasted_iota(jnp.int32, (g * QBLK, span), 1)
    band = jnp.abs(qpos - kpos) <= WIN
    problems = []
    for j in range(HKV_B):
        q = jnp.concatenate([_rope(_head(q_ref, j * g + t), cq, uq, dq) for t in range(g)], axis=0)
        kw = _rope(_head(k_ref, j, krows), ck, uk, dk)
        vw = _head(v_ref, j, krows)
        sink = jnp.concatenate([jnp.full((QBLK, 1), sink_ref[j * g + t], F32) for t in range(g)], axis=0)
        problems.append((q, [kw, _head(kc_ref, j)], [vw, _head(vc_ref, j)], [band, None], [None, None], sink))
    outs = _attend(problems, HD ** -0.5)
    for j, o in enumerate(outs):
        for t in range(g):
            h = j * g + t
            o_ref[:, h * HD:(h + 1) * HD] = o[t * QBLK:(t + 1) * QBLK] * _silu(_head(z_ref, h))


def _win_attn(p, cache_k, cache_v, sink, q_col, k_col, v_col, z_col):
    wq, wkv = H_B * HD, HKV_B * HD
    nq = DEC_SEQ // QBLK
    cos, s_up, s_dn = _rope_tables()
    full = pl.BlockSpec((DEC_SEQ, HD), lambda b, i: (0, 0))
    return pl.pallas_call(
        _win_attn_kernel,
        grid=(DEC_BATCH, nq),
        in_specs=[
            pl.BlockSpec((QBLK, wq), lambda b, i: (b * nq + i, q_col)),
            pl.BlockSpec((DEC_SEQ, wkv), lambda b, i: (b, k_col)),
            pl.BlockSpec((DEC_SEQ, wkv), lambda b, i: (b, v_col)),
            pl.BlockSpec((None, PAST_LEN, wkv), lambda b, i: (b, 0, 0)),
            pl.BlockSpec((None, PAST_LEN, wkv), lambda b, i: (b, 0, 0)),
            pl.BlockSpec((QBLK, wq), lambda b, i: (b * nq + i, z_col)),
            full, full, full,
            pl.BlockSpec(memory_space=pltpu.SMEM),
        ],
        out_specs=pl.BlockSpec((QBLK, wq), lambda b, i: (b * nq + i, 0)),
        out_shape=jax.ShapeDtypeStruct((N_LAT, wq), F32),
        compiler_params=_cparams(("arbitrary", "arbitrary")),
        name="win_attn",
    )(p, p, p, cache_k, cache_v, p, cos, s_up, s_dn, sink)


def _nbr_onehot():
    qc = np.arange(GRID_W)[:, None]
    kc = np.arange(GRID_W)[None, :]
    idx = np.clip(kc - qc, -(NB_W - 1), NB_W - 1) + NB_W - 1
    e = (np.arange(2 * NB_W)[:, None, None] == idx[None]).astype(np.float32)
    return jnp.asarray(e.reshape(2 * NB_W, GRID_W * GRID_W))


def _bias_expand_kernel(r_ref, e_ref, o_ref):
    o_ref[...] = _dot_hi(r_ref[...], e_ref[...])


def _nbr_bias_table(rpb):
    rows = H_D * (2 * NB_H - 1)
    r = jnp.pad(rpb.reshape(rows, 2 * NB_W - 1), ((0, 128 - rows), (0, 1)))
    t = pl.pallas_call(
        _bias_expand_kernel,
        out_shape=jax.ShapeDtypeStruct((128, GRID_W * GRID_W), F32),
        name="nbr_bias_expand",
    )(r, _nbr_onehot())
    t = t[:rows].reshape(H_D, 2 * NB_H - 1, GRID_W, GRID_W)
    return jnp.concatenate([t[:, :-1], t[:, 1:]], axis=-1)


def _nbr_attn_kernel(q_ref, k_ref, v_ref, kc_ref, vc_ref, z_ref, t_ref, o_ref):
    r = pl.program_id(1)
    rows = DEC_SEQ // GRID_W
    rs = jnp.clip(r - NB_H // 2, 0, rows - NB_H)
    dr0 = rs - r + NB_H - 1
    nk = NB_H * GRID_W
    krows = pl.ds(pl.multiple_of(rs * GRID_W, GRID_W), nk)
    qc = lax.broadcasted_iota(jnp.int32, (GRID_W, nk), 0)
    kc = lax.broadcasted_iota(jnp.int32, (GRID_W, nk), 1) % GRID_W
    cstart = jnp.clip(qc - NB_W // 2, 0, GRID_W - NB_W)
    ok = (kc >= cstart) & (kc < cstart + NB_W)
    problems = []
    for h in range(H_D):
        bias = jnp.concatenate([t_ref[h, dr0 + 2 * m] for m in range(nk // LANE)], axis=1)
        problems.append((_head(q_ref, h), [_head(k_ref, h, krows), _head(kc_ref, h)],
                         [_head(v_ref, h, krows), _head(vc_ref, h)], [ok, None], [bias, None], None))
    outs = _attend(problems, HD ** -0.5)
    for h, o in enumerate(outs):
        o_ref[:, h * HD:(h + 1) * HD] = o * _silu(_head(z_ref, h))


def _nbr_attn(p, cache_k, cache_v, table, q_col, k_col, v_col, z_col):
    rows = DEC_SEQ // GRID_W
    return pl.pallas_call(
        _nbr_attn_kernel,
        grid=(DEC_BATCH, rows),
        in_specs=[
            pl.BlockSpec((GRID_W, W_D), lambda b, r: (b * rows + r, q_col)),
            pl.BlockSpec((DEC_SEQ, W_D), lambda b, r: (b, k_col)),
            pl.BlockSpec((DEC_SEQ, W_D), lambda b, r: (b, v_col)),
            pl.BlockSpec((None, PAST_LEN, W_D), lambda b, r: (b, 0, 0)),
            pl.BlockSpec((None, PAST_LEN, W_D), lambda b, r: (b, 0, 0)),
            pl.BlockSpec((GRID_W, W_D), lambda b, r: (b * rows + r, z_col)),
            pl.BlockSpec(table.shape, lambda b, r: (0, 0, 0, 0)),
        ],
        out_specs=pl.BlockSpec((GRID_W, W_D), lambda b, r: (b * rows + r, 0)),
        out_shape=jax.ShapeDtypeStruct((N_LAT, W_D), F32),
        compiler_params=_cparams(("arbitrary", "arbitrary")),
        name="nbr_attn",
    )(p, p, p, cache_k, cache_v, p, table)


def _lane_col(x, idx):
    lane = lax.broadcasted_iota(jnp.int32, x.shape, 1)
    return jnp.sum(jnp.where(lane == idx, x, 0.0), axis=-1, keepdims=True)


def _short_conv(x_ref, w_ref, pad_ref, n):
    pad = CONV_K // 2
    zeros = jnp.zeros((8, HD), F32)
    pad_ref[0:8, :] = zeros
    pad_ref[n + 8:n + 16, :] = zeros
    pad_ref[8:n + 8, :] = x_ref[...]
    y = functools.reduce(jnp.add, [pad_ref[8 - pad + t:8 - pad + t + n, :] * w_ref[t:t + 1, :]
                                   for t in range(CONV_K)])
    return _silu(y)


def _l2norm(x):
    return x * lax.rsqrt(jnp.sum(x * x, axis=-1, keepdims=True) + EPS)


def _split2(x):
    hi = x.astype(BF16)
    return hi, (x - hi.astype(F32)).astype(BF16)


def _dots_x2(pairs):
    ops = []
    for a, b in pairs:
        a_hi, a_lo = _split2(a)
        b_hi, b_lo = _split2(b)
        ops.append((jnp.concatenate([a_hi, a_lo], axis=0), jnp.concatenate([b_hi, b_lo], axis=1)))
    rs = [jnp.dot(a2, b2, preferred_element_type=F32) for a2, b2 in ops]
    out = []
    for (a, b), r in zip(pairs, rs):
        m, n = a.shape[0], b.shape[1]
        out.append((r[m:, :n] + r[:m, n:] + r[m:, n:]) + r[:m, :n])
    return out


def _pair_masks():
    c = lax.broadcasted_iota(jnp.int32, (CHUNK, HD), 0)
    l = lax.broadcasted_iota(jnp.int32, (CHUNK, HD), 1)
    left = l < CHUNK
    j = l % CHUNK
    ahead = jnp.where(left, j - c, c - j)
    return left, ahead <= 0, ahead < 0, j == c


def _block_diag(x, left):
    return jnp.concatenate([jnp.where(left, x, 0.0), jnp.where(left, 0.0, x)], axis=0)


def _inv_unit_triangular_pairs(lmats, left, eye):
    mps = [-x for x in lmats]
    ps = [jnp.where(eye, 1.0, 0.0) + m for m in mps]
    mps = _dots_x2([(m, _block_diag(m, left)) for m in mps])
    for _ in range(4):
        rs = _dots_x2([(jnp.concatenate([p, m], axis=0), _block_diag(m, left)) for p, m in zip(ps, mps)])
        ps = [p + r[:CHUNK] for p, r in zip(ps, rs)]
        mps = [r[CHUNK:] for r in rs]
    rs = _dots_x2([(p, _block_diag(m, left)) for p, m in zip(ps, mps)])
    return [p + r for p, r in zip(ps, rs)]


def _chunk_cumsum(x, reverse):
    n = x.shape[0]
    pos = lax.broadcasted_iota(jnp.int32, x.shape, 0) % CHUNK
    k = 1
    while k < CHUNK:
        if reverse:
            x = x + jnp.where(pos < CHUNK - k, pltpu.roll(x, n - k, 0), 0.0)
        else:
            x = x + jnp.where(pos >= k, pltpu.roll(x, k, 0), 0.0)
        k *= 2
    return x


def _delta_prep(items):
    left, tri, strict, eye = _pair_masks()
    zeros = jnp.zeros((CHUNK, HD), F32)
    pre = []
    for q, k, v, cols in items:
        b_f, b_b, g_f, g_b = (cols[:, t:t + 1] for t in range(4))
        gsel = jnp.where(left, g_f, g_b)
        g_row = jnp.sum(jnp.where(eye, gsel, 0.0), axis=0, keepdims=True)
        decay = jnp.exp(jnp.where(tri, gsel - g_row, NEG))
        kb_f, kb_b = k * b_f, k * b_b
        lhs = jnp.concatenate([jnp.concatenate([kb_f, kb_b], axis=1), jnp.concatenate([q, q], axis=1)], axis=0)
        rhs = jnp.concatenate([jnp.concatenate([k, zeros], axis=1), jnp.concatenate([zeros, k], axis=1)], axis=0)
        pre.append((decay, kb_f, kb_b, g_f, g_b, b_f, b_b, lhs, rhs))
    kqs = [_dot_nt(x[7], x[8]) for x in pre]
    lmats = [jnp.where(strict, kq[:CHUNK] * x[0], 0.0) for kq, x in zip(kqs, pre)]
    attns = [jnp.where(tri, kq[CHUNK:] * x[0], 0.0) for kq, x in zip(kqs, pre)]
    tinvs = _inv_unit_triangular_pairs(lmats, left, eye)
    egs, rhs = [], []
    for (q, k, v, cols), x in zip(items, pre):
        _, kb_f, kb_b, g_f, g_b, b_f, b_b = x[:7]
        eg_f = jnp.exp(jnp.broadcast_to(g_f, (CHUNK, HD)))
        eg_b = jnp.exp(jnp.broadcast_to(g_b, (CHUNK, HD)))
        egs.append((eg_f, eg_b))
        rhs.append(jnp.concatenate([jnp.concatenate([v * b_f, kb_f * eg_f, zeros, zeros], axis=1),
                                    jnp.concatenate([zeros, zeros, v * b_b, kb_b * eg_b], axis=1)], axis=0))
    uws = _dots_x2(list(zip(tinvs, rhs)))
    out = []
    for (q, k, v, cols), x, attn, uw, (eg_f, eg_b) in zip(items, pre, attns, uws, egs):
        g_f, g_b = x[3], x[4]
        gl_f = jnp.broadcast_to(g_f[CHUNK - 1:CHUNK, :], (1, HD))
        gl_b = jnp.broadcast_to(g_b[0:1, :], (1, HD))
        kg_t = jnp.concatenate([k * jnp.exp(gl_f - g_f), k * jnp.exp(gl_b - g_b)], axis=1).T
        out.append(dict(u=[uw[:, :HD], uw[:, 2 * HD:3 * HD]], w=[uw[:, HD:2 * HD], uw[:, 3 * HD:]],
                        attn=[attn[:, :CHUNK], attn[:, CHUNK:]], qg=[q * eg_f, q * eg_b],
                        kg_t=[kg_t[:HD], kg_t[HD:]], eg=[jnp.exp(gl_f), jnp.exp(gl_b)]))
    return out


def _delta_kernel(*refs, n, hb, cg, has_s0):
    if has_s0:
        (qp_ref, kp_ref, vp_ref, z_ref, sm_ref, cq_ref, ck_ref, cv_ref, al_ref, dt_ref, on_ref, s0_ref,
         o_ref, so_ref, q_s, k_s, v_s, pad_s, col_s, u_s, w_s, at_s, qg_s, kg_s, eg_s, acc_s, st_s) = refs
    else:
        (qp_ref, kp_ref, vp_ref, z_ref, sm_ref, cq_ref, ck_ref, cv_ref, al_ref, dt_ref, on_ref,
         o_ref, so_ref, q_s, k_s, v_s, pad_s, col_s, u_s, w_s, at_s, qg_s, kg_s, eg_s, acc_s, st_s) = refs
    h0 = pl.program_id(1) * hb
    nc = n // CHUNK
    sm = sm_ref[...]
    x = sm + dt_ref[...]
    softplus = jnp.maximum(x, 0.0) + jnp.log(1.0 + jnp.exp(-jnp.abs(x)))
    gates = -jnp.exp(al_ref[...]) * softplus
    betas = _sigmoid(sm)
    cum_f = _chunk_cumsum(gates, False)
    cum_b = _chunk_cumsum(gates, True)
    lane = lax.broadcasted_iota(jnp.int32, (n, HD), 1)
    for j in range(hb):
        cols = slice(j * HD, (j + 1) * HD)
        q_s[j] = _l2norm(_short_conv(qp_ref.at[:, cols], cq_ref.at[:, cols], pad_s, n)) * HD ** -0.5
        k_s[j] = _l2norm(_short_conv(kp_ref.at[:, cols], ck_ref.at[:, cols], pad_s, n))
        v_s[j] = _short_conv(vp_ref.at[:, cols], cv_ref.at[:, cols], pad_s, n)
        col_s[j] = jnp.where(lane == 0, _lane_col(betas, h0 + j),
                             jnp.where(lane == 1, _lane_col(betas, H_A + h0 + j),
                                       jnp.where(lane == 2, _lane_col(cum_f, 2 * H_A + h0 + j),
                                                 _lane_col(cum_b, 3 * H_A + h0 + j))))
    acc_s[...] = jnp.zeros((hb, n, HD), F32)
    if has_s0:
        st_s[...] = s0_ref[...]
    else:
        st_s[...] = jnp.zeros((2, hb, HD, HD), F32)

    def prep(ci, carry):
        where = [(j, ci * cg + t) for j in range(hb) for t in range(cg)]
        rows = [pl.ds(pl.multiple_of(c * CHUNK, CHUNK), CHUNK) for _, c in where]
        outs = _delta_prep([(q_s[j, r, :], k_s[j, r, :], v_s[j, r, :], col_s[j, r, :])
                            for (j, _), r in zip(where, rows)])
        for (j, c), r, o in zip(where, rows, outs):
            for d in range(2):
                u_s[d, j, r, :] = o["u"][d]
                w_s[d, j, r, :] = o["w"][d].astype(BF16)
                at_s[d, j, r, :] = o["attn"][d].astype(BF16)
                qg_s[d, j, r, :] = o["qg"][d].astype(BF16)
                kg_s[d, j, c] = o["kg_t"][d].astype(BF16)
                eg_s[d, j, c] = jnp.broadcast_to(o["eg"][d], (8, HD))
        return carry

    lax.fori_loop(0, nc // cg, prep, 0)

    def scan(i, carry):
        chains = [(d, j, (nc - 1 - i) if d else i) for d in range(2) for j in range(hb)]
        rows = [pl.ds(pl.multiple_of(c * CHUNK, CHUNK), CHUNK) for _, _, c in chains]
        ss = [st_s[d, j] for d, j, _ in chains]
        sbs = [s.astype(BF16) for s in ss]
        ws = [jnp.dot(w_s[d, j, r, :], sb, preferred_element_type=F32) for (d, j, _), r, sb in zip(chains, rows, sbs)]
        vns = [(u_s[d, j, r, :] - w).astype(BF16) for (d, j, _), r, w in zip(chains, rows, ws)]
        for (d, j, c), r, s, sb, vn in zip(chains, rows, ss, sbs, vns):
            st_s[d, j] = s * eg_s[d, j, c, 0:1, :] + jnp.dot(kg_s[d, j, c], vn, preferred_element_type=F32)
        for (d, j, c), r, sb, vn in zip(chains, rows, sbs, vns):
            acc_s[j, r, :] += (jnp.dot(qg_s[d, j, r, :], sb, preferred_element_type=F32)
                               + jnp.dot(at_s[d, j, r, :], vn, preferred_element_type=F32))
        return carry

    lax.fori_loop(0, nc, scan, 0)
    for j in range(hb):
        cols = slice(j * HD, (j + 1) * HD)
        o = acc_s[j]
        o = o * lax.rsqrt(jnp.mean(o * o, axis=-1, keepdims=True) + EPS) * on_ref[...]
        o_ref[:, cols] = o * _silu(z_ref[:, cols])
    so_ref[...] = st_s[...]


def _delta_mixer(p, small, conv_w, a_row, dt_row, onorm, s0, seq, nb, hb, cg):
    has_s0 = s0 is not None
    wb = hb * HD
    ng = H_A // hb
    nc = seq // CHUNK
    col = lambda off: (lambda b, h: (b, off + h))
    in_specs = [
        pl.BlockSpec((seq, wb), col(0)),
        pl.BlockSpec((seq, wb), col(ng)),
        pl.BlockSpec((seq, wb), col(2 * ng)),
        pl.BlockSpec((seq, wb), col(3 * ng)),
        pl.BlockSpec((seq, LANE), lambda b, h: (b, 0)),
        pl.BlockSpec((CONV_K, wb), lambda b, h: (0, h)),
        pl.BlockSpec((CONV_K, wb), lambda b, h: (0, ng + h)),
        pl.BlockSpec((CONV_K, wb), lambda b, h: (0, 2 * ng + h)),
        pl.BlockSpec((1, LANE), lambda b, h: (0, 0)),
        pl.BlockSpec((1, LANE), lambda b, h: (0, 0)),
        pl.BlockSpec((1, HD), lambda b, h: (0, 0)),
    ]
    args = [p, p, p, p, small, conv_w, conv_w, conv_w, a_row, dt_row, onorm.reshape(1, HD)]
    if has_s0:
        in_specs.append(pl.BlockSpec((None, 2, hb, HD, HD), lambda b, h: (b, 0, h, 0, 0)))
        args.append(s0)
    return pl.pallas_call(
        functools.partial(_delta_kernel, n=seq, hb=hb, cg=cg, has_s0=has_s0),
        grid=(nb, ng),
        in_specs=in_specs,
        out_specs=[
            pl.BlockSpec((seq, wb), lambda b, h: (b, h)),
            pl.BlockSpec((None, 2, hb, HD, HD), lambda b, h: (b, 0, h, 0, 0)),
        ],
        out_shape=[jax.ShapeDtypeStruct((nb * seq, W_A), F32),
                   jax.ShapeDtypeStruct((nb, 2, H_A, HD, HD), F32)],
        scratch_shapes=[
            pltpu.VMEM((hb, seq, HD), F32), pltpu.VMEM((hb, seq, HD), F32), pltpu.VMEM((hb, seq, HD), F32),
            pltpu.VMEM((seq + 16, HD), F32), pltpu.VMEM((hb, seq, HD), F32),
            pltpu.VMEM((2, hb, seq, HD), F32), pltpu.VMEM((2, hb, seq, HD), BF16),
            pltpu.VMEM((2, hb, seq, CHUNK), BF16), pltpu.VMEM((2, hb, seq, HD), BF16),
            pltpu.VMEM((2, hb, nc, HD, CHUNK), BF16), pltpu.VMEM((2, hb, nc, 8, HD), F32),
            pltpu.VMEM((hb, seq, HD), F32), pltpu.VMEM((2, hb, HD, HD), F32)],
        compiler_params=_cparams(("arbitrary", "arbitrary")),
        name="delta_lat" if has_s0 else "delta_ctx",
    )(*args)


_GLA_LEVELS = (32, 16, 8, 4, 2, 1)
GLA_CHUNKS = 2


def _gla_consts(reverse):
    r = np.arange(CHUNK)
    flip = (lambda a: a[::-1, ::-1]) if reverse else (lambda a: a)
    sel, hi, pair = [], [], []
    for m in _GLA_LEVELS:
        mid = (r // (2 * m)) * (2 * m) + m
        is_hi = r >= mid
        sel.append(flip(r[None, :] == mid[:, None] - 1))
        hi.append(flip(is_hi[:, None]))
        pair.append(flip((r[:, None] // (2 * m) == r[None, :] // (2 * m)) & is_hi[:, None] & ~is_hi[None, :]))
    out = [np.concatenate(sel), np.concatenate(hi), np.stack(pair)]
    out = [jnp.asarray(np.ascontiguousarray(a).astype(np.float32)) for a in out]
    return [out[0].astype(BF16), out[1], out[2]]


def _split3(x):
    hi = x.astype(BF16)
    r1 = x - hi.astype(F32)
    mid = r1.astype(BF16)
    return hi, mid, (r1 - mid.astype(F32)).astype(BF16)


def _gla_prep(items):
    nl = len(_GLA_LEVELS)
    r3s = [jnp.dot(consts[0][...], jnp.concatenate(_split3(b), axis=1), preferred_element_type=F32)
           for _, _, _, b, consts, _ in items]
    refs = [(r3[:, 2 * DK_C:] + r3[:, DK_C:2 * DK_C]) + r3[:, :DK_C] for r3 in r3s]
    c = lax.broadcasted_iota(jnp.int32, (CHUNK, CHUNK), 0)
    j = lax.broadcasted_iota(jnp.int32, (CHUNK, CHUNK), 1)
    lvl = []
    for (q, k, v, b, consts, _), ref in zip(items, refs):
        ops = []
        for lv in range(nl):
            rows = slice(lv * CHUNK, (lv + 1) * CHUNK)
            hi = consts[1][rows, :] > 0.5
            t = b - ref[rows]
            e = jnp.exp(jnp.where(hi, t, -t))
            ops.append((jnp.where(hi, q * e, 0.0), jnp.where(hi, 0.0, k * e)))
        lvl.append(ops)
    prods = [[_dot_nt(ql, kl) for ql, kl in ops] for ops in lvl]
    amats = []
    for (q, k, v, b, consts, _), pr in zip(items, prods):
        a = jnp.where(c == j, jnp.sum(q * k, axis=-1, keepdims=True), 0.0)
        for lv in range(nl):
            a = a + pr[lv] * consts[2][lv]
        amats.append(a)
    intras = [_dot(a, it[2]) for a, it in zip(amats, items)]
    r128 = lax.broadcasted_iota(jnp.int32, (DK_C, DK_C), 0)
    c128 = lax.broadcasted_iota(jnp.int32, (DK_C, DK_C), 1)
    out = []
    for (q, k, v, b, consts, reverse), intra in zip(items, intras):
        last = 0 if reverse else CHUNK - 1
        bl = b[last:last + 1, :]
        dec = jnp.sum(jnp.where(r128 == c128, jnp.broadcast_to(jnp.exp(bl), (DK_C, DK_C)), 0.0),
                      axis=-1, keepdims=True)
        out.append((intra, q * jnp.exp(b), dec, _dot_tn(k * jnp.exp(bl - b), v)))
    return out


def _gla_kernel(*refs, n, cg, has_s0):
    consts_f, consts_b = refs[8:11], refs[11:14]
    q_ref, k_ref, v_ref, z_ref, sm_ref, wg_ref, bg_ref, on_ref = refs[:8]
    if has_s0:
        s0_ref, o_ref, so_ref, gk_s, acc_s, st_s = refs[14:]
    else:
        o_ref, so_ref, gk_s, acc_s, st_s = refs[14:]
    nc = n // CHUNK
    sm = sm_ref[...]
    for d in range(2):
        x = _dot_hi(sm, wg_ref[d]) + bg_ref[d]
        gk = (jnp.minimum(x, 0.0) - jnp.log(1.0 + jnp.exp(-jnp.abs(x)))) / GLA_TAU
        gk_s[d] = _chunk_cumsum(gk, bool(d))
    acc_s[...] = jnp.zeros((n, DV_C), F32)
    if has_s0:
        st_s[...] = s0_ref[...]
    else:
        st_s[...] = jnp.zeros((2, DK_C, DV_C), F32)

    def body(i, carry):
        where = [(d, (nc - 1 - (i * cg + t)) if d else (i * cg + t)) for d in range(2) for t in range(cg)]
        rows = [pl.ds(pl.multiple_of(c * CHUNK, CHUNK), CHUNK) for _, c in where]
        outs = _gla_prep([(q_ref[r, :] * DK_C ** -0.5, k_ref[r, :], v_ref[r, :], gk_s[d, r, :],
                           consts_b if d else consts_f, bool(d)) for (d, _), r in zip(where, rows)])
        for d in range(2):
            s = st_s[d]
            for t in range(cg):
                intra, qe, dec, kv = outs[d * cg + t]
                acc_s[rows[d * cg + t], :] += intra + _dot(qe, s)
                s = s * dec + kv
            st_s[d] = s
        return carry

    lax.fori_loop(0, nc // cg, body, 0)
    o = acc_s[...]
    o = o * lax.rsqrt(jnp.mean(o * o, axis=-1, keepdims=True) + EPS) * on_ref[...]
    o_ref[...] = o * _silu(z_ref[...])
    so_ref[...] = st_s[...]


def _gla_mixer(p, small, w_gate, b_gate, onorm, s0, seq, nb):
    has_s0 = s0 is not None
    consts = _gla_consts(False) + _gla_consts(True)
    const_specs = [pl.BlockSpec(a.shape, (lambda b, h, nd=a.ndim: (0,) * nd)) for a in consts]
    in_specs = [
        pl.BlockSpec((seq, DK_C), lambda b, h: (b, h)),
        pl.BlockSpec((seq, DK_C), lambda b, h: (b, QK_C // DK_C + h)),
        pl.BlockSpec((seq, DV_C), lambda b, h: (b, 2 * QK_C // DV_C + h)),
        pl.BlockSpec((seq, DV_C), lambda b, h: (b, (2 * QK_C + W_C) // DV_C + h)),
        pl.BlockSpec((seq, LANE), lambda b, h: (b, 0)),
        pl.BlockSpec((2, LANE, DK_C), lambda b, h: (0, 0, h)),
        pl.BlockSpec((2, 1, DK_C), lambda b, h: (0, 0, h)),
        pl.BlockSpec((1, DV_C), lambda b, h: (0, 0)),
    ] + const_specs
    args = [p, p, p, p, small, w_gate, b_gate.reshape(2, 1, QK_C), onorm.reshape(1, DV_C)] + consts
    if has_s0:
        in_specs.append(pl.BlockSpec((None, 2, None, DK_C, DV_C), lambda b, h: (b, 0, h, 0, 0)))
        args.append(s0)
    return pl.pallas_call(
        functools.partial(_gla_kernel, n=seq, cg=GLA_CHUNKS, has_s0=has_s0),
        grid=(nb, H_C),
        in_specs=in_specs,
        out_specs=[
            pl.BlockSpec((seq, DV_C), lambda b, h: (b, h)),
            pl.BlockSpec((None, 2, None, DK_C, DV_C), lambda b, h: (b, 0, h, 0, 0)),
        ],
        out_shape=[jax.ShapeDtypeStruct((nb * seq, W_C), F32),
                   jax.ShapeDtypeStruct((nb, 2, H_C, DK_C, DV_C), F32)],
        scratch_shapes=[pltpu.VMEM((2, seq, DK_C), F32), pltpu.VMEM((seq, DV_C), F32),
                        pltpu.VMEM((2, DK_C, DV_C), F32)],
        compiler_params=_cparams(("arbitrary", "arbitrary")),
        name="gla_lat" if has_s0 else "gla_ctx",
    )(*args)


_EV_TILES = tuple(range(4 * W_A // TN_IN)) + (8, 9, 11, 12, 10)
_EV_ALIGNED = 4 * W_A // TN_IN
_EV_QB = 4 * W_A
_EV_ZB = _EV_QB + W_B
_EV_KB = _EV_ZB + W_B
_EV_VB = _EV_KB + HKV_B * HD
_OD_ALIGNED = (2 * QK_C + 2 * W_C) // TN_IN
_OD_TILES = tuple(range((P_ODD - 2 * GLA_RANK) // TN_IN))
_OD_QD = 2 * QK_C + 2 * W_C
_OD_KD = _OD_QD + W_D
_OD_VD = _OD_KD + W_D
_OD_ZD = _OD_VD + W_D


def _lane_row(v, offset):
    return jnp.pad(v.reshape(1, -1), ((0, 0), (offset, LANE - offset - v.size)))


def _even_layer(xc, xl, e, mod, norm_w, w_in, conv_a, a_log, dt_bias, onorm, sink, w_out, state_delta, cache_kv,
                final_w):
    pc, sc = _in_proj(xc, norm_w, mod, w_in, _EV_TILES, _EV_ALIGNED, 4 * W_A, False)
    pq, sq = _in_proj(xl, norm_w, mod, w_in, _EV_TILES, _EV_ALIGNED, 4 * W_A, True)
    a_row = _lane_row(a_log, 2 * H_A)
    dt_row = _lane_row(dt_bias, 2 * H_A)
    oa_c, st = _delta_mixer(pc, sc, conv_a, a_row, dt_row, onorm, None, SEQ, BATCH, 4, 1)
    oa_l, _ = _delta_mixer(pq, sq, conv_a, a_row, dt_row, onorm, state_delta[:, e], DEC_SEQ, DEC_BATCH, 2, 2)
    wkv = HKV_B * HD
    ob_c, kv = _ctx_attn(pc, _EV_QB // W_B, _EV_KB // wkv, _EV_VB // wkv, _EV_ZB // W_B, H_B, HKV_B, sink)
    ck = cache_kv[:, e, 0].reshape(DEC_BATCH, PAST_LEN, wkv)
    cv = cache_kv[:, e, 1].reshape(DEC_BATCH, PAST_LEN, wkv)
    ob_l = _win_attn(pq, ck, cv, sink, _EV_QB // W_B, _EV_KB // wkv, _EV_VB // wkv, _EV_ZB // W_B)
    wo = w_out.astype(BF16)
    xc = _out_proj(oa_c, ob_c, wo[:W_A], wo[W_A:], xc, mod, False, final_w)
    xl = _out_proj(oa_l, ob_l, wo[:W_A], wo[W_A:], xl, mod, True, final_w)
    return xc, xl, st, kv.reshape(BATCH, 2, SEQ, HKV_B, HD)


def _odd_layer(xc, xl, o_i, mod, norm_w, w_in, w_glr, b_glr, onorm, rpb, w_out, state_gla, cache_kv, final_w):
    lo = 2 * QK_C + 2 * W_C
    pc, sc = _in_proj(xc, norm_w, mod, w_in, _OD_TILES, _OD_ALIGNED, lo, False)
    pq, sq = _in_proj(xl, norm_w, mod, w_in, _OD_TILES, _OD_ALIGNED, lo, True)
    w_gate = jnp.stack([jnp.pad(w_glr[0], ((0, LANE - GLA_RANK), (0, 0))),
                        jnp.pad(w_glr[1], ((GLA_RANK, LANE - 2 * GLA_RANK), (0, 0)))])
    oc_c, st = _gla_mixer(pc, sc, w_gate, b_glr, onorm, None, SEQ, BATCH)
    oc_l, _ = _gla_mixer(pq, sq, w_gate, b_glr, onorm, state_gla[:, o_i], DEC_SEQ, DEC_BATCH)
    od_c, kv = _ctx_attn(pc, _OD_QD // W_D, _OD_KD // W_D, _OD_VD // W_D, _OD_ZD // W_D, H_D, H_D, None)
    ck = cache_kv[:, o_i, 0].reshape(DEC_BATCH, PAST_LEN, W_D)
    cv = cache_kv[:, o_i, 1].reshape(DEC_BATCH, PAST_LEN, W_D)
    od_l = _nbr_attn(pq, ck, cv, _nbr_bias_table(rpb), _OD_QD // W_D, _OD_KD // W_D, _OD_VD // W_D, _OD_ZD // W_D)
    wo = w_out.astype(BF16)
    xc = _out_proj(oc_c, od_c, wo[:W_C], wo[W_C:], xc, mod, False, final_w)
    xl = _out_proj(oc_l, od_l, wo[:W_C], wo[W_C:], xl, mod, True, final_w)
    return xc, xl, st, kv.reshape(BATCH, 2, SEQ, H_D, HD)


def kernel(x_prompt, x_sample, state_delta, cache_kv_win, state_gla, cache_kv_nbr, c, c_ctx, norm_w, w_ada, b_ada, w_in_even, conv_a, a_log_a, dt_bias_a, onorm_a, sink_b, w_out_even, w_in_odd, w_glr_c, b_glr_c, onorm_c, rpb_d, w_out_odd, final_norm_w):
    xc = x_prompt.reshape(N_CTX, D_MODEL)
    xl = x_sample.reshape(N_LAT, D_MODEL)
    cond = jnp.concatenate([c_ctx[None, :], c, jnp.zeros((N_COND - 1 - DEC_BATCH, D_MODEL), F32)], axis=0)
    mods = _ada_mod(cond, w_ada, b_ada).reshape(DEPTH, N_COND, 1, 3 * D_MODEL)
    new_delta, new_kvw, new_gla, new_kvn = [], [], [], []
    for li in range(DEPTH):
        final_w = final_norm_w if li == DEPTH - 1 else None
        if li % 2 == 0:
            e = li // 2
            xc, xl, st, kv = _even_layer(xc, xl, e, mods[li], norm_w[li], w_in_even[e], conv_a[e], a_log_a[e],
                                         dt_bias_a[e], onorm_a[e], sink_b[e], w_out_even[e], state_delta,
                                         cache_kv_win, final_w)
            new_delta.append(st)
            new_kvw.append(kv)
        else:
            o_i = li // 2
            xc, xl, st, kv = _odd_layer(xc, xl, o_i, mods[li], norm_w[li], w_in_odd[o_i], w_glr_c[o_i],
                                        b_glr_c[o_i], onorm_c[o_i], rpb_d[o_i], w_out_odd[o_i], state_gla,
                                        cache_kv_nbr, final_w)
            new_gla.append(st)
            new_kvn.append(kv)
    return (xc.reshape(BATCH, SEQ, D_MODEL), xl.reshape(DEC_BATCH, DEC_SEQ, D_MODEL),
            jnp.stack(new_delta, 1), jnp.stack(new_kvw, 1), jnp.stack(new_gla, 1), jnp.stack(new_kvn, 1))
```

```python
import functools

import numpy as np
import jax
import jax.numpy as jnp
from jax import lax
from jax.experimental import pallas as pl
from jax.experimental.pallas import tpu as pltpu

F32 = jnp.float32
BF16 = jnp.bfloat16
HIGHEST = lax.Precision.HIGHEST

D_MODEL = 2048
BATCH = 16
SEQ = 256
DEPTH = 4
DEC_BATCH = 4
DEC_SEQ = 1024
PAST_LEN = 256
GRID_W = 64
HD = 128
EPS = 1e-6
NEG = -1e30
ROPE_THETA = 10000.0
CHUNK = 64
H_A = 8
W_A = H_A * HD
CONV_K = 5
H_B = 8
HKV_B = 2
W_B = H_B * HD
WIN = 128
QBLK = 128
H_C = 4
DK_C = 128
DV_C = 256
QK_C = H_C * DK_C
W_C = H_C * DV_C
GLA_RANK = 16
GLA_TAU = 16.0
H_D = 8
W_D = H_D * HD
NB_H = 8
NB_W = 16
N_EVEN = (DEPTH + 1) // 2
N_ODD = DEPTH // 2
PA_EVEN = 4 * W_A + 4 * H_A
P_EVEN = PA_EVEN + 2 * W_B + 2 * HKV_B * HD
PC_ODD = 2 * QK_C + 2 * W_C + 2 * GLA_RANK
P_ODD = PC_ODD + 4 * W_D

N_CTX = BATCH * SEQ
N_LAT = DEC_BATCH * DEC_SEQ
N_GRP = N_CTX
assert N_LAT == N_GRP
N_COND = 8
LANE = 128
TM_IN = 1024
TN_IN = 512
GATE_COLS = 32
TM_OUT = 512
TN_ADA = 2048
VMEM_LIMIT = 56 * 1024 * 1024


def _cparams(sem):
    return pltpu.CompilerParams(dimension_semantics=sem, vmem_limit_bytes=VMEM_LIMIT)


def _sigmoid(x):
    return 1.0 / (1.0 + jnp.exp(-x))


def _silu(x):
    return x * _sigmoid(x)


def _dot(a, b):
    return jnp.dot(a.astype(BF16), b.astype(BF16), preferred_element_type=F32)


def _dot_nt(a, b):
    return lax.dot_general(a.astype(BF16), b.astype(BF16), (((1,), (1,)), ((), ())),
                           preferred_element_type=F32)


def _dot_tn(a, b):
    return lax.dot_general(a.astype(BF16), b.astype(BF16), (((0,), (0,)), ((), ())),
                           preferred_element_type=F32)


def _dot_hi(a, b):
    return jnp.dot(a, b, precision=HIGHEST, preferred_element_type=F32)


def _ada_kernel(c_ref, w_ref, b_ref, o_ref):
    o_ref[...] = _dot(_silu(c_ref[...]), w_ref[...]) + b_ref[...]


def _ada_mod(cond, w_ada, b_ada):
    n3 = 3 * D_MODEL
    return pl.pallas_call(
        _ada_kernel,
        grid=(DEPTH, n3 // TN_ADA),
        in_specs=[
            pl.BlockSpec((N_COND, D_MODEL), lambda l, j: (0, 0)),
            pl.BlockSpec((None, D_MODEL, TN_ADA), lambda l, j: (l, 0, j)),
            pl.BlockSpec((None, 1, TN_ADA), lambda l, j: (l, 0, j)),
        ],
        out_specs=pl.BlockSpec((None, N_COND, TN_ADA), lambda l, j: (l, 0, j)),
        out_shape=jax.ShapeDtypeStruct((DEPTH, N_COND, n3), F32),
        compiler_params=_cparams(("arbitrary", "arbitrary")),
        name="ada_mod",
    )(cond, w_ada, b_ada.reshape(DEPTH, 1, n3))


def _cond_row(i, tm, latent):
    return 1 + i // (DEC_SEQ // tm) if latent else 0


def _inproj_kernel(src_ref, x_ref, nw_ref, shift_ref, scale_ref, wa_ref, wb_ref, ws_ref, o_ref, os_ref,
                   h_ref, w_ref, *, n_aligned):
    j, i = pl.program_id(0), pl.program_id(1)
    rows = pl.ds(pl.multiple_of(i * TM_IN, TM_IN), TM_IN)

    @pl.when(j == 0)
    def _():
        x = x_ref[...]
        y = x * lax.rsqrt(jnp.mean(x * x, axis=-1, keepdims=True) + EPS) * nw_ref[...]
        h = (y * (1.0 + scale_ref[...]) + shift_ref[...]).astype(BF16)
        h_ref[rows, :] = h
        os_ref[...] = jnp.dot(h, ws_ref[...].astype(BF16), preferred_element_type=F32)

    @pl.when((i == 0) & (j < n_aligned))
    def _():
        w_ref[...] = wa_ref[...].astype(BF16)

    @pl.when((i == 0) & (j >= n_aligned))
    def _():
        lane = lax.broadcasted_iota(jnp.int32, (256, LANE), 1)
        for r in range(0, D_MODEL, 256):
            a = pltpu.roll(wa_ref[r:r + 256, :], TN_IN - GATE_COLS, 1)
            b = pltpu.roll(wb_ref[r:r + 256, :], LANE - GATE_COLS, 1)
            tail = jnp.where(lane >= LANE - GATE_COLS, b, a[:, TN_IN - LANE:])
            w_ref[r:r + 256, :] = jnp.concatenate([a[:, :TN_IN - LANE], tail], axis=1).astype(BF16)

    o_ref[...] = jnp.dot(h_ref[rows, :], w_ref[...], preferred_element_type=F32)


def _in_proj(x, norm_w, mod, w_in, layer, src_tiles, n_aligned, gate_col, latent):
    nj, ni = len(src_tiles), N_GRP // TM_IN
    sub = TN_IN // LANE
    row = functools.partial(_cond_row, tm=TM_IN, latent=latent)
    tok = lambda j, i, s: (jnp.where(j == 0, i, ni - 1), 0)
    grid_spec = pltpu.PrefetchScalarGridSpec(
        num_scalar_prefetch=1,
        grid=(nj, ni),
        in_specs=[
            pl.BlockSpec((TM_IN, D_MODEL), tok),
            pl.BlockSpec((1, D_MODEL), lambda j, i, s: (0, 0)),
            pl.BlockSpec((None, 1, D_MODEL), lambda j, i, s: (row(jnp.where(j == 0, i, ni - 1)), 0, 0)),
            pl.BlockSpec((None, 1, D_MODEL), lambda j, i, s: (row(jnp.where(j == 0, i, ni - 1)), 0, 1)),
            pl.BlockSpec((None, D_MODEL, TN_IN), lambda j, i, s: (layer, 0, s[j])),
            pl.BlockSpec((None, D_MODEL, LANE), lambda j, i, s: (layer, 0, (s[j] + 1) * sub)),
            pl.BlockSpec((None, D_MODEL, LANE), lambda j, i, s: (layer, 0, gate_col // LANE)),
        ],
        out_specs=[
            pl.BlockSpec((TM_IN, TN_IN), lambda j, i, s: (i, j)),
            pl.BlockSpec((TM_IN, LANE), tok),
        ],
        scratch_shapes=[pltpu.VMEM((N_GRP, D_MODEL), BF16), pltpu.VMEM((D_MODEL, TN_IN), BF16)],
    )
    return pl.pallas_call(
        functools.partial(_inproj_kernel, n_aligned=n_aligned),
        grid_spec=grid_spec,
        out_shape=[jax.ShapeDtypeStruct((N_GRP, nj * TN_IN), F32),
                   jax.ShapeDtypeStruct((N_GRP, LANE), F32)],
        compiler_params=_cparams(("arbitrary", "arbitrary")),
        name="in_proj_lat" if latent else "in_proj_ctx",
    )(jnp.asarray(src_tiles, jnp.int32), x, norm_w.reshape(1, D_MODEL), mod, mod, w_in, w_in, w_in)


def _outproj_kernel(oa_ref, ob_ref, wa_ref, wb_ref, x_ref, g_ref, *rest, final):
    acc = _dot(oa_ref[...], wa_ref[...]) + _dot(ob_ref[...], wb_ref[...])
    xn = x_ref[...] + g_ref[...] * acc
    if final:
        fw_ref, y_ref = rest
        y_ref[...] = xn * lax.rsqrt(jnp.mean(xn * xn, axis=-1, keepdims=True) + EPS) * fw_ref[...]
    else:
        (y_ref,) = rest
        y_ref[...] = xn


def _out_proj(o_a, o_b, w_out, layer, x, mod, latent, final_w=None):
    ka, kb = o_a.shape[1], o_b.shape[1]
    assert ka == kb and w_out.shape[1] == ka + kb
    row = functools.partial(_cond_row, tm=TM_OUT, latent=latent)
    final = final_w is not None
    in_specs = [
        pl.BlockSpec((TM_OUT, ka), lambda i: (i, 0)),
        pl.BlockSpec((TM_OUT, kb), lambda i: (i, 0)),
        pl.BlockSpec((None, ka, D_MODEL), lambda i: (layer, 0, 0)),
        pl.BlockSpec((None, kb, D_MODEL), lambda i: (layer, 1, 0)),
        pl.BlockSpec((TM_OUT, D_MODEL), lambda i: (i, 0)),
        pl.BlockSpec((None, 1, D_MODEL), lambda i: (row(i), 0, 2)),
    ]
    args = [o_a, o_b, w_out, w_out, x, mod]
    if final:
        in_specs.append(pl.BlockSpec((1, D_MODEL), lambda i: (0, 0)))
        args.append(final_w.reshape(1, D_MODEL))
    return pl.pallas_call(
        functools.partial(_outproj_kernel, final=final),
        grid=(N_GRP // TM_OUT,),
        in_specs=in_specs,
        out_specs=pl.BlockSpec((TM_OUT, D_MODEL), lambda i: (i, 0)),
        out_shape=jax.ShapeDtypeStruct((N_GRP, D_MODEL), F32),
        compiler_params=_cparams(("arbitrary",)),
        name="out_proj_final" if final else "out_proj",
    )(*args)


def _attend(problems, scale):
    scores = [[_dot_nt(q, k) for k in ks] for q, ks, _, _, _, _ in problems]
    outs = []
    parts = []
    for (q, ks, vs, masks, biases, sink), raw in zip(problems, scores):
        ss = []
        for s, m, bias in zip(raw, masks, biases):
            s = s * scale
            if bias is not None:
                s = s + bias
            if m is not None:
                s = jnp.where(m, s, NEG)
            ss.append(s)
        mx = functools.reduce(jnp.maximum, [jnp.max(s, axis=-1, keepdims=True) for s in ss])
        if sink is not None:
            mx = jnp.maximum(mx, sink)
        es = [jnp.exp(s - mx) for s in ss]
        den = functools.reduce(jnp.add, [jnp.sum(e, axis=-1, keepdims=True) for e in es])
        if sink is not None:
            den = den + jnp.exp(sink - mx)
        parts.append((es, den))
    pvs = [[_dot(e, v) for e, v in zip(es, vs)] for (es, _), (_, _, vs, _, _, _) in zip(parts, problems)]
    for pv, (_, den) in zip(pvs, parts):
        outs.append(functools.reduce(jnp.add, pv) / den)
    return outs


def _head(ref, h, rows=None):
    if rows is None:
        return ref[:, h * HD:(h + 1) * HD]
    return ref[rows, h * HD:(h + 1) * HD]


def _ctx_attn_kernel(*refs, heads, kv_heads, use_sink):
    if use_sink:
        q_ref, k_ref, v_ref, z_ref, sink_ref, o_ref, kv_ref = refs
    else:
        q_ref, k_ref, v_ref, z_ref, o_ref, kv_ref = refs
    g = heads // kv_heads
    n = q_ref.shape[0]
    problems = []
    for j in range(kv_heads):
        q = jnp.concatenate([_head(q_ref, j * g + t) for t in range(g)], axis=0)
        sink = None
        if use_sink:
            sink = jnp.concatenate([jnp.full((n, 1), sink_ref[j * g + t], F32) for t in range(g)], axis=0)
        problems.append((q, [_head(k_ref, j)], [_head(v_ref, j)], [None], [None], sink))
    outs = _attend(problems, HD ** -0.5)
    for j, o in enumerate(outs):
        for t in range(g):
            h = j * g + t
            o_ref[:, h * HD:(h + 1) * HD] = o[t * n:(t + 1) * n] * _silu(_head(z_ref, h))
    kv_ref[0] = k_ref[...]
    kv_ref[1] = v_ref[...]


def _ctx_attn(p, q_col, k_col, v_col, z_col, heads, kv_heads, sink):
    wq, wkv = heads * HD, kv_heads * HD
    use_sink = sink is not None
    in_specs = [
        pl.BlockSpec((SEQ, wq), lambda b: (b, q_col)),
        pl.BlockSpec((SEQ, wkv), lambda b: (b, k_col)),
        pl.BlockSpec((SEQ, wkv), lambda b: (b, v_col)),
        pl.BlockSpec((SEQ, wq), lambda b: (b, z_col)),
    ]
    args = [p, p, p, p]
    if use_sink:
        in_specs.append(pl.BlockSpec(memory_space=pltpu.SMEM))
        args.append(sink)
    return pl.pallas_call(
        functools.partial(_ctx_attn_kernel, heads=heads, kv_heads=kv_heads, use_sink=use_sink),
        grid=(BATCH,),
        in_specs=in_specs,
        out_specs=[pl.BlockSpec((SEQ, wq), lambda b: (b, 0)),
                   pl.BlockSpec((None, 2, SEQ, wkv), lambda b: (b, 0, 0, 0))],
        out_shape=[jax.ShapeDtypeStruct((N_CTX, wq), F32),
                   jax.ShapeDtypeStruct((BATCH, 2, SEQ, wkv), F32)],
        compiler_params=_cparams(("arbitrary",)),
        name="ctx_attn_sink" if use_sink else "ctx_attn",
    )(*args)


def _rope_tables():
    half = HD // 4
    freq = (ROPE_THETA ** (-np.arange(half, dtype=np.float32) / half)).astype(np.float32)
    t = np.arange(DEC_SEQ)
    ang_r = (t // GRID_W).astype(np.float32)[:, None] * freq[None, :]
    ang_c = (t % GRID_W).astype(np.float32)[:, None] * freq[None, :]
    cos = np.concatenate([np.cos(ang_r)] * 2 + [np.cos(ang_c)] * 2, axis=1).astype(np.float32)
    sin_r, sin_c, zero = np.sin(ang_r), np.sin(ang_c), np.zeros_like(ang_r)
    s_up = np.concatenate([-sin_r, zero, -sin_c, zero], axis=1).astype(np.float32)
    s_dn = np.concatenate([zero, sin_r, zero, sin_c], axis=1).astype(np.float32)
    return jnp.asarray(cos), jnp.asarray(s_up), jnp.asarray(s_dn)


def _rope(x, cos, s_up, s_dn):
    return x * cos + pltpu.roll(x, HD - HD // 4, 1) * s_up + pltpu.roll(x, HD // 4, 1) * s_dn


def _win_attn_kernel(q_ref, k_ref, v_ref, kc_ref, vc_ref, z_ref, cos_ref, sup_ref, sdn_ref, sink_ref, o_ref):
    i = pl.program_id(1)
    g = H_B // HKV_B
    span = QBLK + 2 * WIN
    start = pl.multiple_of(jnp.clip(i * QBLK - WIN, 0, DEC_SEQ - span), QBLK)
    qrows = pl.ds(pl.multiple_of(i * QBLK, QBLK), QBLK)
    krows = pl.ds(start, span)
    cq, uq, dq = cos_ref[qrows, :], sup_ref[qrows, :], sdn_ref[qrows, :]
    ck, uk, dk = cos_ref[krows, :], sup_ref[krows, :], sdn_ref[krows, :]
    qpos = i * QBLK + lax.broadcasted_iota(jnp.int32, (g * QBLK, span), 0) % QBLK
    kpos = start + lax.broadcasted_iota(jnp.int32, (g * QBLK, span), 1)
    band = jnp.abs(qpos - kpos) <= WIN
    problems = []
    for j in range(HKV_B):
        q = jnp.concatenate([_rope(_head(q_ref, j * g + t), cq, uq, dq) for t in range(g)], axis=0)
        kw = _rope(_head(k_ref, j, krows), ck, uk, dk)
        vw = _head(v_ref, j, krows)
        sink = jnp.concatenate([jnp.full((QBLK, 1), sink_ref[j * g + t], F32) for t in range(g)], axis=0)
        problems.append((q, [kw, _head(kc_ref, j)], [vw, _head(vc_ref, j)], [band, None], [None, None], sink))
    outs = _attend(problems, HD ** -0.5)
    for j, o in enumerate(outs):
        for t in range(g):
            h = j * g + t
            o_ref[:, h * HD:(h + 1) * HD] = o[t * QBLK:(t + 1) * QBLK] * _silu(_head(z_ref, h))


def _win_attn(p, cache_k, cache_v, sink, q_col, k_col, v_col, z_col):
    wq, wkv = H_B * HD, HKV_B * HD
    nq = DEC_SEQ // QBLK
    cos, s_up, s_dn = _rope_tables()
    full = pl.BlockSpec((DEC_SEQ, HD), lambda b, i: (0, 0))
    return pl.pallas_call(
        _win_attn_kernel,
        grid=(DEC_BATCH, nq),
        in_specs=[
            pl.BlockSpec((QBLK, wq), lambda b, i: (b * nq + i, q_col)),
            pl.BlockSpec((DEC_SEQ, wkv), lambda b, i: (b, k_col)),
            pl.BlockSpec((DEC_SEQ, wkv), lambda b, i: (b, v_col)),
            pl.BlockSpec((None, PAST_LEN, wkv), lambda b, i: (b, 0, 0)),
            pl.BlockSpec((None, PAST_LEN, wkv), lambda b, i: (b, 0, 0)),
            pl.BlockSpec((QBLK, wq), lambda b, i: (b * nq + i, z_col)),
            full, full, full,
            pl.BlockSpec(memory_space=pltpu.SMEM),
        ],
        out_specs=pl.BlockSpec((QBLK, wq), lambda b, i: (b * nq + i, 0)),
        out_shape=jax.ShapeDtypeStruct((N_LAT, wq), F32),
        compiler_params=_cparams(("arbitrary", "arbitrary")),
        name="win_attn",
    )(p, p, p, cache_k, cache_v, p, cos, s_up, s_dn, sink)


def _nbr_onehot():
    qc = np.arange(GRID_W)[:, None]
    kc = np.arange(GRID_W)[None, :]
    idx = np.clip(kc - qc, -(NB_W - 1), NB_W - 1) + NB_W - 1
    e = (np.arange(2 * NB_W)[:, None, None] == idx[None]).astype(np.float32)
    return jnp.asarray(e.reshape(2 * NB_W, GRID_W * GRID_W))


def _bias_expand_kernel(r_ref, e_ref, o_ref):
    o_ref[...] = _dot_hi(r_ref[...], e_ref[...])


def _nbr_bias_table(rpb):
    rows = H_D * (2 * NB_H - 1)
    r = jnp.pad(rpb.reshape(rows, 2 * NB_W - 1), ((0, 128 - rows), (0, 1)))
    t = pl.pallas_call(
        _bias_expand_kernel,
        out_shape=jax.ShapeDtypeStruct((128, GRID_W * GRID_W), F32),
        name="nbr_bias_expand",
    )(r, _nbr_onehot())
    t = t[:rows].reshape(H_D, 2 * NB_H - 1, GRID_W, GRID_W)
    return jnp.concatenate([t[:, :-1], t[:, 1:]], axis=-1)


def _nbr_attn_kernel(q_ref, k_ref, v_ref, kc_ref, vc_ref, z_ref, t_ref, o_ref):
    r = pl.program_id(1)
    rows = DEC_SEQ // GRID_W
    rs = jnp.clip(r - NB_H // 2, 0, rows - NB_H)
    dr0 = rs - r + NB_H - 1
    nk = NB_H * GRID_W
    krows = pl.ds(pl.multiple_of(rs * GRID_W, GRID_W), nk)
    qc = lax.broadcasted_iota(jnp.int32, (GRID_W, nk), 0)
    kc = lax.broadcasted_iota(jnp.int32, (GRID_W, nk), 1) % GRID_W
    cstart = jnp.clip(qc - NB_W // 2, 0, GRID_W - NB_W)
    ok = (kc >= cstart) & (kc < cstart + NB_W)
    problems = []
    for h in range(H_D):
        bias = jnp.concatenate([t_ref[h, dr0 + 2 * m] for m in range(nk // LANE)], axis=1)
        problems.append((_head(q_ref, h), [_head(k_ref, h, krows), _head(kc_ref, h)],
                         [_head(v_ref, h, krows), _head(vc_ref, h)], [ok, None], [bias, None], None))
    outs = _attend(problems, HD ** -0.5)
    for h, o in enumerate(outs):
        o_ref[:, h * HD:(h + 1) * HD] = o * _silu(_head(z_ref, h))


def _nbr_attn(p, cache_k, cache_v, table, q_col, k_col, v_col, z_col):
    rows = DEC_SEQ // GRID_W
    return pl.pallas_call(
        _nbr_attn_kernel,
        grid=(DEC_BATCH, rows),
        in_specs=[
            pl.BlockSpec((GRID_W, W_D), lambda b, r: (b * rows + r, q_col)),
            pl.BlockSpec((DEC_SEQ, W_D), lambda b, r: (b, k_col)),
            pl.BlockSpec((DEC_SEQ, W_D), lambda b, r: (b, v_col)),
            pl.BlockSpec((None, PAST_LEN, W_D), lambda b, r: (b, 0, 0)),
            pl.BlockSpec((None, PAST_LEN, W_D), lambda b, r: (b, 0, 0)),
            pl.BlockSpec((GRID_W, W_D), lambda b, r: (b * rows + r, z_col)),
            pl.BlockSpec(table.shape, lambda b, r: (0, 0, 0, 0)),
        ],
        out_specs=pl.BlockSpec((GRID_W, W_D), lambda b, r: (b * rows + r, 0)),
        out_shape=jax.ShapeDtypeStruct((N_LAT, W_D), F32),
        compiler_params=_cparams(("arbitrary", "arbitrary")),
        name="nbr_attn",
    )(p, p, p, cache_k, cache_v, p, table)


def _lane_col(x, idx):
    lane = lax.broadcasted_iota(jnp.int32, x.shape, 1)
    return jnp.sum(jnp.where(lane == idx, x, 0.0), axis=-1, keepdims=True)


def _short_conv(x_ref, w_ref, pad_ref, n):
    pad = CONV_K // 2
    zeros = jnp.zeros((8, HD), F32)
    pad_ref[0:8, :] = zeros
    pad_ref[n + 8:n + 16, :] = zeros
    pad_ref[8:n + 8, :] = x_ref[...]
    y = functools.reduce(jnp.add, [pad_ref[8 - pad + t:8 - pad + t + n, :] * w_ref[t:t + 1, :]
                                   for t in range(CONV_K)])
    return _silu(y)


def _l2norm(x):
    return x * lax.rsqrt(jnp.sum(x * x, axis=-1, keepdims=True) + EPS)


def _split2(x):
    hi = x.astype(BF16)
    return hi, (x - hi.astype(F32)).astype(BF16)


def _dots_x2(pairs, split_b=True):
    ops = []
    for a, b in pairs:
        a_hi, a_lo = _split2(a)
        b2 = jnp.concatenate(_split2(b), axis=1) if split_b else b.astype(BF16)
        ops.append((jnp.concatenate([a_hi, a_lo], axis=0), b2))
    rs = [jnp.dot(a2, b2, preferred_element_type=F32) for a2, b2 in ops]
    out = []
    for (a, b), r in zip(pairs, rs):
        m, n = a.shape[0], b.shape[1]
        if split_b:
            out.append((r[m:, :n] + r[:m, n:] + r[m:, n:]) + r[:m, :n])
        else:
            out.append(r[m:] + r[:m])
    return out


def _pair_masks():
    c = lax.broadcasted_iota(jnp.int32, (CHUNK, HD), 0)
    l = lax.broadcasted_iota(jnp.int32, (CHUNK, HD), 1)
    left = l < CHUNK
    j = l % CHUNK
    ahead = jnp.where(left, j - c, c - j)
    return left, ahead <= 0, ahead < 0, j == c


def _block_diag(x, left):
    return jnp.concatenate([jnp.where(left, x, 0.0), jnp.where(left, 0.0, x)], axis=0)


def _inv_unit_triangular_pairs(lmats, left, eye):
    mps = [-x for x in lmats]
    ps = [jnp.where(eye, 1.0, 0.0) + m for m in mps]
    mps = _dots_x2([(m, _block_diag(m, left)) for m in mps])
    for _ in range(4):
        rs = _dots_x2([(jnp.concatenate([p, m], axis=0), _block_diag(m, left)) for p, m in zip(ps, mps)])
        ps = [p + r[:CHUNK] for p, r in zip(ps, rs)]
        mps = [r[CHUNK:] for r in rs]
    rs = _dots_x2([(p, _block_diag(m, left)) for p, m in zip(ps, mps)])
    return [p + r for p, r in zip(ps, rs)]


def _chunk_cumsum(x, reverse):
    n = x.shape[0]
    pos = lax.broadcasted_iota(jnp.int32, x.shape, 0) % CHUNK
    k = 1
    while k < CHUNK:
        if reverse:
            x = x + jnp.where(pos < CHUNK - k, pltpu.roll(x, n - k, 0), 0.0)
        else:
            x = x + jnp.where(pos >= k, pltpu.roll(x, k, 0), 0.0)
        k *= 2
    return x


def _delta_prep(items):
    left, tri, strict, eye = _pair_masks()
    zeros = jnp.zeros((CHUNK, HD), F32)
    pre = []
    for q, k, v, cols in items:
        b_f, b_b, g_f, g_b = (cols[:, t:t + 1] for t in range(4))
        gsel = jnp.where(left, g_f, g_b)
        g_row = jnp.sum(jnp.where(eye, gsel, 0.0), axis=0, keepdims=True)
        decay = jnp.exp(jnp.where(tri, gsel - g_row, NEG))
        kb_f, kb_b = k * b_f, k * b_b
        lhs = jnp.concatenate([jnp.concatenate([kb_f, kb_b], axis=1), jnp.concatenate([q, q], axis=1)], axis=0)
        rhs = jnp.concatenate([jnp.concatenate([k, zeros], axis=1), jnp.concatenate([zeros, k], axis=1)], axis=0)
        pre.append((decay, kb_f, kb_b, g_f, g_b, b_f, b_b, lhs, rhs))
    kqs = [_dot_nt(x[7], x[8]) for x in pre]
    lmats = [jnp.where(strict, kq[:CHUNK] * x[0], 0.0) for kq, x in zip(kqs, pre)]
    attns = [jnp.where(tri, kq[CHUNK:] * x[0], 0.0) for kq, x in zip(kqs, pre)]
    tinvs = _inv_unit_triangular_pairs(lmats, left, eye)
    egs, rhs = [], []
    for (q, k, v, cols), x in zip(items, pre):
        _, kb_f, kb_b, g_f, g_b, b_f, b_b = x[:7]
        eg_f = jnp.exp(jnp.broadcast_to(g_f, (CHUNK, HD)))
        eg_b = jnp.exp(jnp.broadcast_to(g_b, (CHUNK, HD)))
        egs.append((eg_f, eg_b))
        rhs.append(jnp.concatenate([jnp.concatenate([v * b_f, kb_f * eg_f, zeros, zeros], axis=1),
                                    jnp.concatenate([zeros, zeros, v * b_b, kb_b * eg_b], axis=1)], axis=0))
    uws = _dots_x2(list(zip(tinvs, rhs)), split_b=False)
    out = []
    for (q, k, v, cols), x, attn, uw, (eg_f, eg_b) in zip(items, pre, attns, uws, egs):
        g_f, g_b = x[3], x[4]
        gl_f = jnp.broadcast_to(g_f[CHUNK - 1:CHUNK, :], (1, HD))
        gl_b = jnp.broadcast_to(g_b[0:1, :], (1, HD))
        kg_t = jnp.concatenate([k * jnp.exp(gl_f - g_f), k * jnp.exp(gl_b - g_b)], axis=1).T
        out.append(dict(u=[uw[:, :HD], uw[:, 2 * HD:3 * HD]], w=[uw[:, HD:2 * HD], uw[:, 3 * HD:]],
                        attn=[attn[:, :CHUNK], attn[:, CHUNK:]], qg=[q * eg_f, q * eg_b],
                        kg_t=[kg_t[:HD], kg_t[HD:]], eg=[jnp.exp(gl_f), jnp.exp(gl_b)]))
    return out


def _delta_kernel(*refs, n, hb, cg, has_s0):
    qp_ref, kp_ref, vp_ref, z_ref, sm_ref, cq_ref, ck_ref, cv_ref, al_ref, dt_ref, on_ref = refs[:11]
    if has_s0:
        s0_ref, o_ref = refs[11:13]
        so_ref = None
    else:
        o_ref, so_ref = refs[11:13]
    q_s, k_s, v_s, pad_s, col_s, u_s, w_s, at_s, qg_s, kg_s, eg_s, acc_s, st_s, gate_s = refs[13:]
    h0 = pl.program_id(1) * hb
    nc = n // CHUNK

    @pl.when(pl.program_id(1) == 0)
    def _():
        sm = sm_ref[...]
        x = sm + dt_ref[...]
        softplus = jnp.maximum(x, 0.0) + jnp.log(1.0 + jnp.exp(-jnp.abs(x)))
        gates = -jnp.exp(al_ref[...]) * softplus
        gate_s[0] = _sigmoid(sm)
        gate_s[1] = _chunk_cumsum(gates, False)
        gate_s[2] = _chunk_cumsum(gates, True)

    betas, cum_f, cum_b = gate_s[0], gate_s[1], gate_s[2]
    lane = lax.broadcasted_iota(jnp.int32, (n, HD), 1)
    for j in range(hb):
        cols = slice(j * HD, (j + 1) * HD)
        q_s[j] = _l2norm(_short_conv(qp_ref.at[:, cols], cq_ref.at[:, cols], pad_s, n)) * HD ** -0.5
        k_s[j] = _l2norm(_short_conv(kp_ref.at[:, cols], ck_ref.at[:, cols], pad_s, n))
        v_s[j] = _short_conv(vp_ref.at[:, cols], cv_ref.at[:, cols], pad_s, n)
        col_s[j] = jnp.where(lane == 0, _lane_col(betas, h0 + j),
                             jnp.where(lane == 1, _lane_col(betas, H_A + h0 + j),
                                       jnp.where(lane == 2, _lane_col(cum_f, 2 * H_A + h0 + j),
                                                 _lane_col(cum_b, 3 * H_A + h0 + j))))
    acc_s[...] = jnp.zeros((hb, n, HD), F32)
    if has_s0:
        st_s[...] = s0_ref[...]
    else:
        st_s[...] = jnp.zeros((2, hb, HD, HD), F32)

    def prep(ci, carry):
        where = [(j, ci * cg + t) for j in range(hb) for t in range(cg)]
        rows = [pl.ds(pl.multiple_of(c * CHUNK, CHUNK), CHUNK) for _, c in where]
        outs = _delta_prep([(q_s[j, r, :], k_s[j, r, :], v_s[j, r, :], col_s[j, r, :])
                            for (j, _), r in zip(where, rows)])
        for (j, c), r, o in zip(where, rows, outs):
            for d in range(2):
                u_s[d, j, r, :] = o["u"][d]
                w_s[d, j, r, :] = o["w"][d].astype(BF16)
                at_s[d, j, r, :] = o["attn"][d].astype(BF16)
                qg_s[d, j, r, :] = o["qg"][d].astype(BF16)
                kg_s[d, j, c] = o["kg_t"][d].astype(BF16)
                eg_s[d, j, c] = jnp.broadcast_to(o["eg"][d], (8, HD))
        return carry

    lax.fori_loop(0, nc // cg, prep, 0)

    def scan(i, carry):
        chains = [(d, j, (nc - 1 - i) if d else i) for d in range(2) for j in range(hb)]
        rows = [pl.ds(pl.multiple_of(c * CHUNK, CHUNK), CHUNK) for _, _, c in chains]
        ss = [st_s[d, j] for d, j, _ in chains]
        sbs = [s.astype(BF16) for s in ss]
        ws = [jnp.dot(w_s[d, j, r, :], sb, preferred_element_type=F32) for (d, j, _), r, sb in zip(chains, rows, sbs)]
        vns = [(u_s[d, j, r, :] - w).astype(BF16) for (d, j, _), r, w in zip(chains, rows, ws)]
        for (d, j, c), r, s, sb, vn in zip(chains, rows, ss, sbs, vns):
            st_s[d, j] = s * eg_s[d, j, c, 0:1, :] + jnp.dot(kg_s[d, j, c], vn, preferred_element_type=F32)
        for (d, j, c), r, sb, vn in zip(chains, rows, sbs, vns):
            acc_s[j, r, :] += (jnp.dot(qg_s[d, j, r, :], sb, preferred_element_type=F32)
                               + jnp.dot(at_s[d, j, r, :], vn, preferred_element_type=F32))
        return carry

    lax.fori_loop(0, nc, scan, 0)
    for j in range(hb):
        cols = slice(j * HD, (j + 1) * HD)
        o = acc_s[j]
        o = o * lax.rsqrt(jnp.mean(o * o, axis=-1, keepdims=True) + EPS) * on_ref[...]
        o_ref[:, cols] = o * _silu(z_ref[:, cols])
    if so_ref is not None:
        so_ref[...] = st_s[...]


def _delta_mixer(p, small, conv_w, a_row, dt_row, onorm, s0, seq, nb, hb, cg):
    has_s0 = s0 is not None
    wb = hb * HD
    ng = H_A // hb
    nc = seq // CHUNK
    col = lambda off: (lambda b, h: (b, off + h))
    in_specs = [
        pl.BlockSpec((seq, wb), col(0)),
        pl.BlockSpec((seq, wb), col(ng)),
        pl.BlockSpec((seq, wb), col(2 * ng)),
        pl.BlockSpec((seq, wb), col(3 * ng)),
        pl.BlockSpec((seq, LANE), lambda b, h: (b, 0)),
        pl.BlockSpec((CONV_K, wb), lambda b, h: (0, h)),
        pl.BlockSpec((CONV_K, wb), lambda b, h: (0, ng + h)),
        pl.BlockSpec((CONV_K, wb), lambda b, h: (0, 2 * ng + h)),
        pl.BlockSpec((1, LANE), lambda b, h: (0, 0)),
        pl.BlockSpec((1, LANE), lambda b, h: (0, 0)),
        pl.BlockSpec((1, HD), lambda b, h: (0, 0)),
    ]
    args = [p, p, p, p, small, conv_w, conv_w, conv_w, a_row, dt_row, onorm.reshape(1, HD)]
    if has_s0:
        in_specs.append(pl.BlockSpec((None, 2, hb, HD, HD), lambda b, h: (b, 0, h, 0, 0)))
        args.append(s0)
    out_specs = [pl.BlockSpec((seq, wb), lambda b, h: (b, h))]
    out_shape = [jax.ShapeDtypeStruct((nb * seq, W_A), F32)]
    if not has_s0:
        out_specs.append(pl.BlockSpec((None, 2, hb, HD, HD), lambda b, h: (b, 0, h, 0, 0)))
        out_shape.append(jax.ShapeDtypeStruct((nb, 2, H_A, HD, HD), F32))
    return pl.pallas_call(
        functools.partial(_delta_kernel, n=seq, hb=hb, cg=cg, has_s0=has_s0),
        grid=(nb, ng),
        in_specs=in_specs,
        out_specs=out_specs,
        out_shape=out_shape,
        scratch_shapes=[
            pltpu.VMEM((hb, seq, HD), F32), pltpu.VMEM((hb, seq, HD), F32), pltpu.VMEM((hb, seq, HD), F32),
            pltpu.VMEM((seq + 16, HD), F32), pltpu.VMEM((hb, seq, HD), F32),
            pltpu.VMEM((2, hb, seq, HD), F32), pltpu.VMEM((2, hb, seq, HD), BF16),
            pltpu.VMEM((2, hb, seq, CHUNK), BF16), pltpu.VMEM((2, hb, seq, HD), BF16),
            pltpu.VMEM((2, hb, nc, HD, CHUNK), BF16), pltpu.VMEM((2, hb, nc, 8, HD), F32),
            pltpu.VMEM((hb, seq, HD), F32), pltpu.VMEM((2, hb, HD, HD), F32),
            pltpu.VMEM((3, seq, LANE), F32)],
        compiler_params=_cparams(("arbitrary", "arbitrary")),
        name="delta_lat" if has_s0 else "delta_ctx",
    )(*args)


_GLA_LEVELS = (32, 16, 8, 4, 2, 1)
GLA_CHUNKS = 2


def _gla_consts(reverse):
    r = np.arange(CHUNK)
    flip = (lambda a: a[::-1, ::-1]) if reverse else (lambda a: a)
    sel, hi, pair = [], [], []
    for m in _GLA_LEVELS:
        mid = (r // (2 * m)) * (2 * m) + m
        is_hi = r >= mid
        sel.append(flip(r[None, :] == mid[:, None] - 1))
        hi.append(flip(is_hi[:, None]))
        pair.append(flip((r[:, None] // (2 * m) == r[None, :] // (2 * m)) & is_hi[:, None] & ~is_hi[None, :]))
    out = [np.concatenate(sel), np.concatenate(hi), np.stack(pair)]
    out = [jnp.asarray(np.ascontiguousarray(a).astype(np.float32)) for a in out]
    return [out[0].astype(BF16), out[1], out[2]]


def _split3(x):
    hi = x.astype(BF16)
    r1 = x - hi.astype(F32)
    mid = r1.astype(BF16)
    return hi, mid, (r1 - mid.astype(F32)).astype(BF16)


def _gla_prep(items):
    nl = len(_GLA_LEVELS)
    r3s = [jnp.dot(consts[0][...], jnp.concatenate(_split3(b), axis=1), preferred_element_type=F32)
           for _, _, _, b, consts, _ in items]
    refs = [(r3[:, 2 * DK_C:] + r3[:, DK_C:2 * DK_C]) + r3[:, :DK_C] for r3 in r3s]
    c = lax.broadcasted_iota(jnp.int32, (CHUNK, CHUNK), 0)
    j = lax.broadcasted_iota(jnp.int32, (CHUNK, CHUNK), 1)
    lvl = []
    for (q, k, v, b, consts, _), ref in zip(items, refs):
        ops = []
        for lv in range(nl):
            rows = slice(lv * CHUNK, (lv + 1) * CHUNK)
            hi = consts[1][rows, :] > 0.5
            t = b - ref[rows]
            e = jnp.exp(jnp.where(hi, t, -t))
            ops.append((jnp.where(hi, q * e, 0.0), jnp.where(hi, 0.0, k * e)))
        lvl.append(ops)
    prods = [[_dot_nt(ql, kl) for ql, kl in ops] for ops in lvl]
    amats = []
    for (q, k, v, b, consts, _), pr in zip(items, prods):
        a = jnp.where(c == j, jnp.sum(q * k, axis=-1, keepdims=True), 0.0)
        for lv in range(nl):
            a = a + pr[lv] * consts[2][lv]
        amats.append(a)
    intras = [_dot(a, it[2]) for a, it in zip(amats, items)]
    r128 = lax.broadcasted_iota(jnp.int32, (DK_C, DK_C), 0)
    c128 = lax.broadcasted_iota(jnp.int32, (DK_C, DK_C), 1)
    out = []
    for (q, k, v, b, consts, reverse), intra in zip(items, intras):
        last = 0 if reverse else CHUNK - 1
        bl = b[last:last + 1, :]
        dec = jnp.sum(jnp.where(r128 == c128, jnp.broadcast_to(jnp.exp(bl), (DK_C, DK_C)), 0.0),
                      axis=-1, keepdims=True)
        out.append((intra, q * jnp.exp(b), dec, _dot_tn(k * jnp.exp(bl - b), v)))
    return out


def _gla_kernel(*refs, n, cg, has_s0):
    consts_f, consts_b = refs[8:11], refs[11:14]
    q_ref, k_ref, v_ref, z_ref, sm_ref, wg_ref, bg_ref, on_ref = refs[:8]
    if has_s0:
        s0_ref, o_ref, gk_s, acc_s, st_s = refs[14:]
        so_ref = None
    else:
        o_ref, so_ref, gk_s, acc_s, st_s = refs[14:]
    nc = n // CHUNK
    sm = sm_ref[...]
    for d in range(2):
        x = _dot_hi(sm, wg_ref[d]) + bg_ref[d]
        gk = (jnp.minimum(x, 0.0) - jnp.log(1.0 + jnp.exp(-jnp.abs(x)))) / GLA_TAU
        gk_s[d] = _chunk_cumsum(gk, bool(d))
    acc_s[...] = jnp.zeros((n, DV_C), F32)
    if has_s0:
        st_s[...] = s0_ref[...]
    else:
        st_s[...] = jnp.zeros((2, DK_C, DV_C), F32)

    def body(i, carry):
        where = [(d, (nc - 1 - (i * cg + t)) if d else (i * cg + t)) for d in range(2) for t in range(cg)]
        rows = [pl.ds(pl.multiple_of(c * CHUNK, CHUNK), CHUNK) for _, c in where]
        outs = _gla_prep([(q_ref[r, :] * DK_C ** -0.5, k_ref[r, :], v_ref[r, :], gk_s[d, r, :],
                           consts_b if d else consts_f, bool(d)) for (d, _), r in zip(where, rows)])
        for d in range(2):
            s = st_s[d]
            for t in range(cg):
                intra, qe, dec, kv = outs[d * cg + t]
                acc_s[rows[d * cg + t], :] += intra + _dot(qe, s)
                s = s * dec + kv
            st_s[d] = s
        return carry

    lax.fori_loop(0, nc // cg, body, 0)
    o = acc_s[...]
    o = o * lax.rsqrt(jnp.mean(o * o, axis=-1, keepdims=True) + EPS) * on_ref[...]
    o_ref[...] = o * _silu(z_ref[...])
    if so_ref is not None:
        so_ref[...] = st_s[...]


def _gla_mixer(p, small, w_gate, b_gate, onorm, s0, seq, nb):
    has_s0 = s0 is not None
    consts = _gla_consts(False) + _gla_consts(True)
    const_specs = [pl.BlockSpec(a.shape, (lambda b, h, nd=a.ndim: (0,) * nd)) for a in consts]
    in_specs = [
        pl.BlockSpec((seq, DK_C), lambda b, h: (b, h)),
        pl.BlockSpec((seq, DK_C), lambda b, h: (b, QK_C // DK_C + h)),
        pl.BlockSpec((seq, DV_C), lambda b, h: (b, 2 * QK_C // DV_C + h)),
        pl.BlockSpec((seq, DV_C), lambda b, h: (b, (2 * QK_C + W_C) // DV_C + h)),
        pl.BlockSpec((seq, LANE), lambda b, h: (b, 0)),
        pl.BlockSpec((2, LANE, DK_C), lambda b, h: (0, 0, h)),
        pl.BlockSpec((2, 1, DK_C), lambda b, h: (0, 0, h)),
        pl.BlockSpec((1, DV_C), lambda b, h: (0, 0)),
    ] + const_specs
    args = [p, p, p, p, small, w_gate, b_gate.reshape(2, 1, QK_C), onorm.reshape(1, DV_C)] + consts
    if has_s0:
        in_specs.append(pl.BlockSpec((None, 2, None, DK_C, DV_C), lambda b, h: (b, 0, h, 0, 0)))
        args.append(s0)
    out_specs = [pl.BlockSpec((seq, DV_C), lambda b, h: (b, h))]
    out_shape = [jax.ShapeDtypeStruct((nb * seq, W_C), F32)]
    if not has_s0:
        out_specs.append(pl.BlockSpec((None, 2, None, DK_C, DV_C), lambda b, h: (b, 0, h, 0, 0)))
        out_shape.append(jax.ShapeDtypeStruct((nb, 2, H_C, DK_C, DV_C), F32))
    return pl.pallas_call(
        functools.partial(_gla_kernel, n=seq, cg=GLA_CHUNKS, has_s0=has_s0),
        grid=(nb, H_C),
        in_specs=in_specs,
        out_specs=out_specs,
        out_shape=out_shape,
        scratch_shapes=[pltpu.VMEM((2, seq, DK_C), F32), pltpu.VMEM((seq, DV_C), F32),
                        pltpu.VMEM((2, DK_C, DV_C), F32)],
        compiler_params=_cparams(("arbitrary", "arbitrary")),
        name="gla_lat" if has_s0 else "gla_ctx",
    )(*args)


_EV_TILES = tuple(range(4 * W_A // TN_IN)) + (8, 9, 11, 12, 10)
_EV_ALIGNED = 4 * W_A // TN_IN
_EV_QB = 4 * W_A
_EV_ZB = _EV_QB + W_B
_EV_KB = _EV_ZB + W_B
_EV_VB = _EV_KB + HKV_B * HD
_OD_ALIGNED = (2 * QK_C + 2 * W_C) // TN_IN
_OD_TILES = tuple(range((P_ODD - 2 * GLA_RANK) // TN_IN))
_OD_QD = 2 * QK_C + 2 * W_C
_OD_KD = _OD_QD + W_D
_OD_VD = _OD_KD + W_D
_OD_ZD = _OD_VD + W_D


def _lane_row(v, offset):
    return jnp.pad(v.reshape(1, -1), ((0, 0), (offset, LANE - offset - v.size)))


def _even_layer(xc, xl, e, mod, norm_w, w_in, conv_a, a_log, dt_bias, onorm, sink, w_out, state_delta, cache_kv,
                final_w):
    pc, sc = _in_proj(xc, norm_w, mod, w_in, e, _EV_TILES, _EV_ALIGNED, 4 * W_A, False)
    pq, sq = _in_proj(xl, norm_w, mod, w_in, e, _EV_TILES, _EV_ALIGNED, 4 * W_A, True)
    a_row = _lane_row(a_log, 2 * H_A)
    dt_row = _lane_row(dt_bias, 2 * H_A)
    oa_c, st = _delta_mixer(pc, sc, conv_a, a_row, dt_row, onorm, None, SEQ, BATCH, 4, 2)
    (oa_l,) = _delta_mixer(pq, sq, conv_a, a_row, dt_row, onorm, state_delta[:, e], DEC_SEQ, DEC_BATCH, 2, 4)
    wkv = HKV_B * HD
    ob_c, kv = _ctx_attn(pc, _EV_QB // W_B, _EV_KB // wkv, _EV_VB // wkv, _EV_ZB // W_B, H_B, HKV_B, sink)
    ck = cache_kv[:, e, 0].reshape(DEC_BATCH, PAST_LEN, wkv)
    cv = cache_kv[:, e, 1].reshape(DEC_BATCH, PAST_LEN, wkv)
    ob_l = _win_attn(pq, ck, cv, sink, _EV_QB // W_B, _EV_KB // wkv, _EV_VB // wkv, _EV_ZB // W_B)
    xc = _out_proj(oa_c, ob_c, w_out, e, xc, mod, False, final_w)
    xl = _out_proj(oa_l, ob_l, w_out, e, xl, mod, True, final_w)
    return xc, xl, st, kv.reshape(BATCH, 2, SEQ, HKV_B, HD)


def _odd_layer(xc, xl, o_i, mod, norm_w, w_in, w_glr, b_glr, onorm, rpb, w_out, state_gla, cache_kv, final_w):
    lo = 2 * QK_C + 2 * W_C
    pc, sc = _in_proj(xc, norm_w, mod, w_in, o_i, _OD_TILES, _OD_ALIGNED, lo, False)
    pq, sq = _in_proj(xl, norm_w, mod, w_in, o_i, _OD_TILES, _OD_ALIGNED, lo, True)
    w_gate = jnp.stack([jnp.pad(w_glr[0], ((0, LANE - GLA_RANK), (0, 0))),
                        jnp.pad(w_glr[1], ((GLA_RANK, LANE - 2 * GLA_RANK), (0, 0)))])
    oc_c, st = _gla_mixer(pc, sc, w_gate, b_glr, onorm, None, SEQ, BATCH)
    (oc_l,) = _gla_mixer(pq, sq, w_gate, b_glr, onorm, state_gla[:, o_i], DEC_SEQ, DEC_BATCH)
    od_c, kv = _ctx_attn(pc, _OD_QD // W_D, _OD_KD // W_D, _OD_VD // W_D, _OD_ZD // W_D, H_D, H_D, None)
    ck = cache_kv[:, o_i, 0].reshape(DEC_BATCH, PAST_LEN, W_D)
    cv = cache_kv[:, o_i, 1].reshape(DEC_BATCH, PAST_LEN, W_D)
    od_l = _nbr_attn(pq, ck, cv, _nbr_bias_table(rpb), _OD_QD // W_D, _OD_KD // W_D, _OD_VD // W_D, _OD_ZD // W_D)
    xc = _out_proj(oc_c, od_c, w_out, o_i, xc, mod, False, final_w)
    xl = _out_proj(oc_l, od_l, w_out, o_i, xl, mod, True, final_w)
    return xc, xl, st, kv.reshape(BATCH, 2, SEQ, H_D, HD)


def kernel(x_prompt, x_sample, state_delta, cache_kv_win, state_gla, cache_kv_nbr, c, c_ctx, norm_w, w_ada, b_ada, w_in_even, conv_a, a_log_a, dt_bias_a, onorm_a, sink_b, w_out_even, w_in_odd, w_glr_c, b_glr_c, onorm_c, rpb_d, w_out_odd, final_norm_w):
    xc = x_prompt.reshape(N_CTX, D_MODEL)
    xl = x_sample.reshape(N_LAT, D_MODEL)
    cond = jnp.concatenate([c_ctx[None, :], c, jnp.zeros((N_COND - 1 - DEC_BATCH, D_MODEL), F32)], axis=0)
    mods = _ada_mod(cond, w_ada, b_ada).reshape(DEPTH, N_COND, 1, 3 * D_MODEL)
    wo_even, wo_odd = w_out_even.astype(BF16), w_out_odd.astype(BF16)
    new_delta, new_kvw, new_gla, new_kvn = [], [], [], []
    for li in range(DEPTH):
        final_w = final_norm_w if li == DEPTH - 1 else None
        if li % 2 == 0:
            e = li // 2
            xc, xl, st, kv = _even_layer(xc, xl, e, mods[li], norm_w[li], w_in_even, conv_a[e], a_log_a[e],
                                         dt_bias_a[e], onorm_a[e], sink_b[e], wo_even, state_delta,
                                         cache_kv_win, final_w)
            new_delta.append(st)
            new_kvw.append(kv)
        else:
            o_i = li // 2
            xc, xl, st, kv = _odd_layer(xc, xl, o_i, mods[li], norm_w[li], w_in_odd, w_glr_c[o_i],
                                        b_glr_c[o_i], onorm_c[o_i], rpb_d[o_i], wo_odd, state_gla,
                                        cache_kv_nbr, final_w)
            new_gla.append(st)
            new_kvn.append(kv)
    return (xc.reshape(BATCH, SEQ, D_MODEL), xl.reshape(DEC_BATCH, DEC_SEQ, D_MODEL),
            jnp.stack(new_delta, 1), jnp.stack(new_kvw, 1), jnp.stack(new_gla, 1), jnp.stack(new_kvn, 1))
```

```python
import functools

import numpy as np
import jax
import jax.numpy as jnp
from jax import lax
from jax.experimental import pallas as pl
from jax.experimental.pallas import tpu as pltpu

F32 = jnp.float32
BF16 = jnp.bfloat16
HIGHEST = lax.Precision.HIGHEST

D_MODEL = 2048
BATCH = 16
SEQ = 256
DEPTH = 4
DEC_BATCH = 4
DEC_SEQ = 1024
PAST_LEN = 256
GRID_W = 64
HD = 128
EPS = 1e-6
NEG = -1e30
ROPE_THETA = 10000.0
CHUNK = 64
H_A = 8
W_A = H_A * HD
CONV_K = 5
H_B = 8
HKV_B = 2
W_B = H_B * HD
WIN = 128
QBLK = 128
H_C = 4
DK_C = 128
DV_C = 256
QK_C = H_C * DK_C
W_C = H_C * DV_C
GLA_RANK = 16
GLA_TAU = 16.0
H_D = 8
W_D = H_D * HD
NB_H = 8
NB_W = 16
N_EVEN = (DEPTH + 1) // 2
N_ODD = DEPTH // 2
PA_EVEN = 4 * W_A + 4 * H_A
P_EVEN = PA_EVEN + 2 * W_B + 2 * HKV_B * HD
PC_ODD = 2 * QK_C + 2 * W_C + 2 * GLA_RANK
P_ODD = PC_ODD + 4 * W_D

N_CTX = BATCH * SEQ
N_LAT = DEC_BATCH * DEC_SEQ
N_GRP = N_CTX
assert N_LAT == N_GRP
N_COND = 8
LANE = 128
TM_IN = 1024
TN_IN = 512
GATE_COLS = 32
TM_OUT = 512
TN_ADA = 2048
VMEM_LIMIT = 56 * 1024 * 1024


def _cparams(sem):
    return pltpu.CompilerParams(dimension_semantics=sem, vmem_limit_bytes=VMEM_LIMIT)


def _sigmoid(x):
    return 1.0 / (1.0 + jnp.exp(-x))


def _silu(x):
    return x * _sigmoid(x)


def _dot(a, b):
    return jnp.dot(a.astype(BF16), b.astype(BF16), preferred_element_type=F32)


def _dot_nt(a, b):
    return lax.dot_general(a.astype(BF16), b.astype(BF16), (((1,), (1,)), ((), ())),
                           preferred_element_type=F32)


def _dot_tn(a, b):
    return lax.dot_general(a.astype(BF16), b.astype(BF16), (((0,), (0,)), ((), ())),
                           preferred_element_type=F32)


def _dot_hi(a, b):
    return jnp.dot(a, b, precision=HIGHEST, preferred_element_type=F32)


def _ada_kernel(c_ref, w_ref, b_ref, o_ref):
    o_ref[...] = _dot(_silu(c_ref[...]), w_ref[...]) + b_ref[...]


def _ada_mod(cond, w_ada, b_ada):
    n3 = 3 * D_MODEL
    return pl.pallas_call(
        _ada_kernel,
        grid=(DEPTH, n3 // TN_ADA),
        in_specs=[
            pl.BlockSpec((N_COND, D_MODEL), lambda l, j: (0, 0)),
            pl.BlockSpec((None, D_MODEL, TN_ADA), lambda l, j: (l, 0, j)),
            pl.BlockSpec((None, 1, TN_ADA), lambda l, j: (l, 0, j)),
        ],
        out_specs=pl.BlockSpec((None, N_COND, TN_ADA), lambda l, j: (l, 0, j)),
        out_shape=jax.ShapeDtypeStruct((DEPTH, N_COND, n3), F32),
        compiler_params=_cparams(("arbitrary", "arbitrary")),
        name="ada_mod",
    )(cond, w_ada, b_ada.reshape(DEPTH, 1, n3))


def _cond_row(i, tm, latent):
    return 1 + i // (DEC_SEQ // tm) if latent else 0


def _inproj_kernel(src_ref, x_ref, nw_ref, shift_ref, scale_ref, wa_ref, wb_ref, ws_ref, o_ref, os_ref,
                   h_ref, w_ref, *, n_aligned):
    j, i = pl.program_id(0), pl.program_id(1)
    rows = pl.ds(pl.multiple_of(i * TM_IN, TM_IN), TM_IN)

    @pl.when(j == 0)
    def _():
        x = x_ref[...]
        y = x * lax.rsqrt(jnp.mean(x * x, axis=-1, keepdims=True) + EPS) * nw_ref[...]
        h = (y * (1.0 + scale_ref[...]) + shift_ref[...]).astype(BF16)
        h_ref[rows, :] = h
        os_ref[...] = _dot_nt(h, ws_ref[...])

    @pl.when((i == 0) & (j < n_aligned))
    def _():
        w_ref[...] = wa_ref[...].astype(BF16)

    @pl.when((i == 0) & (j >= n_aligned))
    def _():
        w_ref[0:TN_IN - GATE_COLS, :] = wa_ref[GATE_COLS:TN_IN, :].astype(BF16)
        w_ref[TN_IN - GATE_COLS:TN_IN, :] = wb_ref[0:GATE_COLS, :].astype(BF16)

    o_ref[...] = _dot_nt(h_ref[rows, :], w_ref[...])


def _in_proj(x, norm_w, mod, w_t, layer, src_tiles, n_aligned, gate_col, latent):
    nj, ni = len(src_tiles), N_GRP // TM_IN
    sub = TN_IN // LANE
    row = functools.partial(_cond_row, tm=TM_IN, latent=latent)
    tok = lambda j, i, s: (jnp.where(j == 0, i, ni - 1), 0)
    grid_spec = pltpu.PrefetchScalarGridSpec(
        num_scalar_prefetch=1,
        grid=(nj, ni),
        in_specs=[
            pl.BlockSpec((TM_IN, D_MODEL), tok),
            pl.BlockSpec((1, D_MODEL), lambda j, i, s: (0, 0)),
            pl.BlockSpec((None, 1, D_MODEL), lambda j, i, s: (row(jnp.where(j == 0, i, ni - 1)), 0, 0)),
            pl.BlockSpec((None, 1, D_MODEL), lambda j, i, s: (row(jnp.where(j == 0, i, ni - 1)), 0, 1)),
            pl.BlockSpec((None, TN_IN, D_MODEL), lambda j, i, s: (layer, s[j], 0)),
            pl.BlockSpec((None, LANE, D_MODEL), lambda j, i, s: (layer, (s[j] + 1) * sub, 0)),
            pl.BlockSpec((None, LANE, D_MODEL), lambda j, i, s: (layer, gate_col // LANE, 0)),
        ],
        out_specs=[
            pl.BlockSpec((TM_IN, TN_IN), lambda j, i, s: (i, j)),
            pl.BlockSpec((TM_IN, LANE), tok),
        ],
        scratch_shapes=[pltpu.VMEM((N_GRP, D_MODEL), BF16), pltpu.VMEM((TN_IN, D_MODEL), BF16)],
    )
    return pl.pallas_call(
        functools.partial(_inproj_kernel, n_aligned=n_aligned),
        grid_spec=grid_spec,
        out_shape=[jax.ShapeDtypeStruct((N_GRP, nj * TN_IN), F32),
                   jax.ShapeDtypeStruct((N_GRP, LANE), F32)],
        compiler_params=_cparams(("arbitrary", "arbitrary")),
        name="in_proj_lat" if latent else "in_proj_ctx",
    )(jnp.asarray(src_tiles, jnp.int32), x, norm_w.reshape(1, D_MODEL), mod, mod, w_t, w_t, w_t)


def _outproj_kernel(oa_ref, ob_ref, wa_ref, wb_ref, x_ref, g_ref, *rest, final):
    acc = _dot(oa_ref[...], wa_ref[...]) + _dot(ob_ref[...], wb_ref[...])
    xn = x_ref[...] + g_ref[...] * acc
    if final:
        fw_ref, y_ref = rest
        y_ref[...] = xn * lax.rsqrt(jnp.mean(xn * xn, axis=-1, keepdims=True) + EPS) * fw_ref[...]
    else:
        (y_ref,) = rest
        y_ref[...] = xn


def _out_proj(o_a, o_b, w_out, layer, x, mod, latent, final_w=None):
    ka, kb = o_a.shape[1], o_b.shape[1]
    assert ka == kb and w_out.shape[1] == ka + kb
    row = functools.partial(_cond_row, tm=TM_OUT, latent=latent)
    final = final_w is not None
    in_specs = [
        pl.BlockSpec((TM_OUT, ka), lambda i: (i, 0)),
        pl.BlockSpec((TM_OUT, kb), lambda i: (i, 0)),
        pl.BlockSpec((None, ka, D_MODEL), lambda i: (layer, 0, 0)),
        pl.BlockSpec((None, kb, D_MODEL), lambda i: (layer, 1, 0)),
        pl.BlockSpec((TM_OUT, D_MODEL), lambda i: (i, 0)),
        pl.BlockSpec((None, 1, D_MODEL), lambda i: (row(i), 0, 2)),
    ]
    args = [o_a, o_b, w_out, w_out, x, mod]
    if final:
        in_specs.append(pl.BlockSpec((1, D_MODEL), lambda i: (0, 0)))
        args.append(final_w.reshape(1, D_MODEL))
    return pl.pallas_call(
        functools.partial(_outproj_kernel, final=final),
        grid=(N_GRP // TM_OUT,),
        in_specs=in_specs,
        out_specs=pl.BlockSpec((TM_OUT, D_MODEL), lambda i: (i, 0)),
        out_shape=jax.ShapeDtypeStruct((N_GRP, D_MODEL), F32),
        compiler_params=_cparams(("arbitrary",)),
        name="out_proj_final" if final else "out_proj",
    )(*args)


def _attend(problems, scale):
    scores = [[_dot_nt(q, k) for k in ks] for q, ks, _, _, _, _ in problems]
    outs = []
    parts = []
    for (q, ks, vs, masks, biases, sink), raw in zip(problems, scores):
        ss = []
        for s, m, bias in zip(raw, masks, biases):
            s = s * scale
            if bias is not None:
                s = s + bias
            if m is not None:
                s = jnp.where(m, s, NEG)
            ss.append(s)
        mx = functools.reduce(jnp.maximum, [jnp.max(s, axis=-1, keepdims=True) for s in ss])
        if sink is not None:
            mx = jnp.maximum(mx, sink)
        es = [jnp.exp(s - mx) for s in ss]
        den = functools.reduce(jnp.add, [jnp.sum(e, axis=-1, keepdims=True) for e in es])
        if sink is not None:
            den = den + jnp.exp(sink - mx)
        parts.append((es, den))
    pvs = [[_dot(e, v) for e, v in zip(es, vs)] for (es, _), (_, _, vs, _, _, _) in zip(parts, problems)]
    for pv, (_, den) in zip(pvs, parts):
        outs.append(functools.reduce(jnp.add, pv) / den)
    return outs


def _head(ref, h, rows=None):
    if rows is None:
        return ref[:, h * HD:(h + 1) * HD]
    return ref[rows, h * HD:(h + 1) * HD]


def _ctx_attn_kernel(*refs, heads, kv_heads, use_sink):
    if use_sink:
        q_ref, k_ref, v_ref, z_ref, sink_ref, o_ref, kv_ref = refs
    else:
        q_ref, k_ref, v_ref, z_ref, o_ref, kv_ref = refs
    g = heads // kv_heads
    n = q_ref.shape[0]
    problems = []
    for j in range(kv_heads):
        q = jnp.concatenate([_head(q_ref, j * g + t) for t in range(g)], axis=0)
        sink = None
        if use_sink:
            sink = jnp.concatenate([jnp.full((n, 1), sink_ref[j * g + t], F32) for t in range(g)], axis=0)
        problems.append((q, [_head(k_ref, j)], [_head(v_ref, j)], [None], [None], sink))
    outs = _attend(problems, HD ** -0.5)
    for j, o in enumerate(outs):
        for t in range(g):
            h = j * g + t
            o_ref[:, h * HD:(h + 1) * HD] = o[t * n:(t + 1) * n] * _silu(_head(z_ref, h))
    kv_ref[0] = k_ref[...]
    kv_ref[1] = v_ref[...]


def _ctx_attn(p, q_col, k_col, v_col, z_col, heads, kv_heads, sink):
    wq, wkv = heads * HD, kv_heads * HD
    use_sink = sink is not None
    in_specs = [
        pl.BlockSpec((SEQ, wq), lambda b: (b, q_col)),
        pl.BlockSpec((SEQ, wkv), lambda b: (b, k_col)),
        pl.BlockSpec((SEQ, wkv), lambda b: (b, v_col)),
        pl.BlockSpec((SEQ, wq), lambda b: (b, z_col)),
    ]
    args = [p, p, p, p]
    if use_sink:
        in_specs.append(pl.BlockSpec(memory_space=pltpu.SMEM))
        args.append(sink)
    return pl.pallas_call(
        functools.partial(_ctx_attn_kernel, heads=heads, kv_heads=kv_heads, use_sink=use_sink),
        grid=(BATCH,),
        in_specs=in_specs,
        out_specs=[pl.BlockSpec((SEQ, wq), lambda b: (b, 0)),
                   pl.BlockSpec((None, 2, SEQ, wkv), lambda b: (b, 0, 0, 0))],
        out_shape=[jax.ShapeDtypeStruct((N_CTX, wq), F32),
                   jax.ShapeDtypeStruct((BATCH, 2, SEQ, wkv), F32)],
        compiler_params=_cparams(("arbitrary",)),
        name="ctx_attn_sink" if use_sink else "ctx_attn",
    )(*args)


def _rope_tables():
    half = HD // 4
    freq = (ROPE_THETA ** (-np.arange(half, dtype=np.float32) / half)).astype(np.float32)
    t = np.arange(DEC_SEQ)
    ang_r = (t // GRID_W).astype(np.float32)[:, None] * freq[None, :]
    ang_c = (t % GRID_W).astype(np.float32)[:, None] * freq[None, :]
    cos = np.concatenate([np.cos(ang_r)] * 2 + [np.cos(ang_c)] * 2, axis=1).astype(np.float32)
    sin_r, sin_c, zero = np.sin(ang_r), np.sin(ang_c), np.zeros_like(ang_r)
    s_up = np.concatenate([-sin_r, zero, -sin_c, zero], axis=1).astype(np.float32)
    s_dn = np.concatenate([zero, sin_r, zero, sin_c], axis=1).astype(np.float32)
    return jnp.asarray(cos), jnp.asarray(s_up), jnp.asarray(s_dn)


def _rope(x, cos, s_up, s_dn):
    return x * cos + pltpu.roll(x, HD - HD // 4, 1) * s_up + pltpu.roll(x, HD // 4, 1) * s_dn


def _win_attn_kernel(q_ref, k_ref, v_ref, kc_ref, vc_ref, z_ref, cos_ref, sup_ref, sdn_ref, sink_ref, o_ref):
    i = pl.program_id(1)
    g = H_B // HKV_B
    span = QBLK + 2 * WIN
    start = pl.multiple_of(jnp.clip(i * QBLK - WIN, 0, DEC_SEQ - span), QBLK)
    qrows = pl.ds(pl.multiple_of(i * QBLK, QBLK), QBLK)
    krows = pl.ds(start, span)
    cq, uq, dq = cos_ref[qrows, :], sup_ref[qrows, :], sdn_ref[qrows, :]
    ck, uk, dk = cos_ref[krows, :], sup_ref[krows, :], sdn_ref[krows, :]
    qpos = i * QBLK + lax.broadcasted_iota(jnp.int32, (g * QBLK, span), 0) % QBLK
    kpos = start + lax.broadcasted_iota(jnp.int32, (g * QBLK, span), 1)
    band = jnp.abs(qpos - kpos) <= WIN
    problems = []
    for j in range(HKV_B):
        q = jnp.concatenate([_rope(_head(q_ref, j * g + t), cq, uq, dq) for t in range(g)], axis=0)
        kw = _rope(_head(k_ref, j, krows), ck, uk, dk)
        vw = _head(v_ref, j, krows)
        sink = jnp.concatenate([jnp.full((QBLK, 1), sink_ref[j * g + t], F32) for t in range(g)], axis=0)
        problems.append((q, [kw, _head(kc_ref, j)], [vw, _head(vc_ref, j)], [band, None], [None, None], sink))
    outs = _attend(problems, HD ** -0.5)
    for j, o in enumerate(outs):
        for t in range(g):
            h = j * g + t
            o_ref[:, h * HD:(h + 1) * HD] = o[t * QBLK:(t + 1) * QBLK] * _silu(_head(z_ref, h))


def _win_attn(p, cache_k, cache_v, sink, q_col, k_col, v_col, z_col):
    wq, wkv = H_B * HD, HKV_B * HD
    nq = DEC_SEQ // QBLK
    cos, s_up, s_dn = _rope_tables()
    full = pl.BlockSpec((DEC_SEQ, HD), lambda b, i: (0, 0))
    return pl.pallas_call(
        _win_attn_kernel,
        grid=(DEC_BATCH, nq),
        in_specs=[
            pl.BlockSpec((QBLK, wq), lambda b, i: (b * nq + i, q_col)),
            pl.BlockSpec((DEC_SEQ, wkv), lambda b, i: (b, k_col)),
            pl.BlockSpec((DEC_SEQ, wkv), lambda b, i: (b, v_col)),
            pl.BlockSpec((None, PAST_LEN, wkv), lambda b, i: (b, 0, 0)),
            pl.BlockSpec((None, PAST_LEN, wkv), lambda b, i: (b, 0, 0)),
            pl.BlockSpec((QBLK, wq), lambda b, i: (b * nq + i, z_col)),
            full, full, full,
            pl.BlockSpec(memory_space=pltpu.SMEM),
        ],
        out_specs=pl.BlockSpec((QBLK, wq), lambda b, i: (b * nq + i, 0)),
        out_shape=jax.ShapeDtypeStruct((N_LAT, wq), F32),
        compiler_params=_cparams(("arbitrary", "arbitrary")),
        name="win_attn",
    )(p, p, p, cache_k, cache_v, p, cos, s_up, s_dn, sink)


def _nbr_onehot():
    qc = np.arange(GRID_W)[:, None]
    kc = np.arange(GRID_W)[None, :]
    idx = np.clip(kc - qc, -(NB_W - 1), NB_W - 1) + NB_W - 1
    e = (np.arange(2 * NB_W)[:, None, None] == idx[None]).astype(np.float32)
    return jnp.asarray(e.reshape(2 * NB_W, GRID_W * GRID_W))


def _bias_expand_kernel(r_ref, e_ref, o_ref):
    o_ref[...] = _dot_hi(r_ref[...], e_ref[...])


def _nbr_bias_table(rpb):
    rows = H_D * (2 * NB_H - 1)
    r = jnp.pad(rpb.reshape(rows, 2 * NB_W - 1), ((0, 128 - rows), (0, 1)))
    t = pl.pallas_call(
        _bias_expand_kernel,
        out_shape=jax.ShapeDtypeStruct((128, GRID_W * GRID_W), F32),
        name="nbr_bias_expand",
    )(r, _nbr_onehot())
    t = t[:rows].reshape(H_D, 2 * NB_H - 1, GRID_W, GRID_W)
    return jnp.concatenate([t[:, :-1], t[:, 1:]], axis=-1)


def _nbr_attn_kernel(q_ref, k_ref, v_ref, kc_ref, vc_ref, z_ref, t_ref, o_ref):
    r = pl.program_id(1)
    rows = DEC_SEQ // GRID_W
    rs = jnp.clip(r - NB_H // 2, 0, rows - NB_H)
    dr0 = rs - r + NB_H - 1
    nk = NB_H * GRID_W
    krows = pl.ds(pl.multiple_of(rs * GRID_W, GRID_W), nk)
    qc = lax.broadcasted_iota(jnp.int32, (GRID_W, nk), 0)
    kc = lax.broadcasted_iota(jnp.int32, (GRID_W, nk), 1) % GRID_W
    cstart = jnp.clip(qc - NB_W // 2, 0, GRID_W - NB_W)
    ok = (kc >= cstart) & (kc < cstart + NB_W)
    problems = []
    for h in range(H_D):
        bias = jnp.concatenate([t_ref[h, dr0 + 2 * m] for m in range(nk // LANE)], axis=1)
        problems.append((_head(q_ref, h), [_head(k_ref, h, krows), _head(kc_ref, h)],
                         [_head(v_ref, h, krows), _head(vc_ref, h)], [ok, None], [bias, None], None))
    outs = _attend(problems, HD ** -0.5)
    for h, o in enumerate(outs):
        o_ref[:, h * HD:(h + 1) * HD] = o * _silu(_head(z_ref, h))


def _nbr_attn(p, cache_k, cache_v, table, q_col, k_col, v_col, z_col):
    rows = DEC_SEQ // GRID_W
    return pl.pallas_call(
        _nbr_attn_kernel,
        grid=(DEC_BATCH, rows),
        in_specs=[
            pl.BlockSpec((GRID_W, W_D), lambda b, r: (b * rows + r, q_col)),
            pl.BlockSpec((DEC_SEQ, W_D), lambda b, r: (b, k_col)),
            pl.BlockSpec((DEC_SEQ, W_D), lambda b, r: (b, v_col)),
            pl.BlockSpec((None, PAST_LEN, W_D), lambda b, r: (b, 0, 0)),
            pl.BlockSpec((None, PAST_LEN, W_D), lambda b, r: (b, 0, 0)),
            pl.BlockSpec((GRID_W, W_D), lambda b, r: (b * rows + r, z_col)),
            pl.BlockSpec(table.shape, lambda b, r: (0, 0, 0, 0)),
        ],
        out_specs=pl.BlockSpec((GRID_W, W_D), lambda b, r: (b * rows + r, 0)),
        out_shape=jax.ShapeDtypeStruct((N_LAT, W_D), F32),
        compiler_params=_cparams(("arbitrary", "arbitrary")),
        name="nbr_attn",
    )(p, p, p, cache_k, cache_v, p, table)


def _lane_col(x, idx):
    lane = lax.broadcasted_iota(jnp.int32, x.shape, 1)
    return jnp.sum(jnp.where(lane == idx, x, 0.0), axis=-1, keepdims=True)


def _short_conv(x_ref, w_ref, pad_ref, n):
    pad = CONV_K // 2
    zeros = jnp.zeros((8, HD), F32)
    pad_ref[0:8, :] = zeros
    pad_ref[n + 8:n + 16, :] = zeros
    pad_ref[8:n + 8, :] = x_ref[...]
    y = functools.reduce(jnp.add, [pad_ref[8 - pad + t:8 - pad + t + n, :] * w_ref[t:t + 1, :]
                                   for t in range(CONV_K)])
    return _silu(y)


def _l2norm(x):
    return x * lax.rsqrt(jnp.sum(x * x, axis=-1, keepdims=True) + EPS)


def _split2(x):
    hi = x.astype(BF16)
    return hi, (x - hi.astype(F32)).astype(BF16)


def _dots_x2(pairs, split_b=True):
    ops = []
    for a, b in pairs:
        a_hi, a_lo = _split2(a)
        b2 = jnp.concatenate(_split2(b), axis=1) if split_b else b.astype(BF16)
        ops.append((jnp.concatenate([a_hi, a_lo], axis=0), b2))
    rs = [jnp.dot(a2, b2, preferred_element_type=F32) for a2, b2 in ops]
    out = []
    for (a, b), r in zip(pairs, rs):
        m, n = a.shape[0], b.shape[1]
        if split_b:
            out.append((r[m:, :n] + r[:m, n:] + r[m:, n:]) + r[:m, :n])
        else:
            out.append(r[m:] + r[:m])
    return out


def _pair_masks():
    c = lax.broadcasted_iota(jnp.int32, (CHUNK, HD), 0)
    l = lax.broadcasted_iota(jnp.int32, (CHUNK, HD), 1)
    left = l < CHUNK
    j = l % CHUNK
    ahead = jnp.where(left, j - c, c - j)
    return left, ahead <= 0, ahead < 0, j == c


def _block_diag(x, left):
    return jnp.concatenate([jnp.where(left, x, 0.0), jnp.where(left, 0.0, x)], axis=0)


def _inv_unit_triangular_pairs(lmats, left, eye):
    mps = [-x for x in lmats]
    ps = [jnp.where(eye, 1.0, 0.0) + m for m in mps]
    mps = _dots_x2([(m, _block_diag(m, left)) for m in mps])
    for _ in range(4):
        rs = _dots_x2([(jnp.concatenate([p, m], axis=0), _block_diag(m, left)) for p, m in zip(ps, mps)])
        ps = [p + r[:CHUNK] for p, r in zip(ps, rs)]
        mps = [r[CHUNK:] for r in rs]
    rs = _dots_x2([(p, _block_diag(m, left)) for p, m in zip(ps, mps)])
    return [p + r for p, r in zip(ps, rs)]


def _chunk_cumsum(x, reverse):
    n = x.shape[0]
    pos = lax.broadcasted_iota(jnp.int32, x.shape, 0) % CHUNK
    k = 1
    while k < CHUNK:
        if reverse:
            x = x + jnp.where(pos < CHUNK - k, pltpu.roll(x, n - k, 0), 0.0)
        else:
            x = x + jnp.where(pos >= k, pltpu.roll(x, k, 0), 0.0)
        k *= 2
    return x


def _delta_prep(items):
    left, tri, strict, eye = _pair_masks()
    zeros = jnp.zeros((CHUNK, HD), F32)
    pre = []
    for q, k, v, cols in items:
        b_f, b_b, g_f, g_b = (cols[:, t:t + 1] for t in range(4))
        gsel = jnp.where(left, g_f, g_b)
        g_row = jnp.sum(jnp.where(eye, gsel, 0.0), axis=0, keepdims=True)
        decay = jnp.exp(jnp.where(tri, gsel - g_row, NEG))
        kb_f, kb_b = k * b_f, k * b_b
        lhs = jnp.concatenate([jnp.concatenate([kb_f, kb_b], axis=1), jnp.concatenate([q, q], axis=1)], axis=0)
        rhs = jnp.concatenate([jnp.concatenate([k, zeros], axis=1), jnp.concatenate([zeros, k], axis=1)], axis=0)
        pre.append((decay, kb_f, kb_b, g_f, g_b, b_f, b_b, lhs, rhs))
    kqs = [_dot_nt(x[7], x[8]) for x in pre]
    lmats = [jnp.where(strict, kq[:CHUNK] * x[0], 0.0) for kq, x in zip(kqs, pre)]
    attns = [jnp.where(tri, kq[CHUNK:] * x[0], 0.0) for kq, x in zip(kqs, pre)]
    tinvs = _inv_unit_triangular_pairs(lmats, left, eye)
    egs, rhs = [], []
    for (q, k, v, cols), x in zip(items, pre):
        _, kb_f, kb_b, g_f, g_b, b_f, b_b = x[:7]
        eg_f = jnp.exp(jnp.broadcast_to(g_f, (CHUNK, HD)))
        eg_b = jnp.exp(jnp.broadcast_to(g_b, (CHUNK, HD)))
        egs.append((eg_f, eg_b))
        rhs.append(jnp.concatenate([jnp.concatenate([v * b_f, kb_f * eg_f, zeros, zeros], axis=1),
                                    jnp.concatenate([zeros, zeros, v * b_b, kb_b * eg_b], axis=1)], axis=0))
    uws = _dots_x2(list(zip(tinvs, rhs)), split_b=False)
    out = []
    for (q, k, v, cols), x, attn, uw, (eg_f, eg_b) in zip(items, pre, attns, uws, egs):
        g_f, g_b = x[3], x[4]
        gl_f = jnp.broadcast_to(g_f[CHUNK - 1:CHUNK, :], (1, HD))
        gl_b = jnp.broadcast_to(g_b[0:1, :], (1, HD))
        kg_t = jnp.concatenate([k * jnp.exp(gl_f - g_f), k * jnp.exp(gl_b - g_b)], axis=1).T
        out.append(dict(u=[uw[:, :HD], uw[:, 2 * HD:3 * HD]], w=[uw[:, HD:2 * HD], uw[:, 3 * HD:]],
                        attn=[attn[:, :CHUNK], attn[:, CHUNK:]], qg=[q * eg_f, q * eg_b],
                        kg_t=[kg_t[:HD], kg_t[HD:]], eg=[jnp.exp(gl_f), jnp.exp(gl_b)]))
    return out


def _delta_kernel(*refs, n, hb, cg, has_s0):
    qp_ref, kp_ref, vp_ref, z_ref, sm_ref, cq_ref, ck_ref, cv_ref, al_ref, dt_ref, on_ref = refs[:11]
    if has_s0:
        s0_ref, o_ref = refs[11:13]
        so_ref = None
    else:
        o_ref, so_ref = refs[11:13]
    q_s, k_s, v_s, pad_s, col_s, u_s, w_s, at_s, qg_s, kg_s, eg_s, acc_s, st_s, gate_s = refs[13:]
    h0 = pl.program_id(1) * hb
    nc = n // CHUNK

    @pl.when(pl.program_id(1) == 0)
    def _():
        sm = sm_ref[...]
        x = sm + dt_ref[...]
        softplus = jnp.maximum(x, 0.0) + jnp.log(1.0 + jnp.exp(-jnp.abs(x)))
        gates = -jnp.exp(al_ref[...]) * softplus
        gate_s[0] = _sigmoid(sm)
        gate_s[1] = _chunk_cumsum(gates, False)
        gate_s[2] = _chunk_cumsum(gates, True)

    betas, cum_f, cum_b = gate_s[0], gate_s[1], gate_s[2]
    lane = lax.broadcasted_iota(jnp.int32, (n, HD), 1)
    for j in range(hb):
        cols = slice(j * HD, (j + 1) * HD)
        q_s[j] = _l2norm(_short_conv(qp_ref.at[:, cols], cq_ref.at[:, cols], pad_s, n)) * HD ** -0.5
        k_s[j] = _l2norm(_short_conv(kp_ref.at[:, cols], ck_ref.at[:, cols], pad_s, n))
        v_s[j] = _short_conv(vp_ref.at[:, cols], cv_ref.at[:, cols], pad_s, n)
        col_s[j] = jnp.where(lane == 0, _lane_col(betas, h0 + j),
                             jnp.where(lane == 1, _lane_col(betas, H_A + h0 + j),
                                       jnp.where(lane == 2, _lane_col(cum_f, 2 * H_A + h0 + j),
                                                 _lane_col(cum_b, 3 * H_A + h0 + j))))
    acc_s[...] = jnp.zeros((hb, n, HD), F32)
    if has_s0:
        st_s[...] = s0_ref[...]
    else:
        st_s[...] = jnp.zeros((2, hb, HD, HD), F32)

    def prep(ci, carry):
        where = [(j, ci * cg + t) for j in range(hb) for t in range(cg)]
        rows = [pl.ds(pl.multiple_of(c * CHUNK, CHUNK), CHUNK) for _, c in where]
        outs = _delta_prep([(q_s[j, r, :], k_s[j, r, :], v_s[j, r, :], col_s[j, r, :])
                            for (j, _), r in zip(where, rows)])
        for (j, c), r, o in zip(where, rows, outs):
            for d in range(2):
                u_s[d, j, r, :] = o["u"][d]
                w_s[d, j, r, :] = o["w"][d].astype(BF16)
                at_s[d, j, r, :] = o["attn"][d].astype(BF16)
                qg_s[d, j, r, :] = o["qg"][d].astype(BF16)
                kg_s[d, j, c] = o["kg_t"][d].astype(BF16)
                eg_s[d, j, c] = jnp.broadcast_to(o["eg"][d], (8, HD))
        return carry

    lax.fori_loop(0, nc // cg, prep, 0)

    def scan(i, carry):
        chains = [(d, j, (nc - 1 - i) if d else i) for d in range(2) for j in range(hb)]
        rows = [pl.ds(pl.multiple_of(c * CHUNK, CHUNK), CHUNK) for _, _, c in chains]
        ss = [st_s[d, j] for d, j, _ in chains]
        sbs = [s.astype(BF16) for s in ss]
        ws = [jnp.dot(w_s[d, j, r, :], sb, preferred_element_type=F32) for (d, j, _), r, sb in zip(chains, rows, sbs)]
        vns = [(u_s[d, j, r, :] - w).astype(BF16) for (d, j, _), r, w in zip(chains, rows, ws)]
        for (d, j, c), r, s, sb, vn in zip(chains, rows, ss, sbs, vns):
            st_s[d, j] = s * eg_s[d, j, c, 0:1, :] + jnp.dot(kg_s[d, j, c], vn, preferred_element_type=F32)
        for (d, j, c), r, sb, vn in zip(chains, rows, sbs, vns):
            acc_s[j, r, :] += (jnp.dot(qg_s[d, j, r, :], sb, preferred_element_type=F32)
                               + jnp.dot(at_s[d, j, r, :], vn, preferred_element_type=F32))
        return carry

    lax.fori_loop(0, nc, scan, 0)
    for j in range(hb):
        cols = slice(j * HD, (j + 1) * HD)
        o = acc_s[j]
        o = o * lax.rsqrt(jnp.mean(o * o, axis=-1, keepdims=True) + EPS) * on_ref[...]
        o_ref[:, cols] = o * _silu(z_ref[:, cols])
    if so_ref is not None:
        so_ref[...] = st_s[...]


def _delta_mixer(p, small, conv_w, a_row, dt_row, onorm, s0, seq, nb, hb, cg):
    has_s0 = s0 is not None
    wb = hb * HD
    ng = H_A // hb
    nc = seq // CHUNK
    col = lambda off: (lambda b, h: (b, off + h))
    in_specs = [
        pl.BlockSpec((seq, wb), col(0)),
        pl.BlockSpec((seq, wb), col(ng)),
        pl.BlockSpec((seq, wb), col(2 * ng)),
        pl.BlockSpec((seq, wb), col(3 * ng)),
        pl.BlockSpec((seq, LANE), lambda b, h: (b, 0)),
        pl.BlockSpec((CONV_K, wb), lambda b, h: (0, h)),
        pl.BlockSpec((CONV_K, wb), lambda b, h: (0, ng + h)),
        pl.BlockSpec((CONV_K, wb), lambda b, h: (0, 2 * ng + h)),
        pl.BlockSpec((1, LANE), lambda b, h: (0, 0)),
        pl.BlockSpec((1, LANE), lambda b, h: (0, 0)),
        pl.BlockSpec((1, HD), lambda b, h: (0, 0)),
    ]
    args = [p, p, p, p, small, conv_w, conv_w, conv_w, a_row, dt_row, onorm.reshape(1, HD)]
    if has_s0:
        in_specs.append(pl.BlockSpec((None, 2, hb, HD, HD), lambda b, h: (b, 0, h, 0, 0)))
        args.append(s0)
    out_specs = [pl.BlockSpec((seq, wb), lambda b, h: (b, h))]
    out_shape = [jax.ShapeDtypeStruct((nb * seq, W_A), F32)]
    if not has_s0:
        out_specs.append(pl.BlockSpec((None, 2, hb, HD, HD), lambda b, h: (b, 0, h, 0, 0)))
        out_shape.append(jax.ShapeDtypeStruct((nb, 2, H_A, HD, HD), F32))
    return pl.pallas_call(
        functools.partial(_delta_kernel, n=seq, hb=hb, cg=cg, has_s0=has_s0),
        grid=(nb, ng),
        in_specs=in_specs,
        out_specs=out_specs,
        out_shape=out_shape,
        scratch_shapes=[
            pltpu.VMEM((hb, seq, HD), F32), pltpu.VMEM((hb, seq, HD), F32), pltpu.VMEM((hb, seq, HD), F32),
            pltpu.VMEM((seq + 16, HD), F32), pltpu.VMEM((hb, seq, HD), F32),
            pltpu.VMEM((2, hb, seq, HD), F32), pltpu.VMEM((2, hb, seq, HD), BF16),
            pltpu.VMEM((2, hb, seq, CHUNK), BF16), pltpu.VMEM((2, hb, seq, HD), BF16),
            pltpu.VMEM((2, hb, nc, HD, CHUNK), BF16), pltpu.VMEM((2, hb, nc, 8, HD), F32),
            pltpu.VMEM((hb, seq, HD), F32), pltpu.VMEM((2, hb, HD, HD), F32),
            pltpu.VMEM((3, seq, LANE), F32)],
        compiler_params=_cparams(("arbitrary", "arbitrary")),
        name="delta_lat" if has_s0 else "delta_ctx",
    )(*args)


_GLA_LEVELS = (32, 16, 8, 4, 2, 1)
GLA_CHUNKS = 2


def _gla_consts(reverse):
    r = np.arange(CHUNK)
    flip = (lambda a: a[::-1, ::-1]) if reverse else (lambda a: a)
    sel, hi, pair = [], [], []
    for m in _GLA_LEVELS:
        mid = (r // (2 * m)) * (2 * m) + m
        is_hi = r >= mid
        sel.append(flip(r[None, :] == mid[:, None] - 1))
        hi.append(flip(is_hi[:, None]))
        pair.append(flip((r[:, None] // (2 * m) == r[None, :] // (2 * m)) & is_hi[:, None] & ~is_hi[None, :]))
    out = [np.concatenate(sel), np.concatenate(hi), np.stack(pair)]
    out = [jnp.asarray(np.ascontiguousarray(a).astype(np.float32)) for a in out]
    return [out[0].astype(BF16), out[1], out[2]]


def _split3(x):
    hi = x.astype(BF16)
    r1 = x - hi.astype(F32)
    mid = r1.astype(BF16)
    return hi, mid, (r1 - mid.astype(F32)).astype(BF16)


def _gla_prep(items):
    nl = len(_GLA_LEVELS)
    r3s = [jnp.dot(consts[0][...], jnp.concatenate(_split3(b), axis=1), preferred_element_type=F32)
           for _, _, _, b, consts, _ in items]
    refs = [(r3[:, 2 * DK_C:] + r3[:, DK_C:2 * DK_C]) + r3[:, :DK_C] for r3 in r3s]
    c = lax.broadcasted_iota(jnp.int32, (CHUNK, CHUNK), 0)
    j = lax.broadcasted_iota(jnp.int32, (CHUNK, CHUNK), 1)
    lvl = []
    for (q, k, v, b, consts, _), ref in zip(items, refs):
        ops = []
        for lv in range(nl):
            rows = slice(lv * CHUNK, (lv + 1) * CHUNK)
            hi = consts[1][rows, :] > 0.5
            t = b - ref[rows]
            e = jnp.exp(jnp.where(hi, t, -t))
            ops.append((jnp.where(hi, q * e, 0.0), jnp.where(hi, 0.0, k * e)))
        lvl.append(ops)
    prods = [[_dot_nt(ql, kl) for ql, kl in ops] for ops in lvl]
    amats = []
    for (q, k, v, b, consts, _), pr in zip(items, prods):
        a = jnp.where(c == j, jnp.sum(q * k, axis=-1, keepdims=True), 0.0)
        for lv in range(nl):
            a = a + pr[lv] * consts[2][lv]
        amats.append(a)
    intras = [_dot(a, it[2]) for a, it in zip(amats, items)]
    r128 = lax.broadcasted_iota(jnp.int32, (DK_C, DK_C), 0)
    c128 = lax.broadcasted_iota(jnp.int32, (DK_C, DK_C), 1)
    out = []
    for (q, k, v, b, consts, reverse), intra in zip(items, intras):
        last = 0 if reverse else CHUNK - 1
        bl = b[last:last + 1, :]
        dec = jnp.sum(jnp.where(r128 == c128, jnp.broadcast_to(jnp.exp(bl), (DK_C, DK_C)), 0.0),
                      axis=-1, keepdims=True)
        out.append((intra, q * jnp.exp(b), dec, _dot_tn(k * jnp.exp(bl - b), v)))
    return out


def _gla_kernel(*refs, n, cg, has_s0):
    consts_f, consts_b = refs[8:11], refs[11:14]
    q_ref, k_ref, v_ref, z_ref, sm_ref, wg_ref, bg_ref, on_ref = refs[:8]
    if has_s0:
        s0_ref, o_ref, gk_s, acc_s, st_s = refs[14:]
        so_ref = None
    else:
        o_ref, so_ref, gk_s, acc_s, st_s = refs[14:]
    nc = n // CHUNK
    sm = sm_ref[...]
    for d in range(2):
        x = _dot_hi(sm, wg_ref[d]) + bg_ref[d]
        gk = (jnp.minimum(x, 0.0) - jnp.log(1.0 + jnp.exp(-jnp.abs(x)))) / GLA_TAU
        gk_s[d] = _chunk_cumsum(gk, bool(d))
    acc_s[...] = jnp.zeros((n, DV_C), F32)
    if has_s0:
        st_s[...] = s0_ref[...]
    else:
        st_s[...] = jnp.zeros((2, DK_C, DV_C), F32)

    def body(i, carry):
        where = [(d, (nc - 1 - (i * cg + t)) if d else (i * cg + t)) for d in range(2) for t in range(cg)]
        rows = [pl.ds(pl.multiple_of(c * CHUNK, CHUNK), CHUNK) for _, c in where]
        outs = _gla_prep([(q_ref[r, :] * DK_C ** -0.5, k_ref[r, :], v_ref[r, :], gk_s[d, r, :],
                           consts_b if d else consts_f, bool(d)) for (d, _), r in zip(where, rows)])
        for d in range(2):
            s = st_s[d]
            for t in range(cg):
                intra, qe, dec, kv = outs[d * cg + t]
                acc_s[rows[d * cg + t], :] += intra + _dot(qe, s)
                s = s * dec + kv
            st_s[d] = s
        return carry

    lax.fori_loop(0, nc // cg, body, 0)
    o = acc_s[...]
    o = o * lax.rsqrt(jnp.mean(o * o, axis=-1, keepdims=True) + EPS) * on_ref[...]
    o_ref[...] = o * _silu(z_ref[...])
    if so_ref is not None:
        so_ref[...] = st_s[...]


def _gla_mixer(p, small, w_gate, b_gate, onorm, s0, seq, nb):
    has_s0 = s0 is not None
    consts = _gla_consts(False) + _gla_consts(True)
    const_specs = [pl.BlockSpec(a.shape, (lambda b, h, nd=a.ndim: (0,) * nd)) for a in consts]
    in_specs = [
        pl.BlockSpec((seq, DK_C), lambda b, h: (b, h)),
        pl.BlockSpec((seq, DK_C), lambda b, h: (b, QK_C // DK_C + h)),
        pl.BlockSpec((seq, DV_C), lambda b, h: (b, 2 * QK_C // DV_C + h)),
        pl.BlockSpec((seq, DV_C), lambda b, h: (b, (2 * QK_C + W_C) // DV_C + h)),
        pl.BlockSpec((seq, LANE), lambda b, h: (b, 0)),
        pl.BlockSpec((2, LANE, DK_C), lambda b, h: (0, 0, h)),
        pl.BlockSpec((2, 1, DK_C), lambda b, h: (0, 0, h)),
        pl.BlockSpec((1, DV_C), lambda b, h: (0, 0)),
    ] + const_specs
    args = [p, p, p, p, small, w_gate, b_gate.reshape(2, 1, QK_C), onorm.reshape(1, DV_C)] + consts
    if has_s0:
        in_specs.append(pl.BlockSpec((None, 2, None, DK_C, DV_C), lambda b, h: (b, 0, h, 0, 0)))
        args.append(s0)
    out_specs = [pl.BlockSpec((seq, DV_C), lambda b, h: (b, h))]
    out_shape = [jax.ShapeDtypeStruct((nb * seq, W_C), F32)]
    if not has_s0:
        out_specs.append(pl.BlockSpec((None, 2, None, DK_C, DV_C), lambda b, h: (b, 0, h, 0, 0)))
        out_shape.append(jax.ShapeDtypeStruct((nb, 2, H_C, DK_C, DV_C), F32))
    return pl.pallas_call(
        functools.partial(_gla_kernel, n=seq, cg=GLA_CHUNKS, has_s0=has_s0),
        grid=(nb, H_C),
        in_specs=in_specs,
        out_specs=out_specs,
        out_shape=out_shape,
        scratch_shapes=[pltpu.VMEM((2, seq, DK_C), F32), pltpu.VMEM((seq, DV_C), F32),
                        pltpu.VMEM((2, DK_C, DV_C), F32)],
        compiler_params=_cparams(("arbitrary", "arbitrary")),
        name="gla_lat" if has_s0 else "gla_ctx",
    )(*args)


_EV_TILES = tuple(range(4 * W_A // TN_IN)) + (8, 9, 11, 12, 10)
_EV_ALIGNED = 4 * W_A // TN_IN
_EV_QB = 4 * W_A
_EV_ZB = _EV_QB + W_B
_EV_KB = _EV_ZB + W_B
_EV_VB = _EV_KB + HKV_B * HD
_OD_ALIGNED = (2 * QK_C + 2 * W_C) // TN_IN
_OD_TILES = tuple(range((P_ODD - 2 * GLA_RANK) // TN_IN))
_OD_QD = 2 * QK_C + 2 * W_C
_OD_KD = _OD_QD + W_D
_OD_VD = _OD_KD + W_D
_OD_ZD = _OD_VD + W_D


def _lane_row(v, offset):
    return jnp.pad(v.reshape(1, -1), ((0, 0), (offset, LANE - offset - v.size)))


def _even_layer(xc, xl, e, mod, norm_w, w_in, conv_a, a_log, dt_bias, onorm, sink, w_out, state_delta, cache_kv,
                final_w):
    pc, sc = _in_proj(xc, norm_w, mod, w_in, e, _EV_TILES, _EV_ALIGNED, 4 * W_A, False)
    pq, sq = _in_proj(xl, norm_w, mod, w_in, e, _EV_TILES, _EV_ALIGNED, 4 * W_A, True)
    a_row = _lane_row(a_log, 2 * H_A)
    dt_row = _lane_row(dt_bias, 2 * H_A)
    oa_c, st = _delta_mixer(pc, sc, conv_a, a_row, dt_row, onorm, None, SEQ, BATCH, 4, 2)
    (oa_l,) = _delta_mixer(pq, sq, conv_a, a_row, dt_row, onorm, state_delta[:, e], DEC_SEQ, DEC_BATCH, 2, 4)
    wkv = HKV_B * HD
    ob_c, kv = _ctx_attn(pc, _EV_QB // W_B, _EV_KB // wkv, _EV_VB // wkv, _EV_ZB // W_B, H_B, HKV_B, sink)
    ck = cache_kv[:, e, 0].reshape(DEC_BATCH, PAST_LEN, wkv)
    cv = cache_kv[:, e, 1].reshape(DEC_BATCH, PAST_LEN, wkv)
    ob_l = _win_attn(pq, ck, cv, sink, _EV_QB // W_B, _EV_KB // wkv, _EV_VB // wkv, _EV_ZB // W_B)
    xc = _out_proj(oa_c, ob_c, w_out, e, xc, mod, False, final_w)
    xl = _out_proj(oa_l, ob_l, w_out, e, xl, mod, True, final_w)
    return xc, xl, st, kv.reshape(BATCH, 2, SEQ, HKV_B, HD)


def _odd_layer(xc, xl, o_i, mod, norm_w, w_in, w_glr, b_glr, onorm, rpb, w_out, state_gla, cache_kv, final_w):
    lo = 2 * QK_C + 2 * W_C
    pc, sc = _in_proj(xc, norm_w, mod, w_in, o_i, _OD_TILES, _OD_ALIGNED, lo, False)
    pq, sq = _in_proj(xl, norm_w, mod, w_in, o_i, _OD_TILES, _OD_ALIGNED, lo, True)
    w_gate = jnp.stack([jnp.pad(w_glr[0], ((0, LANE - GLA_RANK), (0, 0))),
                        jnp.pad(w_glr[1], ((GLA_RANK, LANE - 2 * GLA_RANK), (0, 0)))])
    oc_c, st = _gla_mixer(pc, sc, w_gate, b_glr, onorm, None, SEQ, BATCH)
    (oc_l,) = _gla_mixer(pq, sq, w_gate, b_glr, onorm, state_gla[:, o_i], DEC_SEQ, DEC_BATCH)
    od_c, kv = _ctx_attn(pc, _OD_QD // W_D, _OD_KD // W_D, _OD_VD // W_D, _OD_ZD // W_D, H_D, H_D, None)
    ck = cache_kv[:, o_i, 0].reshape(DEC_BATCH, PAST_LEN, W_D)
    cv = cache_kv[:, o_i, 1].reshape(DEC_BATCH, PAST_LEN, W_D)
    od_l = _nbr_attn(pq, ck, cv, _nbr_bias_table(rpb), _OD_QD // W_D, _OD_KD // W_D, _OD_VD // W_D, _OD_ZD // W_D)
    xc = _out_proj(oc_c, od_c, w_out, o_i, xc, mod, False, final_w)
    xl = _out_proj(oc_l, od_l, w_out, o_i, xl, mod, True, final_w)
    return xc, xl, st, kv.reshape(BATCH, 2, SEQ, H_D, HD)


def kernel(x_prompt, x_sample, state_delta, cache_kv_win, state_gla, cache_kv_nbr, c, c_ctx, norm_w, w_ada, b_ada, w_in_even, conv_a, a_log_a, dt_bias_a, onorm_a, sink_b, w_out_even, w_in_odd, w_glr_c, b_glr_c, onorm_c, rpb_d, w_out_odd, final_norm_w):
    xc = x_prompt.reshape(N_CTX, D_MODEL)
    xl = x_sample.reshape(N_LAT, D_MODEL)
    cond = jnp.concatenate([c_ctx[None, :], c, jnp.zeros((N_COND - 1 - DEC_BATCH, D_MODEL), F32)], axis=0)
    mods = _ada_mod(cond, w_ada, b_ada).reshape(DEPTH, N_COND, 1, 3 * D_MODEL)
    wo_even, wo_odd = w_out_even.astype(BF16), w_out_odd.astype(BF16)
    w_in_even, w_in_odd = jnp.swapaxes(w_in_even, 1, 2), jnp.swapaxes(w_in_odd, 1, 2)
    new_delta, new_kvw, new_gla, new_kvn = [], [], [], []
    for li in range(DEPTH):
        final_w = final_norm_w if li == DEPTH - 1 else None
        if li % 2 == 0:
            e = li // 2
            xc, xl, st, kv = _even_layer(xc, xl, e, mods[li], norm_w[li], w_in_even, conv_a[e], a_log_a[e],
                                         dt_bias_a[e], onorm_a[e], sink_b[e], wo_even, state_delta,
                                         cache_kv_win, final_w)
            new_delta.append(st)
            new_kvw.append(kv)
        else:
            o_i = li // 2
            xc, xl, st, kv = _odd_layer(xc, xl, o_i, mods[li], norm_w[li], w_in_odd, w_glr_c[o_i],
                                        b_glr_c[o_i], onorm_c[o_i], rpb_d[o_i], wo_odd, state_gla,
                                        cache_kv_nbr, final_w)
            new_gla.append(st)
            new_kvn.append(kv)
    return (xc.reshape(BATCH, SEQ, D_MODEL), xl.reshape(DEC_BATCH, DEC_SEQ, D_MODEL),
            jnp.stack(new_delta, 1), jnp.stack(new_kvw, 1), jnp.stack(new_gla, 1), jnp.stack(new_kvn, 1))
```

```python
import functools

import numpy as np
import jax
import jax.numpy as jnp
from jax import lax
from jax.experimental import pallas as pl
from jax.experimental.pallas import tpu as pltpu

F32 = jnp.float32
BF16 = jnp.bfloat16
HIGHEST = lax.Precision.HIGHEST

D_MODEL = 2048
BATCH = 16
SEQ = 256
DEPTH = 4
DEC_BATCH = 4
DEC_SEQ = 1024
PAST_LEN = 256
GRID_W = 64
HD = 128
EPS = 1e-6
NEG = -1e30
ROPE_THETA = 10000.0
CHUNK = 64
H_A = 8
W_A = H_A * HD
CONV_K = 5
H_B = 8
HKV_B = 2
W_B = H_B * HD
WIN = 128
QBLK = 128
H_C = 4
DK_C = 128
DV_C = 256
QK_C = H_C * DK_C
W_C = H_C * DV_C
GLA_RANK = 16
GLA_TAU = 16.0
H_D = 8
W_D = H_D * HD
NB_H = 8
NB_W = 16
N_EVEN = (DEPTH + 1) // 2
N_ODD = DEPTH // 2
PA_EVEN = 4 * W_A + 4 * H_A
P_EVEN = PA_EVEN + 2 * W_B + 2 * HKV_B * HD
PC_ODD = 2 * QK_C + 2 * W_C + 2 * GLA_RANK
P_ODD = PC_ODD + 4 * W_D

N_CTX = BATCH * SEQ
N_LAT = DEC_BATCH * DEC_SEQ
N_GRP = N_CTX
assert N_LAT == N_GRP
N_COND = 8
LANE = 128
TM_IN = 1024
TN_IN = 512
GATE_COLS = 32
TM_OUT = 512
TN_ADA = 2048
VMEM_LIMIT = 60000 * 1024


def _cparams(sem):
    return pltpu.CompilerParams(dimension_semantics=sem, vmem_limit_bytes=VMEM_LIMIT)


def _sigmoid(x):
    return 1.0 / (1.0 + jnp.exp(-x))


def _silu(x):
    return x * _sigmoid(x)


def _dot(a, b):
    return jnp.dot(a.astype(BF16), b.astype(BF16), preferred_element_type=F32)


def _dot_nt(a, b):
    return lax.dot_general(a.astype(BF16), b.astype(BF16), (((1,), (1,)), ((), ())),
                           preferred_element_type=F32)


def _dot_tn(a, b):
    return lax.dot_general(a.astype(BF16), b.astype(BF16), (((0,), (0,)), ((), ())),
                           preferred_element_type=F32)


def _dot_hi(a, b):
    return jnp.dot(a, b, precision=HIGHEST, preferred_element_type=F32)


def _ada_kernel(c_ref, w_ref, b_ref, o_ref):
    o_ref[...] = _dot(_silu(c_ref[...]), w_ref[...]) + b_ref[...]


def _ada_mod(cond, w_ada, b_ada):
    n3 = 3 * D_MODEL
    return pl.pallas_call(
        _ada_kernel,
        grid=(DEPTH, n3 // TN_ADA),
        in_specs=[
            pl.BlockSpec((N_COND, D_MODEL), lambda l, j: (0, 0)),
            pl.BlockSpec((None, D_MODEL, TN_ADA), lambda l, j: (l, 0, j)),
            pl.BlockSpec((None, 1, TN_ADA), lambda l, j: (l, 0, j)),
        ],
        out_specs=pl.BlockSpec((None, N_COND, TN_ADA), lambda l, j: (l, 0, j)),
        out_shape=jax.ShapeDtypeStruct((DEPTH, N_COND, n3), F32),
        compiler_params=_cparams(("arbitrary", "arbitrary")),
        name="ada_mod",
    )(cond, w_ada, b_ada.reshape(DEPTH, 1, n3))


def _cond_row(i, tm, latent):
    return 1 + i // (DEC_SEQ // tm) if latent else 0


def _inproj_kernel(src_ref, x_ref, nw_ref, shift_ref, scale_ref, wa_ref, wb_ref, ws_ref, o_ref, os_ref,
                   h_ref, w_ref, *, n_aligned):
    j, i = pl.program_id(0), pl.program_id(1)
    rows = pl.ds(pl.multiple_of(i * TM_IN, TM_IN), TM_IN)

    @pl.when(j == 0)
    def _():
        x = x_ref[...]
        y = x * lax.rsqrt(jnp.mean(x * x, axis=-1, keepdims=True) + EPS) * nw_ref[...]
        h = (y * (1.0 + scale_ref[...]) + shift_ref[...]).astype(BF16)
        h_ref[rows, :] = h
        os_ref[...] = _dot_nt(h, ws_ref[...])

    @pl.when((i == 0) & (j < n_aligned))
    def _():
        w_ref[...] = wa_ref[...].astype(BF16)

    @pl.when((i == 0) & (j >= n_aligned))
    def _():
        w_ref[0:TN_IN - GATE_COLS, :] = wa_ref[GATE_COLS:TN_IN, :].astype(BF16)
        w_ref[TN_IN - GATE_COLS:TN_IN, :] = wb_ref[0:GATE_COLS, :].astype(BF16)

    o_ref[...] = _dot_nt(h_ref[rows, :], w_ref[...])


def _in_proj(x, norm_w, mod, w_t, layer, src_tiles, n_aligned, gate_col, latent):
    nj, ni = len(src_tiles), N_GRP // TM_IN
    sub = TN_IN // LANE
    row = functools.partial(_cond_row, tm=TM_IN, latent=latent)
    tok = lambda j, i, s: (jnp.where(j == 0, i, ni - 1), 0)
    grid_spec = pltpu.PrefetchScalarGridSpec(
        num_scalar_prefetch=1,
        grid=(nj, ni),
        in_specs=[
            pl.BlockSpec((TM_IN, D_MODEL), tok),
            pl.BlockSpec((1, D_MODEL), lambda j, i, s: (0, 0)),
            pl.BlockSpec((None, 1, D_MODEL), lambda j, i, s: (row(jnp.where(j == 0, i, ni - 1)), 0, 0)),
            pl.BlockSpec((None, 1, D_MODEL), lambda j, i, s: (row(jnp.where(j == 0, i, ni - 1)), 0, 1)),
            pl.BlockSpec((None, TN_IN, D_MODEL), lambda j, i, s: (layer, s[j], 0)),
            pl.BlockSpec((None, LANE, D_MODEL), lambda j, i, s: (layer, (s[j] + 1) * sub, 0)),
            pl.BlockSpec((None, LANE, D_MODEL), lambda j, i, s: (layer, gate_col // LANE, 0)),
        ],
        out_specs=[
            pl.BlockSpec((TM_IN, TN_IN), lambda j, i, s: (i, j)),
            pl.BlockSpec((TM_IN, LANE), tok),
        ],
        scratch_shapes=[pltpu.VMEM((N_GRP, D_MODEL), BF16), pltpu.VMEM((TN_IN, D_MODEL), BF16)],
    )
    return pl.pallas_call(
        functools.partial(_inproj_kernel, n_aligned=n_aligned),
        grid_spec=grid_spec,
        out_shape=[jax.ShapeDtypeStruct((N_GRP, nj * TN_IN), F32),
                   jax.ShapeDtypeStruct((N_GRP, LANE), F32)],
        compiler_params=_cparams(("arbitrary", "arbitrary")),
        name="in_proj_lat" if latent else "in_proj_ctx",
    )(jnp.asarray(src_tiles, jnp.int32), x, norm_w.reshape(1, D_MODEL), mod, mod, w_t, w_t, w_t)


def _outproj_kernel(oa_ref, ob_ref, wa_ref, wb_ref, x_ref, g_ref, *rest, final):
    acc = _dot(oa_ref[...], wa_ref[...]) + _dot(ob_ref[...], wb_ref[...])
    xn = x_ref[...] + g_ref[...] * acc
    if final:
        fw_ref, y_ref = rest
        y_ref[...] = xn * lax.rsqrt(jnp.mean(xn * xn, axis=-1, keepdims=True) + EPS) * fw_ref[...]
    else:
        (y_ref,) = rest
        y_ref[...] = xn


def _out_proj(o_a, o_b, w_out, layer, x, mod, latent, final_w=None):
    ka, kb = o_a.shape[1], o_b.shape[1]
    assert ka == kb and w_out.shape[1] == ka + kb
    row = functools.partial(_cond_row, tm=TM_OUT, latent=latent)
    final = final_w is not None
    in_specs = [
        pl.BlockSpec((TM_OUT, ka), lambda i: (i, 0)),
        pl.BlockSpec((TM_OUT, kb), lambda i: (i, 0)),
        pl.BlockSpec((None, ka, D_MODEL), lambda i: (layer, 0, 0)),
        pl.BlockSpec((None, kb, D_MODEL), lambda i: (layer, 1, 0)),
        pl.BlockSpec((TM_OUT, D_MODEL), lambda i: (i, 0)),
        pl.BlockSpec((None, 1, D_MODEL), lambda i: (row(i), 0, 2)),
    ]
    args = [o_a, o_b, w_out, w_out, x, mod]
    if final:
        in_specs.append(pl.BlockSpec((1, D_MODEL), lambda i: (0, 0)))
        args.append(final_w.reshape(1, D_MODEL))
    return pl.pallas_call(
        functools.partial(_outproj_kernel, final=final),
        grid=(N_GRP // TM_OUT,),
        in_specs=in_specs,
        out_specs=pl.BlockSpec((TM_OUT, D_MODEL), lambda i: (i, 0)),
        out_shape=jax.ShapeDtypeStruct((N_GRP, D_MODEL), F32),
        compiler_params=_cparams(("arbitrary",)),
        name="out_proj_final" if final else "out_proj",
    )(*args)


def _attend(problems, scale):
    scores = [[_dot_nt(q, k) for k in ks] for q, ks, _, _, _, _ in problems]
    outs = []
    parts = []
    for (q, ks, vs, masks, biases, sink), raw in zip(problems, scores):
        ss = []
        for s, m, bias in zip(raw, masks, biases):
            s = s * scale
            if bias is not None:
                s = s + bias
            if m is not None:
                s = jnp.where(m, s, NEG)
            ss.append(s)
        mx = functools.reduce(jnp.maximum, [jnp.max(s, axis=-1, keepdims=True) for s in ss])
        if sink is not None:
            mx = jnp.maximum(mx, sink)
        es = [jnp.exp(s - mx) for s in ss]
        den = functools.reduce(jnp.add, [jnp.sum(e, axis=-1, keepdims=True) for e in es])
        if sink is not None:
            den = den + jnp.exp(sink - mx)
        parts.append((es, den))
    pvs = [[_dot(e, v) for e, v in zip(es, vs)] for (es, _), (_, _, vs, _, _, _) in zip(parts, problems)]
    for pv, (_, den) in zip(pvs, parts):
        outs.append(functools.reduce(jnp.add, pv) / den)
    return outs


def _head(ref, h, rows=None):
    if rows is None:
        return ref[:, h * HD:(h + 1) * HD]
    return ref[rows, h * HD:(h + 1) * HD]


def _ctx_attn_kernel(*refs, heads, kv_heads, use_sink, has_prev):
    q_ref, k_ref, v_ref, z_ref = refs[:4]
    sink_ref = refs[4] if use_sink else None
    o_ref, kv_ref = refs[-2:]
    g = heads // kv_heads
    n = q_ref.shape[0]
    problems = []
    for j in range(kv_heads):
        q = jnp.concatenate([_head(q_ref, j * g + t) for t in range(g)], axis=0)
        sink = None
        if use_sink:
            sink = jnp.concatenate([jnp.full((n, 1), sink_ref[j * g + t], F32) for t in range(g)], axis=0)
        problems.append((q, [_head(k_ref, j)], [_head(v_ref, j)], [None], [None], sink))
    outs = _attend(problems, HD ** -0.5)
    for j, o in enumerate(outs):
        for t in range(g):
            h = j * g + t
            o_ref[:, h * HD:(h + 1) * HD] = o[t * n:(t + 1) * n] * _silu(_head(z_ref, h))
    slots = [kv_ref] if has_prev else [kv_ref.at[l] for l in range(kv_ref.shape[0])]
    for slot in slots:
        slot[0] = k_ref[...]
        slot[1] = v_ref[...]


def _ctx_attn(p, q_col, k_col, v_col, z_col, heads, kv_heads, sink, layers, layer, prev):
    wq, wkv = heads * HD, kv_heads * HD
    use_sink = sink is not None
    in_specs = [
        pl.BlockSpec((SEQ, wq), lambda b: (b, q_col)),
        pl.BlockSpec((SEQ, wkv), lambda b: (b, k_col)),
        pl.BlockSpec((SEQ, wkv), lambda b: (b, v_col)),
        pl.BlockSpec((SEQ, wq), lambda b: (b, z_col)),
    ]
    args = [p, p, p, p]
    if use_sink:
        in_specs.append(pl.BlockSpec(memory_space=pltpu.SMEM))
        args.append(sink)
    aliases = {}
    if prev is None:
        kv_spec = pl.BlockSpec((None, layers, 2, SEQ, wkv), lambda b: (b, 0, 0, 0, 0))
    else:
        in_specs.append(pl.BlockSpec(memory_space=pl.ANY))
        args.append(prev)
        aliases = {len(args) - 1: 1}
        kv_spec = pl.BlockSpec((None, None, 2, SEQ, wkv), lambda b: (b, layer, 0, 0, 0))
    return pl.pallas_call(
        functools.partial(_ctx_attn_kernel, heads=heads, kv_heads=kv_heads, use_sink=use_sink,
                          has_prev=prev is not None),
        grid=(BATCH,),
        in_specs=in_specs,
        out_specs=[pl.BlockSpec((SEQ, wq), lambda b: (b, 0)), kv_spec],
        out_shape=[jax.ShapeDtypeStruct((N_CTX, wq), F32),
                   jax.ShapeDtypeStruct((BATCH, layers, 2, SEQ, wkv), F32)],
        input_output_aliases=aliases,
        compiler_params=_cparams(("arbitrary",)),
        name="ctx_attn_sink" if use_sink else "ctx_attn",
    )(*args)


def _rope_tables():
    half = HD // 4
    freq = (ROPE_THETA ** (-np.arange(half, dtype=np.float32) / half)).astype(np.float32)
    t = np.arange(DEC_SEQ)
    ang_r = (t // GRID_W).astype(np.float32)[:, None] * freq[None, :]
    ang_c = (t % GRID_W).astype(np.float32)[:, None] * freq[None, :]
    cos = np.concatenate([np.cos(ang_r)] * 2 + [np.cos(ang_c)] * 2, axis=1).astype(np.float32)
    sin_r, sin_c, zero = np.sin(ang_r), np.sin(ang_c), np.zeros_like(ang_r)
    s_up = np.concatenate([-sin_r, zero, -sin_c, zero], axis=1).astype(np.float32)
    s_dn = np.concatenate([zero, sin_r, zero, sin_c], axis=1).astype(np.float32)
    return jnp.asarray(cos), jnp.asarray(s_up), jnp.asarray(s_dn)


def _rope(x, cos, s_up, s_dn):
    return x * cos + pltpu.roll(x, HD - HD // 4, 1) * s_up + pltpu.roll(x, HD // 4, 1) * s_dn


def _win_attn_kernel(q_ref, k_ref, v_ref, kc_ref, vc_ref, z_ref, cos_ref, sup_ref, sdn_ref, sink_ref, o_ref):
    i = pl.program_id(1)
    g = H_B // HKV_B
    span = QBLK + 2 * WIN
    start = pl.multiple_of(jnp.clip(i * QBLK - WIN, 0, DEC_SEQ - span), QBLK)
    qrows = pl.ds(pl.multiple_of(i * QBLK, QBLK), QBLK)
    krows = pl.ds(start, span)
    cq, uq, dq = cos_ref[qrows, :], sup_ref[qrows, :], sdn_ref[qrows, :]
    ck, uk, dk = cos_ref[krows, :], sup_ref[krows, :], sdn_ref[krows, :]
    qpos = i * QBLK + lax.broadcasted_iota(jnp.int32, (g * QBLK, span), 0) % QBLK
    kpos = start + lax.broadcasted_iota(jnp.int32, (g * QBLK, span), 1)
    band = jnp.abs(qpos - kpos) <= WIN
    problems = []
    for j in range(HKV_B):
        q = jnp.concatenate([_rope(_head(q_ref, j * g + t), cq, uq, dq) for t in range(g)], axis=0)
        kw = _rope(_head(k_ref, j, krows), ck, uk, dk)
        vw = _head(v_ref, j, krows)
        sink = jnp.concatenate([jnp.full((QBLK, 1), sink_ref[j * g + t], F32) for t in range(g)], axis=0)
        problems.append((q, [kw, _head(kc_ref, j)], [vw, _head(vc_ref, j)], [band, None], [None, None], sink))
    outs = _attend(problems, HD ** -0.5)
    for j, o in enumerate(outs):
        for t in range(g):
            h = j * g + t
            o_ref[:, h * HD:(h + 1) * HD] = o[t * QBLK:(t + 1) * QBLK] * _silu(_head(z_ref, h))


def _win_attn(p, cache_k, cache_v, sink, q_col, k_col, v_col, z_col):
    wq, wkv = H_B * HD, HKV_B * HD
    nq = DEC_SEQ // QBLK
    cos, s_up, s_dn = _rope_tables()
    full = pl.BlockSpec((DEC_SEQ, HD), lambda b, i: (0, 0))
    return pl.pallas_call(
        _win_attn_kernel,
        grid=(DEC_BATCH, nq),
        in_specs=[
            pl.BlockSpec((QBLK, wq), lambda b, i: (b * nq + i, q_col)),
            pl.BlockSpec((DEC_SEQ, wkv), lambda b, i: (b, k_col)),
            pl.BlockSpec((DEC_SEQ, wkv), lambda b, i: (b, v_col)),
            pl.BlockSpec((None, PAST_LEN, wkv), lambda b, i: (b, 0, 0)),
            pl.BlockSpec((None, PAST_LEN, wkv), lambda b, i: (b, 0, 0)),
            pl.BlockSpec((QBLK, wq), lambda b, i: (b * nq + i, z_col)),
            full, full, full,
            pl.BlockSpec(memory_space=pltpu.SMEM),
        ],
        out_specs=pl.BlockSpec((QBLK, wq), lambda b, i: (b * nq + i, 0)),
        out_shape=jax.ShapeDtypeStruct((N_LAT, wq), F32),
        compiler_params=_cparams(("arbitrary", "arbitrary")),
        name="win_attn",
    )(p, p, p, cache_k, cache_v, p, cos, s_up, s_dn, sink)


def _nbr_onehot():
    qc = np.arange(GRID_W)[:, None]
    kc = np.arange(GRID_W)[None, :]
    idx = np.clip(kc - qc, -(NB_W - 1), NB_W - 1) + NB_W - 1
    e = (np.arange(2 * NB_W)[:, None, None] == idx[None]).astype(np.float32)
    return jnp.asarray(e.reshape(2 * NB_W, GRID_W * GRID_W))


def _bias_expand_kernel(r_ref, e_ref, o_ref):
    o_ref[...] = _dot_hi(r_ref[...], e_ref[...])


def _nbr_bias_table(rpb):
    rows = H_D * (2 * NB_H - 1)
    r = jnp.pad(rpb.reshape(rows, 2 * NB_W - 1), ((0, 128 - rows), (0, 1)))
    t = pl.pallas_call(
        _bias_expand_kernel,
        out_shape=jax.ShapeDtypeStruct((128, GRID_W * GRID_W), F32),
        name="nbr_bias_expand",
    )(r, _nbr_onehot())
    t = t[:rows].reshape(H_D, 2 * NB_H - 1, GRID_W, GRID_W)
    return jnp.concatenate([t[:, :-1], t[:, 1:]], axis=-1)


def _nbr_attn_kernel(q_ref, k_ref, v_ref, kc_ref, vc_ref, z_ref, t_ref, o_ref):
    r = pl.program_id(1)
    rows = DEC_SEQ // GRID_W
    rs = jnp.clip(r - NB_H // 2, 0, rows - NB_H)
    dr0 = rs - r + NB_H - 1
    nk = NB_H * GRID_W
    krows = pl.ds(pl.multiple_of(rs * GRID_W, GRID_W), nk)
    qc = lax.broadcasted_iota(jnp.int32, (GRID_W, nk), 0)
    kc = lax.broadcasted_iota(jnp.int32, (GRID_W, nk), 1) % GRID_W
    cstart = jnp.clip(qc - NB_W // 2, 0, GRID_W - NB_W)
    ok = (kc >= cstart) & (kc < cstart + NB_W)
    problems = []
    for h in range(H_D):
        bias = jnp.concatenate([t_ref[h, dr0 + 2 * m] for m in range(nk // LANE)], axis=1)
        problems.append((_head(q_ref, h), [_head(k_ref, h, krows), _head(kc_ref, h)],
                         [_head(v_ref, h, krows), _head(vc_ref, h)], [ok, None], [bias, None], None))
    outs = _attend(problems, HD ** -0.5)
    for h, o in enumerate(outs):
        o_ref[:, h * HD:(h + 1) * HD] = o * _silu(_head(z_ref, h))


def _nbr_attn(p, cache_k, cache_v, table, q_col, k_col, v_col, z_col):
    rows = DEC_SEQ // GRID_W
    return pl.pallas_call(
        _nbr_attn_kernel,
        grid=(DEC_BATCH, rows),
        in_specs=[
            pl.BlockSpec((GRID_W, W_D), lambda b, r: (b * rows + r, q_col)),
            pl.BlockSpec((DEC_SEQ, W_D), lambda b, r: (b, k_col)),
            pl.BlockSpec((DEC_SEQ, W_D), lambda b, r: (b, v_col)),
            pl.BlockSpec((None, PAST_LEN, W_D), lambda b, r: (b, 0, 0)),
            pl.BlockSpec((None, PAST_LEN, W_D), lambda b, r: (b, 0, 0)),
            pl.BlockSpec((GRID_W, W_D), lambda b, r: (b * rows + r, z_col)),
            pl.BlockSpec(table.shape, lambda b, r: (0, 0, 0, 0)),
        ],
        out_specs=pl.BlockSpec((GRID_W, W_D), lambda b, r: (b * rows + r, 0)),
        out_shape=jax.ShapeDtypeStruct((N_LAT, W_D), F32),
        compiler_params=_cparams(("arbitrary", "arbitrary")),
        name="nbr_attn",
    )(p, p, p, cache_k, cache_v, p, table)


def _lane_col(x, idx):
    lane = lax.broadcasted_iota(jnp.int32, x.shape, 1)
    return jnp.sum(jnp.where(lane == idx, x, 0.0), axis=-1, keepdims=True)


def _short_conv(x_ref, w_ref, pad_ref, n):
    pad = CONV_K // 2
    zeros = jnp.zeros((8, HD), F32)
    pad_ref[0:8, :] = zeros
    pad_ref[n + 8:n + 16, :] = zeros
    pad_ref[8:n + 8, :] = x_ref[...]
    y = functools.reduce(jnp.add, [pad_ref[8 - pad + t:8 - pad + t + n, :] * w_ref[t:t + 1, :]
                                   for t in range(CONV_K)])
    return _silu(y)


def _l2norm(x):
    return x * lax.rsqrt(jnp.sum(x * x, axis=-1, keepdims=True) + EPS)


def _split2(x):
    hi = x.astype(BF16)
    return hi, (x - hi.astype(F32)).astype(BF16)


def _dots_x2(pairs, split_b=True):
    ops = []
    for a, b in pairs:
        a_hi, a_lo = _split2(a)
        b2 = jnp.concatenate(_split2(b), axis=1) if split_b else b.astype(BF16)
        ops.append((jnp.concatenate([a_hi, a_lo], axis=0), b2))
    rs = [jnp.dot(a2, b2, preferred_element_type=F32) for a2, b2 in ops]
    out = []
    for (a, b), r in zip(pairs, rs):
        m, n = a.shape[0], b.shape[1]
        if split_b:
            out.append((r[m:, :n] + r[:m, n:] + r[m:, n:]) + r[:m, :n])
        else:
            out.append(r[m:] + r[:m])
    return out


def _pair_masks():
    c = lax.broadcasted_iota(jnp.int32, (CHUNK, HD), 0)
    l = lax.broadcasted_iota(jnp.int32, (CHUNK, HD), 1)
    left = l < CHUNK
    j = l % CHUNK
    ahead = jnp.where(left, j - c, c - j)
    return left, ahead <= 0, ahead < 0, j == c


def _block_diag(x, left):
    return jnp.concatenate([jnp.where(left, x, 0.0), jnp.where(left, 0.0, x)], axis=0)


def _inv_unit_triangular_pairs(lmats, left, eye):
    mps = [-x for x in lmats]
    ps = [jnp.where(eye, 1.0, 0.0) + m for m in mps]
    mps = _dots_x2([(m, _block_diag(m, left)) for m in mps])
    for _ in range(4):
        rs = _dots_x2([(jnp.concatenate([p, m], axis=0), _block_diag(m, left)) for p, m in zip(ps, mps)])
        ps = [p + r[:CHUNK] for p, r in zip(ps, rs)]
        mps = [r[CHUNK:] for r in rs]
    rs = _dots_x2([(p, _block_diag(m, left)) for p, m in zip(ps, mps)])
    return [p + r for p, r in zip(ps, rs)]


def _chunk_cumsum(x, reverse):
    n = x.shape[0]
    pos = lax.broadcasted_iota(jnp.int32, x.shape, 0) % CHUNK
    k = 1
    while k < CHUNK:
        if reverse:
            x = x + jnp.where(pos < CHUNK - k, pltpu.roll(x, n - k, 0), 0.0)
        else:
            x = x + jnp.where(pos >= k, pltpu.roll(x, k, 0), 0.0)
        k *= 2
    return x


def _delta_prep(items):
    left, tri, strict, eye = _pair_masks()
    zeros = jnp.zeros((CHUNK, HD), F32)
    pre = []
    for q, k, v, cols in items:
        b_f, b_b, g_f, g_b = (cols[:, t:t + 1] for t in range(4))
        gsel = jnp.where(left, g_f, g_b)
        g_row = jnp.sum(jnp.where(eye, gsel, 0.0), axis=0, keepdims=True)
        decay = jnp.exp(jnp.where(tri, gsel - g_row, NEG))
        kb_f, kb_b = k * b_f, k * b_b
        lhs = jnp.concatenate([jnp.concatenate([kb_f, kb_b], axis=1), jnp.concatenate([q, q], axis=1)], axis=0)
        rhs = jnp.concatenate([jnp.concatenate([k, zeros], axis=1), jnp.concatenate([zeros, k], axis=1)], axis=0)
        pre.append((decay, kb_f, kb_b, g_f, g_b, b_f, b_b, lhs, rhs))
    kqs = [_dot_nt(x[7], x[8]) for x in pre]
    lmats = [jnp.where(strict, kq[:CHUNK] * x[0], 0.0) for kq, x in zip(kqs, pre)]
    attns = [jnp.where(tri, kq[CHUNK:] * x[0], 0.0) for kq, x in zip(kqs, pre)]
    tinvs = _inv_unit_triangular_pairs(lmats, left, eye)
    egs, rhs = [], []
    for (q, k, v, cols), x in zip(items, pre):
        _, kb_f, kb_b, g_f, g_b, b_f, b_b = x[:7]
        eg_f = jnp.exp(jnp.broadcast_to(g_f, (CHUNK, HD)))
        eg_b = jnp.exp(jnp.broadcast_to(g_b, (CHUNK, HD)))
        egs.append((eg_f, eg_b))
        rhs.append(jnp.concatenate([jnp.concatenate([v * b_f, kb_f * eg_f, zeros, zeros], axis=1),
                                    jnp.concatenate([zeros, zeros, v * b_b, kb_b * eg_b], axis=1)], axis=0))
    uws = _dots_x2(list(zip(tinvs, rhs)), split_b=False)
    out = []
    for (q, k, v, cols), x, attn, uw, (eg_f, eg_b) in zip(items, pre, attns, uws, egs):
        g_f, g_b = x[3], x[4]
        gl_f = jnp.broadcast_to(g_f[CHUNK - 1:CHUNK, :], (1, HD))
        gl_b = jnp.broadcast_to(g_b[0:1, :], (1, HD))
        kg_t = jnp.concatenate([k * jnp.exp(gl_f - g_f), k * jnp.exp(gl_b - g_b)], axis=1).T
        out.append(dict(u=[uw[:, :HD], uw[:, 2 * HD:3 * HD]], w=[uw[:, HD:2 * HD], uw[:, 3 * HD:]],
                        attn=[attn[:, :CHUNK], attn[:, CHUNK:]], qg=[q * eg_f, q * eg_b],
                        kg_t=[kg_t[:HD], kg_t[HD:]], eg=[jnp.exp(gl_f), jnp.exp(gl_b)]))
    return out


def _delta_kernel(*refs, n, hb, cg, has_s0, has_prev):
    qp_ref, kp_ref, vp_ref, z_ref, sm_ref, cq_ref, ck_ref, cv_ref, al_ref, dt_ref, on_ref = refs[:11]
    rest = refs[11 + int(has_prev):]
    if has_s0:
        s0_ref, o_ref = rest[:2]
        so_ref = None
    else:
        o_ref, so_ref = rest[:2]
    q_s, k_s, v_s, pad_s, col_s, u_s, w_s, at_s, qg_s, kg_s, eg_s, acc_s, st_s, gate_s = rest[2:]
    h0 = pl.program_id(1) * hb
    nc = n // CHUNK

    @pl.when(pl.program_id(1) == 0)
    def _():
        sm = sm_ref[...]
        x = sm + dt_ref[...]
        softplus = jnp.maximum(x, 0.0) + jnp.log(1.0 + jnp.exp(-jnp.abs(x)))
        gates = -jnp.exp(al_ref[...]) * softplus
        gate_s[0] = _sigmoid(sm)
        gate_s[1] = _chunk_cumsum(gates, False)
        gate_s[2] = _chunk_cumsum(gates, True)

    betas, cum_f, cum_b = gate_s[0], gate_s[1], gate_s[2]
    lane = lax.broadcasted_iota(jnp.int32, (n, HD), 1)
    for j in range(hb):
        cols = slice(j * HD, (j + 1) * HD)
        q_s[j] = _l2norm(_short_conv(qp_ref.at[:, cols], cq_ref.at[:, cols], pad_s, n)) * HD ** -0.5
        k_s[j] = _l2norm(_short_conv(kp_ref.at[:, cols], ck_ref.at[:, cols], pad_s, n))
        v_s[j] = _short_conv(vp_ref.at[:, cols], cv_ref.at[:, cols], pad_s, n)
        col_s[j] = jnp.where(lane == 0, _lane_col(betas, h0 + j),
                             jnp.where(lane == 1, _lane_col(betas, H_A + h0 + j),
                                       jnp.where(lane == 2, _lane_col(cum_f, 2 * H_A + h0 + j),
                                                 _lane_col(cum_b, 3 * H_A + h0 + j))))
    acc_s[...] = jnp.zeros((hb, n, HD), F32)
    if has_s0:
        st_s[...] = s0_ref[...]
    else:
        st_s[...] = jnp.zeros((2, hb, HD, HD), F32)

    def prep(ci, carry):
        where = [(j, ci * cg + t) for j in range(hb) for t in range(cg)]
        rows = [pl.ds(pl.multiple_of(c * CHUNK, CHUNK), CHUNK) for _, c in where]
        outs = _delta_prep([(q_s[j, r, :], k_s[j, r, :], v_s[j, r, :], col_s[j, r, :])
                            for (j, _), r in zip(where, rows)])
        for (j, c), r, o in zip(where, rows, outs):
            for d in range(2):
                u_s[d, j, r, :] = o["u"][d]
                w_s[d, j, r, :] = o["w"][d].astype(BF16)
                at_s[d, j, r, :] = o["attn"][d].astype(BF16)
                qg_s[d, j, r, :] = o["qg"][d].astype(BF16)
                kg_s[d, j, c] = o["kg_t"][d].astype(BF16)
                eg_s[d, j, c] = jnp.broadcast_to(o["eg"][d], (8, HD))
        return carry

    lax.fori_loop(0, nc // cg, prep, 0)

    def scan(i, carry):
        chains = [(d, j, (nc - 1 - i) if d else i) for d in range(2) for j in range(hb)]
        rows = [pl.ds(pl.multiple_of(c * CHUNK, CHUNK), CHUNK) for _, _, c in chains]
        ss = [st_s[d, j] for d, j, _ in chains]
        sbs = [s.astype(BF16) for s in ss]
        ws = [jnp.dot(w_s[d, j, r, :], sb, preferred_element_type=F32) for (d, j, _), r, sb in zip(chains, rows, sbs)]
        vns = [(u_s[d, j, r, :] - w).astype(BF16) for (d, j, _), r, w in zip(chains, rows, ws)]
        for (d, j, c), r, s, sb, vn in zip(chains, rows, ss, sbs, vns):
            st_s[d, j] = s * eg_s[d, j, c, 0:1, :] + jnp.dot(kg_s[d, j, c], vn, preferred_element_type=F32)
        for (d, j, c), r, sb, vn in zip(chains, rows, sbs, vns):
            acc_s[j, r, :] += (jnp.dot(qg_s[d, j, r, :], sb, preferred_element_type=F32)
                               + jnp.dot(at_s[d, j, r, :], vn, preferred_element_type=F32))
        return carry

    lax.fori_loop(0, nc, scan, 0)
    for j in range(hb):
        cols = slice(j * HD, (j + 1) * HD)
        o = acc_s[j]
        o = o * lax.rsqrt(jnp.mean(o * o, axis=-1, keepdims=True) + EPS) * on_ref[...]
        o_ref[:, cols] = o * _silu(z_ref[:, cols])
    if so_ref is not None:
        _write_layer_slots(so_ref, st_s[...], has_prev)


def _write_layer_slots(ref, value, has_prev):
    if has_prev:
        ref[...] = value
    else:
        for l in range(ref.shape[0]):
            ref[l] = value


def _delta_mixer(p, small, conv_w, a_row, dt_row, onorm, s0, seq, nb, hb, cg, layers=1, layer=0, prev=None):
    has_s0 = s0 is not None
    wb = hb * HD
    ng = H_A // hb
    nc = seq // CHUNK
    col = lambda off: (lambda b, h: (b, off + h))
    in_specs = [
        pl.BlockSpec((seq, wb), col(0)),
        pl.BlockSpec((seq, wb), col(ng)),
        pl.BlockSpec((seq, wb), col(2 * ng)),
        pl.BlockSpec((seq, wb), col(3 * ng)),
        pl.BlockSpec((seq, LANE), lambda b, h: (b, 0)),
        pl.BlockSpec((CONV_K, wb), lambda b, h: (0, h)),
        pl.BlockSpec((CONV_K, wb), lambda b, h: (0, ng + h)),
        pl.BlockSpec((CONV_K, wb), lambda b, h: (0, 2 * ng + h)),
        pl.BlockSpec((1, LANE), lambda b, h: (0, 0)),
        pl.BlockSpec((1, LANE), lambda b, h: (0, 0)),
        pl.BlockSpec((1, HD), lambda b, h: (0, 0)),
    ]
    args = [p, p, p, p, small, conv_w, conv_w, conv_w, a_row, dt_row, onorm.reshape(1, HD)]
    if has_s0:
        in_specs.append(pl.BlockSpec((None, 2, hb, HD, HD), lambda b, h: (b, 0, h, 0, 0)))
        args.append(s0)
    out_specs = [pl.BlockSpec((seq, wb), lambda b, h: (b, h))]
    out_shape = [jax.ShapeDtypeStruct((nb * seq, W_A), F32)]
    aliases = {}
    if not has_s0:
        if prev is None:
            out_specs.append(pl.BlockSpec((None, layers, 2, hb, HD, HD), lambda b, h: (b, 0, 0, h, 0, 0)))
        else:
            in_specs.append(pl.BlockSpec(memory_space=pl.ANY))
            args.append(prev)
            aliases = {len(args) - 1: 1}
            out_specs.append(pl.BlockSpec((None, None, 2, hb, HD, HD), lambda b, h: (b, layer, 0, h, 0, 0)))
        out_shape.append(jax.ShapeDtypeStruct((nb, layers, 2, H_A, HD, HD), F32))
    return pl.pallas_call(
        functools.partial(_delta_kernel, n=seq, hb=hb, cg=cg, has_s0=has_s0, has_prev=prev is not None),
        grid=(nb, ng),
        in_specs=in_specs,
        out_specs=out_specs,
        out_shape=out_shape,
        input_output_aliases=aliases,
        scratch_shapes=[
            pltpu.VMEM((hb, seq, HD), F32), pltpu.VMEM((hb, seq, HD), F32), pltpu.VMEM((hb, seq, HD), F32),
            pltpu.VMEM((seq + 16, HD), F32), pltpu.VMEM((hb, seq, HD), F32),
            pltpu.VMEM((2, hb, seq, HD), F32), pltpu.VMEM((2, hb, seq, HD), BF16),
            pltpu.VMEM((2, hb, seq, CHUNK), BF16), pltpu.VMEM((2, hb, seq, HD), BF16),
            pltpu.VMEM((2, hb, nc, HD, CHUNK), BF16), pltpu.VMEM((2, hb, nc, 8, HD), F32),
            pltpu.VMEM((hb, seq, HD), F32), pltpu.VMEM((2, hb, HD, HD), F32),
            pltpu.VMEM((3, seq, LANE), F32)],
        compiler_params=_cparams(("arbitrary", "arbitrary")),
        name="delta_lat" if has_s0 else "delta_ctx",
    )(*args)


_GLA_LEVELS = (32, 16, 8, 4, 2, 1)
GLA_CHUNKS = 4


def _gla_consts(reverse):
    r = np.arange(CHUNK)
    flip = (lambda a: a[::-1, ::-1]) if reverse else (lambda a: a)
    sel, hi, pair = [], [], []
    for m in _GLA_LEVELS:
        mid = (r // (2 * m)) * (2 * m) + m
        is_hi = r >= mid
        sel.append(flip(r[None, :] == mid[:, None] - 1))
        hi.append(flip(is_hi[:, None]))
        pair.append(flip((r[:, None] // (2 * m) == r[None, :] // (2 * m)) & is_hi[:, None] & ~is_hi[None, :]))
    out = [np.concatenate(sel), np.concatenate(hi), np.stack(pair)]
    out = [jnp.asarray(np.ascontiguousarray(a).astype(np.float32)) for a in out]
    return [out[0].astype(BF16), out[1], out[2]]


def _split3(x):
    hi = x.astype(BF16)
    r1 = x - hi.astype(F32)
    mid = r1.astype(BF16)
    return hi, mid, (r1 - mid.astype(F32)).astype(BF16)


def _gla_prep(items):
    nl = len(_GLA_LEVELS)
    r3s = [jnp.dot(consts[0][...], jnp.concatenate(_split3(b), axis=1), preferred_element_type=F32)
           for _, _, _, b, consts, _ in items]
    refs = [(r3[:, 2 * DK_C:] + r3[:, DK_C:2 * DK_C]) + r3[:, :DK_C] for r3 in r3s]
    c = lax.broadcasted_iota(jnp.int32, (CHUNK, CHUNK), 0)
    j = lax.broadcasted_iota(jnp.int32, (CHUNK, CHUNK), 1)
    lvl = []
    for (q, k, v, b, consts, _), ref in zip(items, refs):
        ops = []
        for lv in range(nl):
            rows = slice(lv * CHUNK, (lv + 1) * CHUNK)
            hi = consts[1][rows, :] > 0.5
            t = b - ref[rows]
            e = jnp.exp(jnp.where(hi, t, -t))
            ops.append((jnp.where(hi, q * e, 0.0), jnp.where(hi, 0.0, k * e)))
        lvl.append(ops)
    prods = [[_dot_nt(ql, kl) for ql, kl in ops] for ops in lvl]
    amats = []
    for (q, k, v, b, consts, _), pr in zip(items, prods):
        a = jnp.where(c == j, jnp.sum(q * k, axis=-1, keepdims=True), 0.0)
        for lv in range(nl):
            a = a + pr[lv] * consts[2][lv]
        amats.append(a)
    intras = [_dot(a, it[2]) for a, it in zip(amats, items)]
    r128 = lax.broadcasted_iota(jnp.int32, (DK_C, DK_C), 0)
    c128 = lax.broadcasted_iota(jnp.int32, (DK_C, DK_C), 1)
    out = []
    for (q, k, v, b, consts, reverse), intra in zip(items, intras):
        last = 0 if reverse else CHUNK - 1
        bl = b[last:last + 1, :]
        dec = jnp.sum(jnp.where(r128 == c128, jnp.broadcast_to(jnp.exp(bl), (DK_C, DK_C)), 0.0),
                      axis=-1, keepdims=True)
        out.append((intra, q * jnp.exp(b), dec, _dot_tn(k * jnp.exp(bl - b), v)))
    return out


def _gla_kernel(*refs, n, cg, has_s0, has_prev):
    consts_f, consts_b = refs[8:11], refs[11:14]
    q_ref, k_ref, v_ref, z_ref, sm_ref, wg_ref, bg_ref, on_ref = refs[:8]
    if has_s0:
        s0_ref, o_ref, gk_s, acc_s, st_s = refs[14:]
        so_ref = None
    else:
        o_ref, so_ref, gk_s, acc_s, st_s = refs[14 + int(has_prev):]
    nc = n // CHUNK
    sm = sm_ref[...]
    for d in range(2):
        x = _dot_hi(sm, wg_ref[d]) + bg_ref[d]
        gk = (jnp.minimum(x, 0.0) - jnp.log(1.0 + jnp.exp(-jnp.abs(x)))) / GLA_TAU
        gk_s[d] = _chunk_cumsum(gk, bool(d))
    acc_s[...] = jnp.zeros((n, DV_C), F32)
    if has_s0:
        st_s[...] = s0_ref[...]
    else:
        st_s[...] = jnp.zeros((2, DK_C, DV_C), F32)

    def body(i, carry):
        where = [(d, (nc - 1 - (i * cg + t)) if d else (i * cg + t)) for d in range(2) for t in range(cg)]
        rows = [pl.ds(pl.multiple_of(c * CHUNK, CHUNK), CHUNK) for _, c in where]
        outs = _gla_prep([(q_ref[r, :] * DK_C ** -0.5, k_ref[r, :], v_ref[r, :], gk_s[d, r, :],
                           consts_b if d else consts_f, bool(d)) for (d, _), r in zip(where, rows)])
        for d in range(2):
            s = st_s[d]
            for t in range(cg):
                intra, qe, dec, kv = outs[d * cg + t]
                acc_s[rows[d * cg + t], :] += intra + _dot(qe, s)
                s = s * dec + kv
            st_s[d] = s
        return carry

    lax.fori_loop(0, nc // cg, body, 0)
    o = acc_s[...]
    o = o * lax.rsqrt(jnp.mean(o * o, axis=-1, keepdims=True) + EPS) * on_ref[...]
    o_ref[...] = o * _silu(z_ref[...])
    if so_ref is not None:
        _write_layer_slots(so_ref, st_s[...], has_prev)


def _gla_mixer(p, small, w_gate, b_gate, onorm, s0, seq, nb, layers=1, layer=0, prev=None):
    has_s0 = s0 is not None
    consts = _gla_consts(False) + _gla_consts(True)
    const_specs = [pl.BlockSpec(a.shape, (lambda b, h, nd=a.ndim: (0,) * nd)) for a in consts]
    in_specs = [
        pl.BlockSpec((seq, DK_C), lambda b, h: (b, h)),
        pl.BlockSpec((seq, DK_C), lambda b, h: (b, QK_C // DK_C + h)),
        pl.BlockSpec((seq, DV_C), lambda b, h: (b, 2 * QK_C // DV_C + h)),
        pl.BlockSpec((seq, DV_C), lambda b, h: (b, (2 * QK_C + W_C) // DV_C + h)),
        pl.BlockSpec((seq, LANE), lambda b, h: (b, 0)),
        pl.BlockSpec((2, LANE, DK_C), lambda b, h: (0, 0, h)),
        pl.BlockSpec((2, 1, DK_C), lambda b, h: (0, 0, h)),
        pl.BlockSpec((1, DV_C), lambda b, h: (0, 0)),
    ] + const_specs
    args = [p, p, p, p, small, w_gate, b_gate.reshape(2, 1, QK_C), onorm.reshape(1, DV_C)] + consts
    if has_s0:
        in_specs.append(pl.BlockSpec((None, 2, None, DK_C, DV_C), lambda b, h: (b, 0, h, 0, 0)))
        args.append(s0)
    out_specs = [pl.BlockSpec((seq, DV_C), lambda b, h: (b, h))]
    out_shape = [jax.ShapeDtypeStruct((nb * seq, W_C), F32)]
    aliases = {}
    if not has_s0:
        if prev is None:
            out_specs.append(pl.BlockSpec((None, layers, 2, None, DK_C, DV_C), lambda b, h: (b, 0, 0, h, 0, 0)))
        else:
            in_specs.append(pl.BlockSpec(memory_space=pl.ANY))
            args.append(prev)
            aliases = {len(args) - 1: 1}
            out_specs.append(pl.BlockSpec((None, None, 2, None, DK_C, DV_C), lambda b, h: (b, layer, 0, h, 0, 0)))
        out_shape.append(jax.ShapeDtypeStruct((nb, layers, 2, H_C, DK_C, DV_C), F32))
    return pl.pallas_call(
        functools.partial(_gla_kernel, n=seq, cg=GLA_CHUNKS, has_s0=has_s0, has_prev=prev is not None),
        grid=(nb, H_C),
        in_specs=in_specs,
        out_specs=out_specs,
        out_shape=out_shape,
        input_output_aliases=aliases,
        scratch_shapes=[pltpu.VMEM((2, seq, DK_C), F32), pltpu.VMEM((seq, DV_C), F32),
                        pltpu.VMEM((2, DK_C, DV_C), F32)],
        compiler_params=_cparams(("arbitrary", "arbitrary")),
        name="gla_lat" if has_s0 else "gla_ctx",
    )(*args)


_EV_TILES = tuple(range(4 * W_A // TN_IN)) + (8, 9, 11, 12, 10)
_EV_ALIGNED = 4 * W_A // TN_IN
_EV_QB = 4 * W_A
_EV_ZB = _EV_QB + W_B
_EV_KB = _EV_ZB + W_B
_EV_VB = _EV_KB + HKV_B * HD
_OD_ALIGNED = (2 * QK_C + 2 * W_C) // TN_IN
_OD_TILES = tuple(range((P_ODD - 2 * GLA_RANK) // TN_IN))
_OD_QD = 2 * QK_C + 2 * W_C
_OD_KD = _OD_QD + W_D
_OD_VD = _OD_KD + W_D
_OD_ZD = _OD_VD + W_D


def _lane_row(v, offset):
    return jnp.pad(v.reshape(1, -1), ((0, 0), (offset, LANE - offset - v.size)))


def _even_layer(xc, xl, e, mod, norm_w, w_in, conv_a, a_log, dt_bias, onorm, sink, w_out, state_delta, cache_kv,
                final_w, prev_st, prev_kv):
    pc, sc = _in_proj(xc, norm_w, mod, w_in, e, _EV_TILES, _EV_ALIGNED, 4 * W_A, False)
    pq, sq = _in_proj(xl, norm_w, mod, w_in, e, _EV_TILES, _EV_ALIGNED, 4 * W_A, True)
    a_row = _lane_row(a_log, 2 * H_A)
    dt_row = _lane_row(dt_bias, 2 * H_A)
    oa_c, st = _delta_mixer(pc, sc, conv_a, a_row, dt_row, onorm, None, SEQ, BATCH, 4, 2, N_EVEN, e, prev_st)
    (oa_l,) = _delta_mixer(pq, sq, conv_a, a_row, dt_row, onorm, state_delta[:, e], DEC_SEQ, DEC_BATCH, 4, 2)
    wkv = HKV_B * HD
    ob_c, kv = _ctx_attn(pc, _EV_QB // W_B, _EV_KB // wkv, _EV_VB // wkv, _EV_ZB // W_B, H_B, HKV_B, sink,
                         N_EVEN, e, prev_kv)
    ck = cache_kv[:, e, 0].reshape(DEC_BATCH, PAST_LEN, wkv)
    cv = cache_kv[:, e, 1].reshape(DEC_BATCH, PAST_LEN, wkv)
    ob_l = _win_attn(pq, ck, cv, sink, _EV_QB // W_B, _EV_KB // wkv, _EV_VB // wkv, _EV_ZB // W_B)
    xc = _out_proj(oa_c, ob_c, w_out, e, xc, mod, False, final_w)
    xl = _out_proj(oa_l, ob_l, w_out, e, xl, mod, True, final_w)
    return xc, xl, st, kv


def _odd_layer(xc, xl, o_i, mod, norm_w, w_in, w_glr, b_glr, onorm, rpb, w_out, state_gla, cache_kv, final_w,
               prev_st, prev_kv):
    lo = 2 * QK_C + 2 * W_C
    pc, sc = _in_proj(xc, norm_w, mod, w_in, o_i, _OD_TILES, _OD_ALIGNED, lo, False)
    pq, sq = _in_proj(xl, norm_w, mod, w_in, o_i, _OD_TILES, _OD_ALIGNED, lo, True)
    w_gate = jnp.stack([jnp.pad(w_glr[0], ((0, LANE - GLA_RANK), (0, 0))),
                        jnp.pad(w_glr[1], ((GLA_RANK, LANE - 2 * GLA_RANK), (0, 0)))])
    oc_c, st = _gla_mixer(pc, sc, w_gate, b_glr, onorm, None, SEQ, BATCH, N_ODD, o_i, prev_st)
    (oc_l,) = _gla_mixer(pq, sq, w_gate, b_glr, onorm, state_gla[:, o_i], DEC_SEQ, DEC_BATCH)
    od_c, kv = _ctx_attn(pc, _OD_QD // W_D, _OD_KD // W_D, _OD_VD // W_D, _OD_ZD // W_D, H_D, H_D, None,
                         N_ODD, o_i, prev_kv)
    ck = cache_kv[:, o_i, 0].reshape(DEC_BATCH, PAST_LEN, W_D)
    cv = cache_kv[:, o_i, 1].reshape(DEC_BATCH, PAST_LEN, W_D)
    od_l = _nbr_attn(pq, ck, cv, _nbr_bias_table(rpb), _OD_QD // W_D, _OD_KD // W_D, _OD_VD // W_D, _OD_ZD // W_D)
    xc = _out_proj(oc_c, od_c, w_out, o_i, xc, mod, False, final_w)
    xl = _out_proj(oc_l, od_l, w_out, o_i, xl, mod, True, final_w)
    return xc, xl, st, kv


def kernel(x_prompt, x_sample, state_delta, cache_kv_win, state_gla, cache_kv_nbr, c, c_ctx, norm_w, w_ada, b_ada, w_in_even, conv_a, a_log_a, dt_bias_a, onorm_a, sink_b, w_out_even, w_in_odd, w_glr_c, b_glr_c, onorm_c, rpb_d, w_out_odd, final_norm_w):
    xc = x_prompt.reshape(N_CTX, D_MODEL)
    xl = x_sample.reshape(N_LAT, D_MODEL)
    cond = jnp.concatenate([c_ctx[None, :], c, jnp.zeros((N_COND - 1 - DEC_BATCH, D_MODEL), F32)], axis=0)
    mods = _ada_mod(cond, w_ada, b_ada).reshape(DEPTH, N_COND, 1, 3 * D_MODEL)
    wo_even, wo_odd = w_out_even.astype(BF16), w_out_odd.astype(BF16)
    w_in_even, w_in_odd = jnp.swapaxes(w_in_even, 1, 2), jnp.swapaxes(w_in_odd, 1, 2)
    new_delta = new_kvw = new_gla = new_kvn = None
    for li in range(DEPTH):
        final_w = final_norm_w if li == DEPTH - 1 else None
        if li % 2 == 0:
            e = li // 2
            xc, xl, new_delta, new_kvw = _even_layer(xc, xl, e, mods[li], norm_w[li], w_in_even, conv_a[e], a_log_a[e],
                                                     dt_bias_a[e], onorm_a[e], sink_b[e], wo_even, state_delta,
                                                     cache_kv_win, final_w, new_delta, new_kvw)
        else:
            o_i = li // 2
            xc, xl, new_gla, new_kvn = _odd_layer(xc, xl, o_i, mods[li], norm_w[li], w_in_odd, w_glr_c[o_i],
                                                  b_glr_c[o_i], onorm_c[o_i], rpb_d[o_i], wo_odd, state_gla,
                                                  cache_kv_nbr, final_w, new_gla, new_kvn)
    return (xc.reshape(BATCH, SEQ, D_MODEL), xl.reshape(DEC_BATCH, DEC_SEQ, D_MODEL),
            new_delta, new_kvw.reshape(BATCH, N_EVEN, 2, SEQ, HKV_B, HD),
            new_gla, new_kvn.reshape(BATCH, N_ODD, 2, SEQ, H_D, HD))
```

```python
import functools

import numpy as np
import jax
import jax.numpy as jnp
from jax import lax
from jax.experimental import pallas as pl
from jax.experimental.pallas import tpu as pltpu

F32 = jnp.float32
BF16 = jnp.bfloat16
HIGHEST = lax.Precision.HIGHEST

D_MODEL = 2048
BATCH = 16
SEQ = 256
DEPTH = 4
DEC_BATCH = 4
DEC_SEQ = 1024
PAST_LEN = 256
GRID_W = 64
HD = 128
EPS = 1e-6
NEG = -1e30
ROPE_THETA = 10000.0
CHUNK = 64
H_A = 8
W_A = H_A * HD
CONV_K = 5
H_B = 8
HKV_B = 2
W_B = H_B * HD
WIN = 128
QBLK = 128
H_C = 4
DK_C = 128
DV_C = 256
QK_C = H_C * DK_C
W_C = H_C * DV_C
GLA_RANK = 16
GLA_TAU = 16.0
H_D = 8
W_D = H_D * HD
NB_H = 8
NB_W = 16
N_EVEN = (DEPTH + 1) // 2
N_ODD = DEPTH // 2
PA_EVEN = 4 * W_A + 4 * H_A
P_EVEN = PA_EVEN + 2 * W_B + 2 * HKV_B * HD
PC_ODD = 2 * QK_C + 2 * W_C + 2 * GLA_RANK
P_ODD = PC_ODD + 4 * W_D

N_CTX = BATCH * SEQ
N_LAT = DEC_BATCH * DEC_SEQ
N_GRP = N_CTX
assert N_LAT == N_GRP
N_COND = 8
LANE = 128
TM_IN = 1024
TN_IN = 512
GATE_COLS = 32
TM_OUT = 512
TN_ADA = 2048
VMEM_LIMIT = 60000 * 1024


def _cparams(sem):
    return pltpu.CompilerParams(dimension_semantics=sem, vmem_limit_bytes=VMEM_LIMIT)


def _sigmoid(x):
    return 1.0 / (1.0 + jnp.exp(-x))


def _silu(x):
    return x * _sigmoid(x)


def _dot(a, b):
    return jnp.dot(a.astype(BF16), b.astype(BF16), preferred_element_type=F32)


def _dot_nt(a, b):
    return lax.dot_general(a.astype(BF16), b.astype(BF16), (((1,), (1,)), ((), ())),
                           preferred_element_type=F32)


def _dot_tn(a, b):
    return lax.dot_general(a.astype(BF16), b.astype(BF16), (((0,), (0,)), ((), ())),
                           preferred_element_type=F32)


def _dot_hi(a, b):
    return jnp.dot(a, b, precision=HIGHEST, preferred_element_type=F32)


def _ada_kernel(c_ref, w_ref, b_ref, o_ref):
    o_ref[...] = _dot(_silu(c_ref[...]), w_ref[...]) + b_ref[...]


def _ada_mod(cond, w_ada, b_ada):
    n3 = 3 * D_MODEL
    return pl.pallas_call(
        _ada_kernel,
        grid=(DEPTH, n3 // TN_ADA),
        in_specs=[
            pl.BlockSpec((N_COND, D_MODEL), lambda l, j: (0, 0)),
            pl.BlockSpec((None, D_MODEL, TN_ADA), lambda l, j: (l, 0, j)),
            pl.BlockSpec((None, 1, TN_ADA), lambda l, j: (l, 0, j)),
        ],
        out_specs=pl.BlockSpec((None, N_COND, TN_ADA), lambda l, j: (l, 0, j)),
        out_shape=jax.ShapeDtypeStruct((DEPTH, N_COND, n3), F32),
        compiler_params=_cparams(("arbitrary", "arbitrary")),
        name="ada_mod",
    )(cond, w_ada, b_ada.reshape(DEPTH, 1, n3))


def _cond_row(i, tm, latent):
    return 1 + i // (DEC_SEQ // tm) if latent else 0


def _inproj_kernel(src_ref, x_ref, nw_ref, shift_ref, scale_ref, wa_ref, wb_ref, ws_ref, o_ref, os_ref,
                   h_ref, w_ref, *, n_aligned):
    j, i = pl.program_id(0), pl.program_id(1)
    rows = pl.ds(pl.multiple_of(i * TM_IN, TM_IN), TM_IN)

    @pl.when(j == 0)
    def _():
        x = x_ref[...]
        gain = nw_ref[...] * (1.0 + scale_ref[...])
        h = (x * lax.rsqrt(jnp.mean(x * x, axis=-1, keepdims=True) + EPS) * gain + shift_ref[...]).astype(BF16)
        h_ref[rows, :] = h
        os_ref[...] = _dot_nt(h, ws_ref[...])

    @pl.when((i == 0) & (j < n_aligned))
    def _():
        w_ref[...] = wa_ref[...].astype(BF16)

    @pl.when((i == 0) & (j >= n_aligned))
    def _():
        w_ref[0:TN_IN - GATE_COLS, :] = wa_ref[GATE_COLS:TN_IN, :].astype(BF16)
        w_ref[TN_IN - GATE_COLS:TN_IN, :] = wb_ref[0:GATE_COLS, :].astype(BF16)

    o_ref[...] = _dot_nt(h_ref[rows, :], w_ref[...])


def _in_proj(x, norm_w, mod, w_t, layer, src_tiles, n_aligned, gate_col, latent):
    nj, ni = len(src_tiles), N_GRP // TM_IN
    sub = TN_IN // LANE
    row = functools.partial(_cond_row, tm=TM_IN, latent=latent)
    tok = lambda j, i, s: (jnp.where(j == 0, i, ni - 1), 0)
    grid_spec = pltpu.PrefetchScalarGridSpec(
        num_scalar_prefetch=1,
        grid=(nj, ni),
        in_specs=[
            pl.BlockSpec((TM_IN, D_MODEL), tok),
            pl.BlockSpec((1, D_MODEL), lambda j, i, s: (0, 0)),
            pl.BlockSpec((None, 1, D_MODEL), lambda j, i, s: (row(jnp.where(j == 0, i, ni - 1)), 0, 0)),
            pl.BlockSpec((None, 1, D_MODEL), lambda j, i, s: (row(jnp.where(j == 0, i, ni - 1)), 0, 1)),
            pl.BlockSpec((None, TN_IN, D_MODEL), lambda j, i, s: (layer, s[j], 0)),
            pl.BlockSpec((None, LANE, D_MODEL), lambda j, i, s: (layer, (s[j] + 1) * sub, 0)),
            pl.BlockSpec((None, LANE, D_MODEL), lambda j, i, s: (layer, gate_col // LANE, 0)),
        ],
        out_specs=[
            pl.BlockSpec((TM_IN, TN_IN), lambda j, i, s: (i, j)),
            pl.BlockSpec((TM_IN, LANE), tok),
        ],
        scratch_shapes=[pltpu.VMEM((N_GRP, D_MODEL), BF16), pltpu.VMEM((TN_IN, D_MODEL), BF16)],
    )
    return pl.pallas_call(
        functools.partial(_inproj_kernel, n_aligned=n_aligned),
        grid_spec=grid_spec,
        out_shape=[jax.ShapeDtypeStruct((N_GRP, nj * TN_IN), F32),
                   jax.ShapeDtypeStruct((N_GRP, LANE), F32)],
        compiler_params=_cparams(("arbitrary", "arbitrary")),
        name="in_proj_lat" if latent else "in_proj_ctx",
    )(jnp.asarray(src_tiles, jnp.int32), x, norm_w.reshape(1, D_MODEL), mod, mod, w_t, w_t, w_t)


def _outproj_kernel(oa_ref, ob_ref, wa_ref, wb_ref, x_ref, g_ref, *rest, final):
    acc = _dot(oa_ref[...], wa_ref[...]) + _dot(ob_ref[...], wb_ref[...])
    xn = x_ref[...] + g_ref[...] * acc
    if final:
        fw_ref, y_ref = rest
        y_ref[...] = xn * lax.rsqrt(jnp.mean(xn * xn, axis=-1, keepdims=True) + EPS) * fw_ref[...]
    else:
        (y_ref,) = rest
        y_ref[...] = xn


def _out_proj(o_a, o_b, w_out, layer, x, mod, latent, final_w=None):
    ka, kb = o_a.shape[1], o_b.shape[1]
    assert ka == kb and w_out.shape[1] == ka + kb
    row = functools.partial(_cond_row, tm=TM_OUT, latent=latent)
    final = final_w is not None
    in_specs = [
        pl.BlockSpec((TM_OUT, ka), lambda i: (i, 0)),
        pl.BlockSpec((TM_OUT, kb), lambda i: (i, 0)),
        pl.BlockSpec((None, ka, D_MODEL), lambda i: (layer, 0, 0)),
        pl.BlockSpec((None, kb, D_MODEL), lambda i: (layer, 1, 0)),
        pl.BlockSpec((TM_OUT, D_MODEL), lambda i: (i, 0)),
        pl.BlockSpec((None, 1, D_MODEL), lambda i: (row(i), 0, 2)),
    ]
    args = [o_a, o_b, w_out, w_out, x, mod]
    if final:
        in_specs.append(pl.BlockSpec((1, D_MODEL), lambda i: (0, 0)))
        args.append(final_w.reshape(1, D_MODEL))
    return pl.pallas_call(
        functools.partial(_outproj_kernel, final=final),
        grid=(N_GRP // TM_OUT,),
        in_specs=in_specs,
        out_specs=pl.BlockSpec((TM_OUT, D_MODEL), lambda i: (i, 0)),
        out_shape=jax.ShapeDtypeStruct((N_GRP, D_MODEL), F32),
        compiler_params=_cparams(("arbitrary",)),
        name="out_proj_final" if final else "out_proj",
    )(*args)


def _attend(problems, scale):
    scores = [[_dot_nt(q, k) for k in ks] for q, ks, _, _, _, _ in problems]
    outs = []
    parts = []
    for (q, ks, vs, masks, biases, sink), raw in zip(problems, scores):
        ss = []
        for s, m, bias in zip(raw, masks, biases):
            s = s * scale
            if bias is not None:
                s = s + bias
            if m is not None:
                s = jnp.where(m, s, NEG)
            ss.append(s)
        mx = functools.reduce(jnp.maximum, [jnp.max(s, axis=-1, keepdims=True) for s in ss])
        if sink is not None:
            mx = jnp.maximum(mx, sink)
        es = [jnp.exp(s - mx) for s in ss]
        den = functools.reduce(jnp.add, [jnp.sum(e, axis=-1, keepdims=True) for e in es])
        if sink is not None:
            den = den + jnp.exp(sink - mx)
        parts.append((es, den))
    pvs = [[_dot(e, v) for e, v in zip(es, vs)] for (es, _), (_, _, vs, _, _, _) in zip(parts, problems)]
    for pv, (_, den) in zip(pvs, parts):
        outs.append(functools.reduce(jnp.add, pv) / den)
    return outs


def _head(ref, h, rows=None):
    if rows is None:
        return ref[:, h * HD:(h + 1) * HD]
    return ref[rows, h * HD:(h + 1) * HD]


def _ctx_attn_kernel(*refs, heads, kv_heads, use_sink, has_prev):
    q_ref, k_ref, v_ref, z_ref = refs[:4]
    sink_ref = refs[4] if use_sink else None
    o_ref, kv_ref = refs[-2:]
    g = heads // kv_heads
    n = q_ref.shape[0]
    problems = []
    for j in range(kv_heads):
        q = jnp.concatenate([_head(q_ref, j * g + t) for t in range(g)], axis=0)
        sink = None
        if use_sink:
            sink = jnp.concatenate([jnp.full((n, 1), sink_ref[j * g + t], F32) for t in range(g)], axis=0)
        problems.append((q, [_head(k_ref, j)], [_head(v_ref, j)], [None], [None], sink))
    outs = _attend(problems, HD ** -0.5)
    for j, o in enumerate(outs):
        for t in range(g):
            h = j * g + t
            o_ref[:, h * HD:(h + 1) * HD] = (o[t * n:(t + 1) * n] * _silu(_head(z_ref, h))).astype(BF16)
    slots = [kv_ref] if has_prev else [kv_ref.at[l] for l in range(kv_ref.shape[0])]
    for slot in slots:
        slot[0] = k_ref[...]
        slot[1] = v_ref[...]


def _ctx_attn(p, q_col, k_col, v_col, z_col, heads, kv_heads, sink, layers, layer, prev):
    wq, wkv = heads * HD, kv_heads * HD
    use_sink = sink is not None
    in_specs = [
        pl.BlockSpec((SEQ, wq), lambda b: (b, q_col)),
        pl.BlockSpec((SEQ, wkv), lambda b: (b, k_col)),
        pl.BlockSpec((SEQ, wkv), lambda b: (b, v_col)),
        pl.BlockSpec((SEQ, wq), lambda b: (b, z_col)),
    ]
    args = [p, p, p, p]
    if use_sink:
        in_specs.append(pl.BlockSpec(memory_space=pltpu.SMEM))
        args.append(sink)
    aliases = {}
    if prev is None:
        kv_spec = pl.BlockSpec((None, layers, 2, SEQ, wkv), lambda b: (b, 0, 0, 0, 0))
    else:
        in_specs.append(pl.BlockSpec(memory_space=pl.ANY))
        args.append(prev)
        aliases = {len(args) - 1: 1}
        kv_spec = pl.BlockSpec((None, None, 2, SEQ, wkv), lambda b: (b, layer, 0, 0, 0))
    return pl.pallas_call(
        functools.partial(_ctx_attn_kernel, heads=heads, kv_heads=kv_heads, use_sink=use_sink,
                          has_prev=prev is not None),
        grid=(BATCH,),
        in_specs=in_specs,
        out_specs=[pl.BlockSpec((SEQ, wq), lambda b: (b, 0)), kv_spec],
        out_shape=[jax.ShapeDtypeStruct((N_CTX, wq), BF16),
                   jax.ShapeDtypeStruct((BATCH, layers, 2, SEQ, wkv), F32)],
        input_output_aliases=aliases,
        compiler_params=_cparams(("arbitrary",)),
        name="ctx_attn_sink" if use_sink else "ctx_attn",
    )(*args)


def _rope_tables():
    half = HD // 4
    freq = (ROPE_THETA ** (-np.arange(half, dtype=np.float32) / half)).astype(np.float32)
    t = np.arange(DEC_SEQ)
    ang_r = (t // GRID_W).astype(np.float32)[:, None] * freq[None, :]
    ang_c = (t % GRID_W).astype(np.float32)[:, None] * freq[None, :]
    cos = np.concatenate([np.cos(ang_r)] * 2 + [np.cos(ang_c)] * 2, axis=1).astype(np.float32)
    sin_r, sin_c, zero = np.sin(ang_r), np.sin(ang_c), np.zeros_like(ang_r)
    s_up = np.concatenate([-sin_r, zero, -sin_c, zero], axis=1).astype(np.float32)
    s_dn = np.concatenate([zero, sin_r, zero, sin_c], axis=1).astype(np.float32)
    return jnp.asarray(cos), jnp.asarray(s_up), jnp.asarray(s_dn)


def _rope(x, cos, s_up, s_dn):
    return x * cos + pltpu.roll(x, HD - HD // 4, 1) * s_up + pltpu.roll(x, HD // 4, 1) * s_dn


def _win_attn_kernel(q_ref, k_ref, v_ref, kc_ref, vc_ref, z_ref, cos_ref, sup_ref, sdn_ref, sink_ref, o_ref):
    i = pl.program_id(1)
    g = H_B // HKV_B
    span = QBLK + 2 * WIN
    start = pl.multiple_of(jnp.clip(i * QBLK - WIN, 0, DEC_SEQ - span), QBLK)
    qrows = pl.ds(pl.multiple_of(i * QBLK, QBLK), QBLK)
    krows = pl.ds(start, span)
    cq, uq, dq = cos_ref[qrows, :], sup_ref[qrows, :], sdn_ref[qrows, :]
    ck, uk, dk = cos_ref[krows, :], sup_ref[krows, :], sdn_ref[krows, :]
    qpos = i * QBLK + lax.broadcasted_iota(jnp.int32, (g * QBLK, span), 0) % QBLK
    kpos = start + lax.broadcasted_iota(jnp.int32, (g * QBLK, span), 1)
    band = jnp.abs(qpos - kpos) <= WIN
    problems = []
    for j in range(HKV_B):
        q = jnp.concatenate([_rope(_head(q_ref, j * g + t), cq, uq, dq) for t in range(g)], axis=0)
        kw = _rope(_head(k_ref, j, krows), ck, uk, dk)
        vw = _head(v_ref, j, krows)
        sink = jnp.concatenate([jnp.full((QBLK, 1), sink_ref[j * g + t], F32) for t in range(g)], axis=0)
        problems.append((q, [kw, _head(kc_ref, j)], [vw, _head(vc_ref, j)], [band, None], [None, None], sink))
    outs = _attend(problems, HD ** -0.5)
    for j, o in enumerate(outs):
        for t in range(g):
            h = j * g + t
            o_ref[:, h * HD:(h + 1) * HD] = (o[t * QBLK:(t + 1) * QBLK] * _silu(_head(z_ref, h))).astype(BF16)


def _win_attn(p, cache_k, cache_v, sink, q_col, k_col, v_col, z_col):
    wq, wkv = H_B * HD, HKV_B * HD
    nq = DEC_SEQ // QBLK
    cos, s_up, s_dn = _rope_tables()
    full = pl.BlockSpec((DEC_SEQ, HD), lambda b, i: (0, 0))
    return pl.pallas_call(
        _win_attn_kernel,
        grid=(DEC_BATCH, nq),
        in_specs=[
            pl.BlockSpec((QBLK, wq), lambda b, i: (b * nq + i, q_col)),
            pl.BlockSpec((DEC_SEQ, wkv), lambda b, i: (b, k_col)),
            pl.BlockSpec((DEC_SEQ, wkv), lambda b, i: (b, v_col)),
            pl.BlockSpec((None, PAST_LEN, wkv), lambda b, i: (b, 0, 0)),
            pl.BlockSpec((None, PAST_LEN, wkv), lambda b, i: (b, 0, 0)),
            pl.BlockSpec((QBLK, wq), lambda b, i: (b * nq + i, z_col)),
            full, full, full,
            pl.BlockSpec(memory_space=pltpu.SMEM),
        ],
        out_specs=pl.BlockSpec((QBLK, wq), lambda b, i: (b * nq + i, 0)),
        out_shape=jax.ShapeDtypeStruct((N_LAT, wq), BF16),
        compiler_params=_cparams(("arbitrary", "arbitrary")),
        name="win_attn",
    )(p, p, p, cache_k, cache_v, p, cos, s_up, s_dn, sink)


def _nbr_onehot():
    qc = np.arange(GRID_W)[:, None]
    kc = np.arange(GRID_W)[None, :]
    idx = np.clip(kc - qc, -(NB_W - 1), NB_W - 1) + NB_W - 1
    e = (np.arange(2 * NB_W)[:, None, None] == idx[None]).astype(np.float32)
    return jnp.asarray(e.reshape(2 * NB_W, GRID_W * GRID_W))


def _bias_expand_kernel(r_ref, e_ref, o_ref):
    o_ref[...] = _dot_hi(r_ref[...], e_ref[...])


def _nbr_bias_table(rpb):
    rows = H_D * (2 * NB_H - 1)
    r = jnp.pad(rpb.reshape(rows, 2 * NB_W - 1), ((0, 128 - rows), (0, 1)))
    t = pl.pallas_call(
        _bias_expand_kernel,
        out_shape=jax.ShapeDtypeStruct((128, GRID_W * GRID_W), F32),
        name="nbr_bias_expand",
    )(r, _nbr_onehot())
    t = t[:rows].reshape(H_D, 2 * NB_H - 1, GRID_W, GRID_W)
    return jnp.concatenate([t[:, :-1], t[:, 1:]], axis=-1)


def _nbr_attn_kernel(q_ref, k_ref, v_ref, kc_ref, vc_ref, z_ref, t_ref, o_ref):
    r = pl.program_id(1)
    rows = DEC_SEQ // GRID_W
    rs = jnp.clip(r - NB_H // 2, 0, rows - NB_H)
    dr0 = rs - r + NB_H - 1
    nk = NB_H * GRID_W
    krows = pl.ds(pl.multiple_of(rs * GRID_W, GRID_W), nk)
    qc = lax.broadcasted_iota(jnp.int32, (GRID_W, nk), 0)
    kc = lax.broadcasted_iota(jnp.int32, (GRID_W, nk), 1) % GRID_W
    cstart = jnp.clip(qc - NB_W // 2, 0, GRID_W - NB_W)
    ok = (kc >= cstart) & (kc < cstart + NB_W)
    problems = []
    for h in range(H_D):
        bias = jnp.concatenate([t_ref[h, dr0 + 2 * m] for m in range(nk // LANE)], axis=1)
        problems.append((_head(q_ref, h), [_head(k_ref, h, krows), _head(kc_ref, h)],
                         [_head(v_ref, h, krows), _head(vc_ref, h)], [ok, None], [bias, None], None))
    outs = _attend(problems, HD ** -0.5)
    for h, o in enumerate(outs):
        o_ref[:, h * HD:(h + 1) * HD] = (o * _silu(_head(z_ref, h))).astype(BF16)


def _nbr_attn(p, cache_k, cache_v, table, q_col, k_col, v_col, z_col):
    rows = DEC_SEQ // GRID_W
    return pl.pallas_call(
        _nbr_attn_kernel,
        grid=(DEC_BATCH, rows),
        in_specs=[
            pl.BlockSpec((GRID_W, W_D), lambda b, r: (b * rows + r, q_col)),
            pl.BlockSpec((DEC_SEQ, W_D), lambda b, r: (b, k_col)),
            pl.BlockSpec((DEC_SEQ, W_D), lambda b, r: (b, v_col)),
            pl.BlockSpec((None, PAST_LEN, W_D), lambda b, r: (b, 0, 0)),
            pl.BlockSpec((None, PAST_LEN, W_D), lambda b, r: (b, 0, 0)),
            pl.BlockSpec((GRID_W, W_D), lambda b, r: (b * rows + r, z_col)),
            pl.BlockSpec(table.shape, lambda b, r: (0, 0, 0, 0)),
        ],
        out_specs=pl.BlockSpec((GRID_W, W_D), lambda b, r: (b * rows + r, 0)),
        out_shape=jax.ShapeDtypeStruct((N_LAT, W_D), BF16),
        compiler_params=_cparams(("arbitrary", "arbitrary")),
        name="nbr_attn",
    )(p, p, p, cache_k, cache_v, p, table)


def _lane_col(x, idx):
    lane = lax.broadcasted_iota(jnp.int32, x.shape, 1)
    return jnp.sum(jnp.where(lane == idx, x, 0.0), axis=-1, keepdims=True)


def _short_conv(x_ref, w_ref, pad_ref, n):
    pad = CONV_K // 2
    zeros = jnp.zeros((8, HD), F32)
    pad_ref[0:8, :] = zeros
    pad_ref[n + 8:n + 16, :] = zeros
    pad_ref[8:n + 8, :] = x_ref[...]
    y = functools.reduce(jnp.add, [pad_ref[8 - pad + t:8 - pad + t + n, :] * w_ref[t:t + 1, :]
                                   for t in range(CONV_K)])
    return _silu(y)


def _l2norm(x):
    return x * lax.rsqrt(jnp.sum(x * x, axis=-1, keepdims=True) + EPS)


def _split2(x):
    hi = x.astype(BF16)
    return hi, (x - hi.astype(F32)).astype(BF16)


def _dots_x2(pairs, split_b=True):
    ops = []
    for a, b in pairs:
        a_hi, a_lo = _split2(a)
        b2 = jnp.concatenate(_split2(b), axis=1) if split_b else b.astype(BF16)
        ops.append((jnp.concatenate([a_hi, a_lo], axis=0), b2))
    rs = [jnp.dot(a2, b2, preferred_element_type=F32) for a2, b2 in ops]
    out = []
    for (a, b), r in zip(pairs, rs):
        m, n = a.shape[0], b.shape[1]
        if split_b:
            out.append((r[m:, :n] + r[:m, n:] + r[m:, n:]) + r[:m, :n])
        else:
            out.append(r[m:] + r[:m])
    return out


def _pair_masks():
    c = lax.broadcasted_iota(jnp.int32, (CHUNK, HD), 0)
    l = lax.broadcasted_iota(jnp.int32, (CHUNK, HD), 1)
    left = l < CHUNK
    j = l % CHUNK
    ahead = jnp.where(left, j - c, c - j)
    return left, ahead <= 0, ahead < 0, j == c


def _block_diag(x, left):
    return jnp.concatenate([jnp.where(left, x, 0.0), jnp.where(left, 0.0, x)], axis=0)


def _inv_unit_triangular_pairs(lmats, left, eye):
    mps = [-x for x in lmats]
    ps = [jnp.where(eye, 1.0, 0.0) + m for m in mps]
    mps = _dots_x2([(m, _block_diag(m, left)) for m in mps])
    for _ in range(4):
        rs = _dots_x2([(jnp.concatenate([p, m], axis=0), _block_diag(m, left)) for p, m in zip(ps, mps)])
        ps = [p + r[:CHUNK] for p, r in zip(ps, rs)]
        mps = [r[CHUNK:] for r in rs]
    rs = _dots_x2([(p, _block_diag(m, left)) for p, m in zip(ps, mps)])
    return [p + r for p, r in zip(ps, rs)]


def _chunk_cumsum(x, reverse):
    n = x.shape[0]
    pos = lax.broadcasted_iota(jnp.int32, x.shape, 0) % CHUNK
    k = 1
    while k < CHUNK:
        if reverse:
            x = x + jnp.where(pos < CHUNK - k, pltpu.roll(x, n - k, 0), 0.0)
        else:
            x = x + jnp.where(pos >= k, pltpu.roll(x, k, 0), 0.0)
        k *= 2
    return x


def _delta_prep(items):
    left, tri, strict, eye = _pair_masks()
    zeros = jnp.zeros((CHUNK, HD), F32)
    pre = []
    for q, k, v, cols in items:
        b_f, b_b, g_f, g_b = (cols[:, t:t + 1] for t in range(4))
        gsel = jnp.where(left, g_f, g_b)
        g_row = jnp.sum(jnp.where(eye, gsel, 0.0), axis=0, keepdims=True)
        decay = jnp.exp(jnp.where(tri, gsel - g_row, NEG))
        kb_f, kb_b = k * b_f, k * b_b
        lhs = jnp.concatenate([jnp.concatenate([kb_f, kb_b], axis=1), jnp.concatenate([q, q], axis=1)], axis=0)
        rhs = jnp.concatenate([jnp.concatenate([k, zeros], axis=1), jnp.concatenate([zeros, k], axis=1)], axis=0)
        pre.append((decay, kb_f, kb_b, g_f, g_b, b_f, b_b, lhs, rhs))
    kqs = [_dot_nt(x[7], x[8]) for x in pre]
    lmats = [jnp.where(strict, kq[:CHUNK] * x[0], 0.0) for kq, x in zip(kqs, pre)]
    attns = [jnp.where(tri, kq[CHUNK:] * x[0], 0.0) for kq, x in zip(kqs, pre)]
    tinvs = _inv_unit_triangular_pairs(lmats, left, eye)
    egs, rhs = [], []
    for (q, k, v, cols), x in zip(items, pre):
        _, kb_f, kb_b, g_f, g_b, b_f, b_b = x[:7]
        eg_f = jnp.exp(jnp.broadcast_to(g_f, (CHUNK, HD)))
        eg_b = jnp.exp(jnp.broadcast_to(g_b, (CHUNK, HD)))
        egs.append((eg_f, eg_b))
        rhs.append(jnp.concatenate([jnp.concatenate([v * b_f, kb_f * eg_f, zeros, zeros], axis=1),
                                    jnp.concatenate([zeros, zeros, v * b_b, kb_b * eg_b], axis=1)], axis=0))
    uws = _dots_x2(list(zip(tinvs, rhs)), split_b=False)
    out = []
    for (q, k, v, cols), x, attn, uw, (eg_f, eg_b) in zip(items, pre, attns, uws, egs):
        g_f, g_b = x[3], x[4]
        gl_f = jnp.broadcast_to(g_f[CHUNK - 1:CHUNK, :], (1, HD))
        gl_b = jnp.broadcast_to(g_b[0:1, :], (1, HD))
        kg_t = jnp.concatenate([k * jnp.exp(gl_f - g_f), k * jnp.exp(gl_b - g_b)], axis=1).T
        out.append(dict(u=[uw[:, :HD], uw[:, 2 * HD:3 * HD]], w=[uw[:, HD:2 * HD], uw[:, 3 * HD:]],
                        attn=[attn[:, :CHUNK], attn[:, CHUNK:]], qg=[q * eg_f, q * eg_b],
                        kg_t=[kg_t[:HD], kg_t[HD:]], eg=[jnp.exp(gl_f), jnp.exp(gl_b)]))
    return out


def _delta_kernel(*refs, n, hb, cg, has_s0, has_prev):
    qp_ref, kp_ref, vp_ref, z_ref, sm_ref, cq_ref, ck_ref, cv_ref, al_ref, dt_ref, on_ref = refs[:11]
    rest = refs[11 + int(has_prev):]
    if has_s0:
        s0_ref, o_ref = rest[:2]
        so_ref = None
    else:
        o_ref, so_ref = rest[:2]
    q_s, k_s, v_s, pad_s, col_s, u_s, w_s, at_s, qg_s, kg_s, eg_s, acc_s, st_s, gate_s = rest[2:]
    h0 = pl.program_id(1) * hb
    nc = n // CHUNK

    @pl.when(pl.program_id(1) == 0)
    def _():
        sm = sm_ref[...]
        x = sm + dt_ref[...]
        softplus = jnp.maximum(x, 0.0) + jnp.log(1.0 + jnp.exp(-jnp.abs(x)))
        gates = -jnp.exp(al_ref[...]) * softplus
        gate_s[0] = _sigmoid(sm)
        gate_s[1] = _chunk_cumsum(gates, False)
        gate_s[2] = _chunk_cumsum(gates, True)

    betas, cum_f, cum_b = gate_s[0], gate_s[1], gate_s[2]
    lane = lax.broadcasted_iota(jnp.int32, (n, HD), 1)
    for j in range(hb):
        cols = slice(j * HD, (j + 1) * HD)
        q_s[j] = _l2norm(_short_conv(qp_ref.at[:, cols], cq_ref.at[:, cols], pad_s, n)) * HD ** -0.5
        k_s[j] = _l2norm(_short_conv(kp_ref.at[:, cols], ck_ref.at[:, cols], pad_s, n))
        v_s[j] = _short_conv(vp_ref.at[:, cols], cv_ref.at[:, cols], pad_s, n)
        col_s[j] = jnp.where(lane == 0, _lane_col(betas, h0 + j),
                             jnp.where(lane == 1, _lane_col(betas, H_A + h0 + j),
                                       jnp.where(lane == 2, _lane_col(cum_f, 2 * H_A + h0 + j),
                                                 _lane_col(cum_b, 3 * H_A + h0 + j))))
    acc_s[...] = jnp.zeros((hb, n, HD), F32)
    if has_s0:
        st_s[...] = s0_ref[...]
    else:
        st_s[...] = jnp.zeros((2, hb, HD, HD), F32)

    def prep(ci, carry):
        where = [(j, ci * cg + t) for j in range(hb) for t in range(cg)]
        rows = [pl.ds(pl.multiple_of(c * CHUNK, CHUNK), CHUNK) for _, c in where]
        outs = _delta_prep([(q_s[j, r, :], k_s[j, r, :], v_s[j, r, :], col_s[j, r, :])
                            for (j, _), r in zip(where, rows)])
        for (j, c), r, o in zip(where, rows, outs):
            for d in range(2):
                u_s[d, j, r, :] = o["u"][d]
                w_s[d, j, r, :] = o["w"][d].astype(BF16)
                at_s[d, j, r, :] = o["attn"][d].astype(BF16)
                qg_s[d, j, r, :] = o["qg"][d].astype(BF16)
                kg_s[d, j, c] = o["kg_t"][d].astype(BF16)
                eg_s[d, j, c] = jnp.broadcast_to(o["eg"][d], (8, HD))
        return carry

    lax.fori_loop(0, nc // cg, prep, 0)

    def scan(i, carry):
        chains = [(d, j, (nc - 1 - i) if d else i) for d in range(2) for j in range(hb)]
        rows = [pl.ds(pl.multiple_of(c * CHUNK, CHUNK), CHUNK) for _, _, c in chains]
        ss = [st_s[d, j] for d, j, _ in chains]
        sbs = [s.astype(BF16) for s in ss]
        ws = [jnp.dot(w_s[d, j, r, :], sb, preferred_element_type=F32) for (d, j, _), r, sb in zip(chains, rows, sbs)]
        vns = [(u_s[d, j, r, :] - w).astype(BF16) for (d, j, _), r, w in zip(chains, rows, ws)]
        for (d, j, c), r, s, sb, vn in zip(chains, rows, ss, sbs, vns):
            st_s[d, j] = s * eg_s[d, j, c, 0:1, :] + jnp.dot(kg_s[d, j, c], vn, preferred_element_type=F32)
        for (d, j, c), r, sb, vn in zip(chains, rows, sbs, vns):
            acc_s[j, r, :] += (jnp.dot(qg_s[d, j, r, :], sb, preferred_element_type=F32)
                               + jnp.dot(at_s[d, j, r, :], vn, preferred_element_type=F32))
        return carry

    lax.fori_loop(0, nc, scan, 0)
    for j in range(hb):
        cols = slice(j * HD, (j + 1) * HD)
        o = acc_s[j]
        o = o * lax.rsqrt(jnp.mean(o * o, axis=-1, keepdims=True) + EPS) * on_ref[...]
        o_ref[:, cols] = (o * _silu(z_ref[:, cols])).astype(BF16)
    if so_ref is not None:
        _write_layer_slots(so_ref, st_s[...], has_prev)


def _write_layer_slots(ref, value, has_prev):
    if has_prev:
        ref[...] = value
    else:
        for l in range(ref.shape[0]):
            ref[l] = value


def _delta_mixer(p, small, conv_w, a_row, dt_row, onorm, s0, seq, nb, hb, cg, layers=1, layer=0, prev=None):
    has_s0 = s0 is not None
    wb = hb * HD
    ng = H_A // hb
    nc = seq // CHUNK
    col = lambda off: (lambda b, h: (b, off + h))
    in_specs = [
        pl.BlockSpec((seq, wb), col(0)),
        pl.BlockSpec((seq, wb), col(ng)),
        pl.BlockSpec((seq, wb), col(2 * ng)),
        pl.BlockSpec((seq, wb), col(3 * ng)),
        pl.BlockSpec((seq, LANE), lambda b, h: (b, 0)),
        pl.BlockSpec((CONV_K, wb), lambda b, h: (0, h)),
        pl.BlockSpec((CONV_K, wb), lambda b, h: (0, ng + h)),
        pl.BlockSpec((CONV_K, wb), lambda b, h: (0, 2 * ng + h)),
        pl.BlockSpec((1, LANE), lambda b, h: (0, 0)),
        pl.BlockSpec((1, LANE), lambda b, h: (0, 0)),
        pl.BlockSpec((1, HD), lambda b, h: (0, 0)),
    ]
    args = [p, p, p, p, small, conv_w, conv_w, conv_w, a_row, dt_row, onorm.reshape(1, HD)]
    if has_s0:
        in_specs.append(pl.BlockSpec((None, 2, hb, HD, HD), lambda b, h: (b, 0, h, 0, 0)))
        args.append(s0)
    out_specs = [pl.BlockSpec((seq, wb), lambda b, h: (b, h))]
    out_shape = [jax.ShapeDtypeStruct((nb * seq, W_A), BF16)]
    aliases = {}
    if not has_s0:
        if prev is None:
            out_specs.append(pl.BlockSpec((None, layers, 2, hb, HD, HD), lambda b, h: (b, 0, 0, h, 0, 0)))
        else:
            in_specs.append(pl.BlockSpec(memory_space=pl.ANY))
            args.append(prev)
            aliases = {len(args) - 1: 1}
            out_specs.append(pl.BlockSpec((None, None, 2, hb, HD, HD), lambda b, h: (b, layer, 0, h, 0, 0)))
        out_shape.append(jax.ShapeDtypeStruct((nb, layers, 2, H_A, HD, HD), F32))
    return pl.pallas_call(
        functools.partial(_delta_kernel, n=seq, hb=hb, cg=cg, has_s0=has_s0, has_prev=prev is not None),
        grid=(nb, ng),
        in_specs=in_specs,
        out_specs=out_specs,
        out_shape=out_shape,
        input_output_aliases=aliases,
        scratch_shapes=[
            pltpu.VMEM((hb, seq, HD), F32), pltpu.VMEM((hb, seq, HD), F32), pltpu.VMEM((hb, seq, HD), F32),
            pltpu.VMEM((seq + 16, HD), F32), pltpu.VMEM((hb, seq, HD), F32),
            pltpu.VMEM((2, hb, seq, HD), F32), pltpu.VMEM((2, hb, seq, HD), BF16),
            pltpu.VMEM((2, hb, seq, CHUNK), BF16), pltpu.VMEM((2, hb, seq, HD), BF16),
            pltpu.VMEM((2, hb, nc, HD, CHUNK), BF16), pltpu.VMEM((2, hb, nc, 8, HD), F32),
            pltpu.VMEM((hb, seq, HD), F32), pltpu.VMEM((2, hb, HD, HD), F32),
            pltpu.VMEM((3, seq, LANE), F32)],
        compiler_params=_cparams(("arbitrary", "arbitrary")),
        name="delta_lat" if has_s0 else "delta_ctx",
    )(*args)


_GLA_LEVELS = (32, 16, 8, 4, 2, 1)
GLA_CHUNKS = 4


def _gla_consts(reverse):
    r = np.arange(CHUNK)
    flip = (lambda a: a[::-1, ::-1]) if reverse else (lambda a: a)
    sel, hi, pair = [], [], []
    for m in _GLA_LEVELS:
        mid = (r // (2 * m)) * (2 * m) + m
        is_hi = r >= mid
        sel.append(flip(r[None, :] == mid[:, None] - 1))
        hi.append(flip(is_hi[:, None]))
        pair.append(flip((r[:, None] // (2 * m) == r[None, :] // (2 * m)) & is_hi[:, None] & ~is_hi[None, :]))
    out = [np.concatenate(sel), np.concatenate(hi), np.stack(pair)]
    out = [jnp.asarray(np.ascontiguousarray(a).astype(np.float32)) for a in out]
    return [out[0].astype(BF16), out[1], out[2]]


def _split3(x):
    hi = x.astype(BF16)
    r1 = x - hi.astype(F32)
    mid = r1.astype(BF16)
    return hi, mid, (r1 - mid.astype(F32)).astype(BF16)


def _gla_prep(items):
    nl = len(_GLA_LEVELS)
    r3s = [jnp.dot(consts[0][...], jnp.concatenate(_split3(b), axis=1), preferred_element_type=F32)
           for _, _, _, b, consts, _ in items]
    refs = [(r3[:, 2 * DK_C:] + r3[:, DK_C:2 * DK_C]) + r3[:, :DK_C] for r3 in r3s]
    c = lax.broadcasted_iota(jnp.int32, (CHUNK, CHUNK), 0)
    j = lax.broadcasted_iota(jnp.int32, (CHUNK, CHUNK), 1)
    lvl = []
    for (q, k, v, b, consts, _), ref in zip(items, refs):
        ops = []
        for lv in range(nl):
            rows = slice(lv * CHUNK, (lv + 1) * CHUNK)
            hi = consts[1][rows, :] > 0.5
            t = b - ref[rows]
            e = jnp.exp(jnp.where(hi, t, -t))
            ops.append((jnp.where(hi, q * e, 0.0), jnp.where(hi, 0.0, k * e)))
        lvl.append(ops)
    prods = [[_dot_nt(ql, kl) for ql, kl in ops] for ops in lvl]
    amats = []
    for (q, k, v, b, consts, _), pr in zip(items, prods):
        a = jnp.where(c == j, jnp.sum(q * k, axis=-1, keepdims=True), 0.0)
        for lv in range(nl):
            a = a + pr[lv] * consts[2][lv]
        amats.append(a)
    intras = [_dot(a, it[2]) for a, it in zip(amats, items)]
    r128 = lax.broadcasted_iota(jnp.int32, (DK_C, DK_C), 0)
    c128 = lax.broadcasted_iota(jnp.int32, (DK_C, DK_C), 1)
    out = []
    for (q, k, v, b, consts, reverse), intra in zip(items, intras):
        last = 0 if reverse else CHUNK - 1
        bl = b[last:last + 1, :]
        dec = jnp.sum(jnp.where(r128 == c128, jnp.broadcast_to(jnp.exp(bl), (DK_C, DK_C)), 0.0),
                      axis=-1, keepdims=True)
        out.append((intra, q * jnp.exp(b), dec, _dot_tn(k * jnp.exp(bl - b), v)))
    return out


def _gla_kernel(*refs, n, cg, has_s0, has_prev):
    consts_f, consts_b = refs[8:11], refs[11:14]
    q_ref, k_ref, v_ref, z_ref, sm_ref, wg_ref, bg_ref, on_ref = refs[:8]
    if has_s0:
        s0_ref, o_ref, gk_s, acc_s, st_s = refs[14:]
        so_ref = None
    else:
        o_ref, so_ref, gk_s, acc_s, st_s = refs[14 + int(has_prev):]
    nc = n // CHUNK
    sm = sm_ref[...]
    for d in range(2):
        x = _dot_hi(sm, wg_ref[d]) + bg_ref[d]
        gk = (jnp.minimum(x, 0.0) - jnp.log(1.0 + jnp.exp(-jnp.abs(x)))) / GLA_TAU
        gk_s[d] = _chunk_cumsum(gk, bool(d))
    acc_s[...] = jnp.zeros((n, DV_C), F32)
    if has_s0:
        st_s[...] = s0_ref[...]
    else:
        st_s[...] = jnp.zeros((2, DK_C, DV_C), F32)

    def body(i, carry):
        where = [(d, (nc - 1 - (i * cg + t)) if d else (i * cg + t)) for d in range(2) for t in range(cg)]
        rows = [pl.ds(pl.multiple_of(c * CHUNK, CHUNK), CHUNK) for _, c in where]
        outs = _gla_prep([(q_ref[r, :] * DK_C ** -0.5, k_ref[r, :], v_ref[r, :], gk_s[d, r, :],
                           consts_b if d else consts_f, bool(d)) for (d, _), r in zip(where, rows)])
        for d in range(2):
            s = st_s[d]
            for t in range(cg):
                intra, qe, dec, kv = outs[d * cg + t]
                acc_s[rows[d * cg + t], :] += intra + _dot(qe, s)
                s = s * dec + kv
            st_s[d] = s
        return carry

    lax.fori_loop(0, nc // cg, body, 0)
    o = acc_s[...]
    o = o * lax.rsqrt(jnp.mean(o * o, axis=-1, keepdims=True) + EPS) * on_ref[...]
    o_ref[...] = (o * _silu(z_ref[...])).astype(BF16)
    if so_ref is not None:
        _write_layer_slots(so_ref, st_s[...], has_prev)


def _gla_mixer(p, small, w_gate, b_gate, onorm, s0, seq, nb, layers=1, layer=0, prev=None):
    has_s0 = s0 is not None
    consts = _gla_consts(False) + _gla_consts(True)
    const_specs = [pl.BlockSpec(a.shape, (lambda b, h, nd=a.ndim: (0,) * nd)) for a in consts]
    in_specs = [
        pl.BlockSpec((seq, DK_C), lambda b, h: (b, h)),
        pl.BlockSpec((seq, DK_C), lambda b, h: (b, QK_C // DK_C + h)),
        pl.BlockSpec((seq, DV_C), lambda b, h: (b, 2 * QK_C // DV_C + h)),
        pl.BlockSpec((seq, DV_C), lambda b, h: (b, (2 * QK_C + W_C) // DV_C + h)),
        pl.BlockSpec((seq, LANE), lambda b, h: (b, 0)),
        pl.BlockSpec((2, LANE, DK_C), lambda b, h: (0, 0, h)),
        pl.BlockSpec((2, 1, DK_C), lambda b, h: (0, 0, h)),
        pl.BlockSpec((1, DV_C), lambda b, h: (0, 0)),
    ] + const_specs
    args = [p, p, p, p, small, w_gate, b_gate.reshape(2, 1, QK_C), onorm.reshape(1, DV_C)] + consts
    if has_s0:
        in_specs.append(pl.BlockSpec((None, 2, None, DK_C, DV_C), lambda b, h: (b, 0, h, 0, 0)))
        args.append(s0)
    out_specs = [pl.BlockSpec((seq, DV_C), lambda b, h: (b, h))]
    out_shape = [jax.ShapeDtypeStruct((nb * seq, W_C), BF16)]
    aliases = {}
    if not has_s0:
        if prev is None:
            out_specs.append(pl.BlockSpec((None, layers, 2, None, DK_C, DV_C), lambda b, h: (b, 0, 0, h, 0, 0)))
        else:
            in_specs.append(pl.BlockSpec(memory_space=pl.ANY))
            args.append(prev)
            aliases = {len(args) - 1: 1}
            out_specs.append(pl.BlockSpec((None, None, 2, None, DK_C, DV_C), lambda b, h: (b, layer, 0, h, 0, 0)))
        out_shape.append(jax.ShapeDtypeStruct((nb, layers, 2, H_C, DK_C, DV_C), F32))
    return pl.pallas_call(
        functools.partial(_gla_kernel, n=seq, cg=GLA_CHUNKS, has_s0=has_s0, has_prev=prev is not None),
        grid=(nb, H_C),
        in_specs=in_specs,
        out_specs=out_specs,
        out_shape=out_shape,
        input_output_aliases=aliases,
        scratch_shapes=[pltpu.VMEM((2, seq, DK_C), F32), pltpu.VMEM((seq, DV_C), F32),
                        pltpu.VMEM((2, DK_C, DV_C), F32)],
        compiler_params=_cparams(("arbitrary", "arbitrary")),
        name="gla_lat" if has_s0 else "gla_ctx",
    )(*args)


_EV_TILES = tuple(range(4 * W_A // TN_IN)) + (8, 9, 11, 12, 10)
_EV_ALIGNED = 4 * W_A // TN_IN
_EV_QB = 4 * W_A
_EV_ZB = _EV_QB + W_B
_EV_KB = _EV_ZB + W_B
_EV_VB = _EV_KB + HKV_B * HD
_OD_ALIGNED = (2 * QK_C + 2 * W_C) // TN_IN
_OD_TILES = tuple(range((P_ODD - 2 * GLA_RANK) // TN_IN))
_OD_QD = 2 * QK_C + 2 * W_C
_OD_KD = _OD_QD + W_D
_OD_VD = _OD_KD + W_D
_OD_ZD = _OD_VD + W_D


def _lane_row(v, offset):
    return jnp.pad(v.reshape(1, -1), ((0, 0), (offset, LANE - offset - v.size)))


def _even_layer(xc, xl, e, mod, norm_w, w_in, conv_a, a_log, dt_bias, onorm, sink, w_out, state_delta, cache_kv,
                final_w, prev_st, prev_kv):
    pc, sc = _in_proj(xc, norm_w, mod, w_in, e, _EV_TILES, _EV_ALIGNED, 4 * W_A, False)
    pq, sq = _in_proj(xl, norm_w, mod, w_in, e, _EV_TILES, _EV_ALIGNED, 4 * W_A, True)
    a_row = _lane_row(a_log, 2 * H_A)
    dt_row = _lane_row(dt_bias, 2 * H_A)
    oa_c, st = _delta_mixer(pc, sc, conv_a, a_row, dt_row, onorm, None, SEQ, BATCH, 8, 1, N_EVEN, e, prev_st)
    (oa_l,) = _delta_mixer(pq, sq, conv_a, a_row, dt_row, onorm, state_delta[:, e], DEC_SEQ, DEC_BATCH, 4, 2)
    wkv = HKV_B * HD
    ob_c, kv = _ctx_attn(pc, _EV_QB // W_B, _EV_KB // wkv, _EV_VB // wkv, _EV_ZB // W_B, H_B, HKV_B, sink,
                         N_EVEN, e, prev_kv)
    ck = cache_kv[:, e, 0].reshape(DEC_BATCH, PAST_LEN, wkv)
    cv = cache_kv[:, e, 1].reshape(DEC_BATCH, PAST_LEN, wkv)
    ob_l = _win_attn(pq, ck, cv, sink, _EV_QB // W_B, _EV_KB // wkv, _EV_VB // wkv, _EV_ZB // W_B)
    xc = _out_proj(oa_c, ob_c, w_out, e, xc, mod, False, final_w)
    xl = _out_proj(oa_l, ob_l, w_out, e, xl, mod, True, final_w)
    return xc, xl, st, kv


def _odd_layer(xc, xl, o_i, mod, norm_w, w_in, w_glr, b_glr, onorm, rpb, w_out, state_gla, cache_kv, final_w,
               prev_st, prev_kv):
    lo = 2 * QK_C + 2 * W_C
    pc, sc = _in_proj(xc, norm_w, mod, w_in, o_i, _OD_TILES, _OD_ALIGNED, lo, False)
    pq, sq = _in_proj(xl, norm_w, mod, w_in, o_i, _OD_TILES, _OD_ALIGNED, lo, True)
    w_gate = jnp.stack([jnp.pad(w_glr[0], ((0, LANE - GLA_RANK), (0, 0))),
                        jnp.pad(w_glr[1], ((GLA_RANK, LANE - 2 * GLA_RANK), (0, 0)))])
    oc_c, st = _gla_mixer(pc, sc, w_gate, b_glr, onorm, None, SEQ, BATCH, N_ODD, o_i, prev_st)
    (oc_l,) = _gla_mixer(pq, sq, w_gate, b_glr, onorm, state_gla[:, o_i], DEC_SEQ, DEC_BATCH)
    od_c, kv = _ctx_attn(pc, _OD_QD // W_D, _OD_KD // W_D, _OD_VD // W_D, _OD_ZD // W_D, H_D, H_D, None,
                         N_ODD, o_i, prev_kv)
    ck = cache_kv[:, o_i, 0].reshape(DEC_BATCH, PAST_LEN, W_D)
    cv = cache_kv[:, o_i, 1].reshape(DEC_BATCH, PAST_LEN, W_D)
    od_l = _nbr_attn(pq, ck, cv, _nbr_bias_table(rpb), _OD_QD // W_D, _OD_KD // W_D, _OD_VD // W_D, _OD_ZD // W_D)
    xc = _out_proj(oc_c, od_c, w_out, o_i, xc, mod, False, final_w)
    xl = _out_proj(oc_l, od_l, w_out, o_i, xl, mod, True, final_w)
    return xc, xl, st, kv


def kernel(x_prompt, x_sample, state_delta, cache_kv_win, state_gla, cache_kv_nbr, c, c_ctx, norm_w, w_ada, b_ada, w_in_even, conv_a, a_log_a, dt_bias_a, onorm_a, sink_b, w_out_even, w_in_odd, w_glr_c, b_glr_c, onorm_c, rpb_d, w_out_odd, final_norm_w):
    xc = x_prompt.reshape(N_CTX, D_MODEL)
    xl = x_sample.reshape(N_LAT, D_MODEL)
    cond = jnp.concatenate([c_ctx[None, :], c, jnp.zeros((N_COND - 1 - DEC_BATCH, D_MODEL), F32)], axis=0)
    mods = _ada_mod(cond, w_ada, b_ada).reshape(DEPTH, N_COND, 1, 3 * D_MODEL)
    wo_even, wo_odd = w_out_even.astype(BF16), w_out_odd.astype(BF16)
    w_in_even, w_in_odd = jnp.swapaxes(w_in_even, 1, 2), jnp.swapaxes(w_in_odd, 1, 2)
    new_delta = new_kvw = new_gla = new_kvn = None
    for li in range(DEPTH):
        final_w = final_norm_w if li == DEPTH - 1 else None
        if li % 2 == 0:
            e = li // 2
            xc, xl, new_delta, new_kvw = _even_layer(xc, xl, e, mods[li], norm_w[li], w_in_even, conv_a[e], a_log_a[e],
                                                     dt_bias_a[e], onorm_a[e], sink_b[e], wo_even, state_delta,
                                                     cache_kv_win, final_w, new_delta, new_kvw)
        else:
            o_i = li // 2
            xc, xl, new_gla, new_kvn = _odd_layer(xc, xl, o_i, mods[li], norm_w[li], w_in_odd, w_glr_c[o_i],
                                                  b_glr_c[o_i], onorm_c[o_i], rpb_d[o_i], wo_odd, state_gla,
                                                  cache_kv_nbr, final_w, new_gla, new_kvn)
    return (xc.reshape(BATCH, SEQ, D_MODEL), xl.reshape(DEC_BATCH, DEC_SEQ, D_MODEL),
            new_delta, new_kvw.reshape(BATCH, N_EVEN, 2, SEQ, HKV_B, HD),
            new_gla, new_kvn.reshape(BATCH, N_ODD, 2, SEQ, H_D, HD))
```

```python
import functools

import numpy as np
import jax
import jax.numpy as jnp
from jax import lax
from jax.experimental import pallas as pl
from jax.experimental.pallas import tpu as pltpu

F32 = jnp.float32
BF16 = jnp.bfloat16
HIGHEST = lax.Precision.HIGHEST

D_MODEL = 2048
BATCH = 16
SEQ = 256
DEPTH = 4
DEC_BATCH = 4
DEC_SEQ = 1024
PAST_LEN = 256
GRID_W = 64
HD = 128
EPS = 1e-6
NEG = -1e30
ROPE_THETA = 10000.0
CHUNK = 64
H_A = 8
W_A = H_A * HD
CONV_K = 5
H_B = 8
HKV_B = 2
W_B = H_B * HD
WIN = 128
QBLK = 128
H_C = 4
DK_C = 128
DV_C = 256
QK_C = H_C * DK_C
W_C = H_C * DV_C
GLA_RANK = 16
GLA_TAU = 16.0
H_D = 8
W_D = H_D * HD
NB_H = 8
NB_W = 16
N_EVEN = (DEPTH + 1) // 2
N_ODD = DEPTH // 2
PA_EVEN = 4 * W_A + 4 * H_A
P_EVEN = PA_EVEN + 2 * W_B + 2 * HKV_B * HD
PC_ODD = 2 * QK_C + 2 * W_C + 2 * GLA_RANK
P_ODD = PC_ODD + 4 * W_D

N_CTX = BATCH * SEQ
N_LAT = DEC_BATCH * DEC_SEQ
N_GRP = N_CTX
assert N_LAT == N_GRP
N_COND = 8
LANE = 128
TM_IN = 1024
TN_IN = 512
GATE_COLS = 32
TM_OUT = 512
TN_ADA = 2048
VMEM_LIMIT = 60000 * 1024


def _cparams(sem):
    return pltpu.CompilerParams(dimension_semantics=sem, vmem_limit_bytes=VMEM_LIMIT)


def _sigmoid(x):
    return 1.0 / (1.0 + jnp.exp(-x))


def _silu(x):
    return x * _sigmoid(x)


def _dot(a, b):
    return jnp.dot(a.astype(BF16), b.astype(BF16), preferred_element_type=F32)


def _dot_nt(a, b):
    return lax.dot_general(a.astype(BF16), b.astype(BF16), (((1,), (1,)), ((), ())),
                           preferred_element_type=F32)


def _dot_tn(a, b):
    return lax.dot_general(a.astype(BF16), b.astype(BF16), (((0,), (0,)), ((), ())),
                           preferred_element_type=F32)


def _dot_hi(a, b):
    return jnp.dot(a, b, precision=HIGHEST, preferred_element_type=F32)


def _ada_kernel(c_ref, w_ref, b_ref, o_ref):
    o_ref[...] = _dot(_silu(c_ref[...]), w_ref[...]) + b_ref[...]


def _ada_mod(cond, w_ada, b_ada):
    n3 = 3 * D_MODEL
    return pl.pallas_call(
        _ada_kernel,
        grid=(DEPTH, n3 // TN_ADA),
        in_specs=[
            pl.BlockSpec((N_COND, D_MODEL), lambda l, j: (0, 0)),
            pl.BlockSpec((None, D_MODEL, TN_ADA), lambda l, j: (l, 0, j)),
            pl.BlockSpec((None, 1, TN_ADA), lambda l, j: (l, 0, j)),
        ],
        out_specs=pl.BlockSpec((None, N_COND, TN_ADA), lambda l, j: (l, 0, j)),
        out_shape=jax.ShapeDtypeStruct((DEPTH, N_COND, n3), F32),
        compiler_params=_cparams(("arbitrary", "arbitrary")),
        name="ada_mod",
    )(cond, w_ada, b_ada.reshape(DEPTH, 1, n3))


def _cond_row(i, tm, latent):
    return 1 + i // (DEC_SEQ // tm) if latent else 0


def _inproj_kernel(src_ref, x_ref, nw_ref, shift_ref, scale_ref, wa_ref, wb_ref, ws_ref, o_ref, os_ref,
                   h_ref, w_ref, *, n_aligned):
    j, i = pl.program_id(0), pl.program_id(1)
    rows = pl.ds(pl.multiple_of(i * TM_IN, TM_IN), TM_IN)

    @pl.when(j == 0)
    def _():
        x = x_ref[...]
        gain = nw_ref[...] * (1.0 + scale_ref[...])
        h = (x * lax.rsqrt(jnp.mean(x * x, axis=-1, keepdims=True) + EPS) * gain + shift_ref[...]).astype(BF16)
        h_ref[rows, :] = h
        os_ref[...] = _dot_nt(h, ws_ref[...])

    @pl.when((i == 0) & (j < n_aligned))
    def _():
        w_ref[...] = wa_ref[...].astype(BF16)

    @pl.when((i == 0) & (j >= n_aligned))
    def _():
        w_ref[0:TN_IN - GATE_COLS, :] = wa_ref[GATE_COLS:TN_IN, :].astype(BF16)
        w_ref[TN_IN - GATE_COLS:TN_IN, :] = wb_ref[0:GATE_COLS, :].astype(BF16)

    o_ref[...] = _dot_nt(h_ref[rows, :], w_ref[...])


def _in_proj(x, norm_w, mod, w_t, layer, src_tiles, n_aligned, gate_col, latent):
    nj, ni = len(src_tiles), N_GRP // TM_IN
    sub = TN_IN // LANE
    row = functools.partial(_cond_row, tm=TM_IN, latent=latent)
    tok = lambda j, i, s: (jnp.where(j == 0, i, ni - 1), 0)
    grid_spec = pltpu.PrefetchScalarGridSpec(
        num_scalar_prefetch=1,
        grid=(nj, ni),
        in_specs=[
            pl.BlockSpec((TM_IN, D_MODEL), tok),
            pl.BlockSpec((1, D_MODEL), lambda j, i, s: (0, 0)),
            pl.BlockSpec((None, 1, D_MODEL), lambda j, i, s: (row(jnp.where(j == 0, i, ni - 1)), 0, 0)),
            pl.BlockSpec((None, 1, D_MODEL), lambda j, i, s: (row(jnp.where(j == 0, i, ni - 1)), 0, 1)),
            pl.BlockSpec((None, TN_IN, D_MODEL), lambda j, i, s: (layer, s[j], 0)),
            pl.BlockSpec((None, LANE, D_MODEL), lambda j, i, s: (layer, (s[j] + 1) * sub, 0)),
            pl.BlockSpec((None, LANE, D_MODEL), lambda j, i, s: (layer, gate_col // LANE, 0)),
        ],
        out_specs=[
            pl.BlockSpec((TM_IN, TN_IN), lambda j, i, s: (i, j)),
            pl.BlockSpec((TM_IN, LANE), tok),
        ],
        scratch_shapes=[pltpu.VMEM((N_GRP, D_MODEL), BF16), pltpu.VMEM((TN_IN, D_MODEL), BF16)],
    )
    return pl.pallas_call(
        functools.partial(_inproj_kernel, n_aligned=n_aligned),
        grid_spec=grid_spec,
        out_shape=[jax.ShapeDtypeStruct((N_GRP, nj * TN_IN), F32),
                   jax.ShapeDtypeStruct((N_GRP, LANE), F32)],
        compiler_params=_cparams(("arbitrary", "arbitrary")),
        name="in_proj_lat" if latent else "in_proj_ctx",
    )(jnp.asarray(src_tiles, jnp.int32), x, norm_w.reshape(1, D_MODEL), mod, mod, w_t, w_t, w_t)


def _outproj_kernel(oa_ref, ob_ref, wa_ref, wb_ref, x_ref, g_ref, *rest, final):
    acc = _dot(oa_ref[...], wa_ref[...]) + _dot(ob_ref[...], wb_ref[...])
    xn = x_ref[...] + g_ref[...] * acc
    if final:
        fw_ref, y_ref = rest
        y_ref[...] = xn * lax.rsqrt(jnp.mean(xn * xn, axis=-1, keepdims=True) + EPS) * fw_ref[...]
    else:
        (y_ref,) = rest
        y_ref[...] = xn


def _out_proj(o_a, o_b, w_out, layer, x, mod, latent, final_w=None):
    ka, kb = o_a.shape[1], o_b.shape[1]
    assert ka == kb and w_out.shape[1] == ka + kb
    row = functools.partial(_cond_row, tm=TM_OUT, latent=latent)
    final = final_w is not None
    in_specs = [
        pl.BlockSpec((TM_OUT, ka), lambda i: (i, 0)),
        pl.BlockSpec((TM_OUT, kb), lambda i: (i, 0)),
        pl.BlockSpec((None, ka, D_MODEL), lambda i: (layer, 0, 0)),
        pl.BlockSpec((None, kb, D_MODEL), lambda i: (layer, 1, 0)),
        pl.BlockSpec((TM_OUT, D_MODEL), lambda i: (i, 0)),
        pl.BlockSpec((None, 1, D_MODEL), lambda i: (row(i), 0, 2)),
    ]
    args = [o_a, o_b, w_out, w_out, x, mod]
    if final:
        in_specs.append(pl.BlockSpec((1, D_MODEL), lambda i: (0, 0)))
        args.append(final_w.reshape(1, D_MODEL))
    return pl.pallas_call(
        functools.partial(_outproj_kernel, final=final),
        grid=(N_GRP // TM_OUT,),
        in_specs=in_specs,
        out_specs=pl.BlockSpec((TM_OUT, D_MODEL), lambda i: (i, 0)),
        out_shape=jax.ShapeDtypeStruct((N_GRP, D_MODEL), F32),
        compiler_params=_cparams(("arbitrary",)),
        name="out_proj_final" if final else "out_proj",
    )(*args)


def _attend(problems, scale):
    scores = [[_dot_nt(q, k) for k in ks] for q, ks, _, _, _, _ in problems]
    outs = []
    parts = []
    for (q, ks, vs, masks, biases, sink), raw in zip(problems, scores):
        ss = []
        for s, m, bias in zip(raw, masks, biases):
            s = s * scale
            if bias is not None:
                s = s + bias
            if m is not None:
                s = jnp.where(m, s, NEG)
            ss.append(s)
        mx = functools.reduce(jnp.maximum, [jnp.max(s, axis=-1, keepdims=True) for s in ss])
        if sink is not None:
            mx = jnp.maximum(mx, sink)
        es = [jnp.exp(s - mx) for s in ss]
        den = functools.reduce(jnp.add, [jnp.sum(e, axis=-1, keepdims=True) for e in es])
        if sink is not None:
            den = den + jnp.exp(sink - mx)
        parts.append((es, den))
    pvs = [[_dot(e, v) for e, v in zip(es, vs)] for (es, _), (_, _, vs, _, _, _) in zip(parts, problems)]
    for pv, (_, den) in zip(pvs, parts):
        outs.append(functools.reduce(jnp.add, pv) / den)
    return outs


def _head(ref, h, rows=None):
    if rows is None:
        return ref[:, h * HD:(h + 1) * HD]
    return ref[rows, h * HD:(h + 1) * HD]


def _ctx_attn_kernel(*refs, heads, kv_heads, use_sink, has_prev):
    q_ref, k_ref, v_ref, z_ref = refs[:4]
    sink_ref = refs[4] if use_sink else None
    o_ref, kv_ref = refs[-2:]
    g = heads // kv_heads
    n = q_ref.shape[0]
    problems = []
    for j in range(kv_heads):
        q = jnp.concatenate([_head(q_ref, j * g + t) for t in range(g)], axis=0)
        sink = None
        if use_sink:
            sink = jnp.concatenate([jnp.full((n, 1), sink_ref[j * g + t], F32) for t in range(g)], axis=0)
        problems.append((q, [_head(k_ref, j)], [_head(v_ref, j)], [None], [None], sink))
    outs = _attend(problems, HD ** -0.5)
    for j, o in enumerate(outs):
        for t in range(g):
            h = j * g + t
            o_ref[:, h * HD:(h + 1) * HD] = (o[t * n:(t + 1) * n] * _silu(_head(z_ref, h))).astype(BF16)
    slots = [kv_ref] if has_prev else [kv_ref.at[l] for l in range(kv_ref.shape[0])]
    for slot in slots:
        slot[0] = k_ref[...]
        slot[1] = v_ref[...]


def _ctx_attn(p, q_col, k_col, v_col, z_col, heads, kv_heads, sink, layers, layer, prev):
    wq, wkv = heads * HD, kv_heads * HD
    use_sink = sink is not None
    in_specs = [
        pl.BlockSpec((SEQ, wq), lambda b: (b, q_col)),
        pl.BlockSpec((SEQ, wkv), lambda b: (b, k_col)),
        pl.BlockSpec((SEQ, wkv), lambda b: (b, v_col)),
        pl.BlockSpec((SEQ, wq), lambda b: (b, z_col)),
    ]
    args = [p, p, p, p]
    if use_sink:
        in_specs.append(pl.BlockSpec(memory_space=pltpu.SMEM))
        args.append(sink)
    aliases = {}
    if prev is None:
        kv_spec = pl.BlockSpec((None, layers, 2, SEQ, wkv), lambda b: (b, 0, 0, 0, 0))
    else:
        in_specs.append(pl.BlockSpec(memory_space=pl.ANY))
        args.append(prev)
        aliases = {len(args) - 1: 1}
        kv_spec = pl.BlockSpec((None, None, 2, SEQ, wkv), lambda b: (b, layer, 0, 0, 0))
    return pl.pallas_call(
        functools.partial(_ctx_attn_kernel, heads=heads, kv_heads=kv_heads, use_sink=use_sink,
                          has_prev=prev is not None),
        grid=(BATCH,),
        in_specs=in_specs,
        out_specs=[pl.BlockSpec((SEQ, wq), lambda b: (b, 0)), kv_spec],
        out_shape=[jax.ShapeDtypeStruct((N_CTX, wq), BF16),
                   jax.ShapeDtypeStruct((BATCH, layers, 2, SEQ, wkv), F32)],
        input_output_aliases=aliases,
        compiler_params=_cparams(("arbitrary",)),
        name="ctx_attn_sink" if use_sink else "ctx_attn",
    )(*args)


def _rope_tables():
    half = HD // 4
    freq = (ROPE_THETA ** (-np.arange(half, dtype=np.float32) / half)).astype(np.float32)
    t = np.arange(DEC_SEQ)
    ang_r = (t // GRID_W).astype(np.float32)[:, None] * freq[None, :]
    ang_c = (t % GRID_W).astype(np.float32)[:, None] * freq[None, :]
    cos = np.concatenate([np.cos(ang_r)] * 2 + [np.cos(ang_c)] * 2, axis=1).astype(np.float32)
    sin_r, sin_c, zero = np.sin(ang_r), np.sin(ang_c), np.zeros_like(ang_r)
    s_up = np.concatenate([-sin_r, zero, -sin_c, zero], axis=1).astype(np.float32)
    s_dn = np.concatenate([zero, sin_r, zero, sin_c], axis=1).astype(np.float32)
    return jnp.asarray(cos), jnp.asarray(s_up), jnp.asarray(s_dn)


def _rope(x, cos, s_up, s_dn):
    return x * cos + pltpu.roll(x, HD - HD // 4, 1) * s_up + pltpu.roll(x, HD // 4, 1) * s_dn


def _win_attn_kernel(q_ref, k_ref, v_ref, kc_ref, vc_ref, z_ref, cos_ref, sup_ref, sdn_ref, sink_ref, o_ref):
    i = pl.program_id(1)
    g = H_B // HKV_B
    span = QBLK + 2 * WIN
    start = pl.multiple_of(jnp.clip(i * QBLK - WIN, 0, DEC_SEQ - span), QBLK)
    qrows = pl.ds(pl.multiple_of(i * QBLK, QBLK), QBLK)
    krows = pl.ds(start, span)
    cq, uq, dq = cos_ref[qrows, :], sup_ref[qrows, :], sdn_ref[qrows, :]
    ck, uk, dk = cos_ref[krows, :], sup_ref[krows, :], sdn_ref[krows, :]
    qpos = i * QBLK + lax.broadcasted_iota(jnp.int32, (g * QBLK, span), 0) % QBLK
    kpos = start + lax.broadcasted_iota(jnp.int32, (g * QBLK, span), 1)
    band = jnp.abs(qpos - kpos) <= WIN
    problems = []
    for j in range(HKV_B):
        q = jnp.concatenate([_rope(_head(q_ref, j * g + t), cq, uq, dq) for t in range(g)], axis=0)
        kw = _rope(_head(k_ref, j, krows), ck, uk, dk)
        vw = _head(v_ref, j, krows)
        sink = jnp.concatenate([jnp.full((QBLK, 1), sink_ref[j * g + t], F32) for t in range(g)], axis=0)
        problems.append((q, [kw, _head(kc_ref, j)], [vw, _head(vc_ref, j)], [band, None], [None, None], sink))
    outs = _attend(problems, HD ** -0.5)
    for j, o in enumerate(outs):
        for t in range(g):
            h = j * g + t
            o_ref[:, h * HD:(h + 1) * HD] = (o[t * QBLK:(t + 1) * QBLK] * _silu(_head(z_ref, h))).astype(BF16)


def _win_attn(p, cache_k, cache_v, sink, q_col, k_col, v_col, z_col):
    wq, wkv = H_B * HD, HKV_B * HD
    nq = DEC_SEQ // QBLK
    cos, s_up, s_dn = _rope_tables()
    full = pl.BlockSpec((DEC_SEQ, HD), lambda b, i: (0, 0))
    return pl.pallas_call(
        _win_attn_kernel,
        grid=(DEC_BATCH, nq),
        in_specs=[
            pl.BlockSpec((QBLK, wq), lambda b, i: (b * nq + i, q_col)),
            pl.BlockSpec((DEC_SEQ, wkv), lambda b, i: (b, k_col)),
            pl.BlockSpec((DEC_SEQ, wkv), lambda b, i: (b, v_col)),
            pl.BlockSpec((None, PAST_LEN, wkv), lambda b, i: (b, 0, 0)),
            pl.BlockSpec((None, PAST_LEN, wkv), lambda b, i: (b, 0, 0)),
            pl.BlockSpec((QBLK, wq), lambda b, i: (b * nq + i, z_col)),
            full, full, full,
            pl.BlockSpec(memory_space=pltpu.SMEM),
        ],
        out_specs=pl.BlockSpec((QBLK, wq), lambda b, i: (b * nq + i, 0)),
        out_shape=jax.ShapeDtypeStruct((N_LAT, wq), BF16),
        compiler_params=_cparams(("arbitrary", "arbitrary")),
        name="win_attn",
    )(p, p, p, cache_k, cache_v, p, cos, s_up, s_dn, sink)


def _nbr_onehot():
    qc = np.arange(GRID_W)[:, None]
    kc = np.arange(GRID_W)[None, :]
    idx = np.clip(kc - qc, -(NB_W - 1), NB_W - 1) + NB_W - 1
    e = (np.arange(2 * NB_W)[:, None, None] == idx[None]).astype(np.float32)
    return jnp.asarray(e.reshape(2 * NB_W, GRID_W * GRID_W))


def _bias_expand_kernel(r_ref, e_ref, o_ref):
    o_ref[...] = _dot_hi(r_ref[...], e_ref[...])


def _nbr_bias_table(rpb):
    rows = H_D * (2 * NB_H - 1)
    r = jnp.pad(rpb.reshape(rows, 2 * NB_W - 1), ((0, 128 - rows), (0, 1)))
    t = pl.pallas_call(
        _bias_expand_kernel,
        out_shape=jax.ShapeDtypeStruct((128, GRID_W * GRID_W), F32),
        name="nbr_bias_expand",
    )(r, _nbr_onehot())
    t = t[:rows].reshape(H_D, 2 * NB_H - 1, GRID_W, GRID_W)
    return jnp.concatenate([t[:, :-1], t[:, 1:]], axis=-1)


NBR_ROWS = 2


def _nbr_attn_kernel(q_ref, k_ref, v_ref, kc_ref, vc_ref, z_ref, t_ref, o_ref):
    rows = DEC_SEQ // GRID_W
    nk = NB_H * GRID_W
    qc = lax.broadcasted_iota(jnp.int32, (GRID_W, nk), 0)
    kc = lax.broadcasted_iota(jnp.int32, (GRID_W, nk), 1) % GRID_W
    cstart = jnp.clip(qc - NB_W // 2, 0, GRID_W - NB_W)
    ok = (kc >= cstart) & (kc < cstart + NB_W)
    problems = []
    for t in range(NBR_ROWS):
        r = pl.program_id(1) * NBR_ROWS + t
        rs = jnp.clip(r - NB_H // 2, 0, rows - NB_H)
        dr0 = rs - r + NB_H - 1
        krows = pl.ds(pl.multiple_of(rs * GRID_W, GRID_W), nk)
        qrows = slice(t * GRID_W, (t + 1) * GRID_W)
        for h in range(H_D):
            bias = jnp.concatenate([t_ref[h, dr0 + 2 * m] for m in range(nk // LANE)], axis=1)
            problems.append((_head(q_ref, h, qrows), [_head(k_ref, h, krows), _head(kc_ref, h)],
                             [_head(v_ref, h, krows), _head(vc_ref, h)], [ok, None], [bias, None], None))
    outs = _attend(problems, HD ** -0.5)
    for i, o in enumerate(outs):
        t, h = divmod(i, H_D)
        qrows = slice(t * GRID_W, (t + 1) * GRID_W)
        o_ref[qrows, h * HD:(h + 1) * HD] = (o * _silu(_head(z_ref, h, qrows))).astype(BF16)


def _nbr_attn(p, cache_k, cache_v, table, q_col, k_col, v_col, z_col):
    rows = DEC_SEQ // GRID_W // NBR_ROWS
    qb = NBR_ROWS * GRID_W
    return pl.pallas_call(
        _nbr_attn_kernel,
        grid=(DEC_BATCH, rows),
        in_specs=[
            pl.BlockSpec((qb, W_D), lambda b, r: (b * rows + r, q_col)),
            pl.BlockSpec((DEC_SEQ, W_D), lambda b, r: (b, k_col)),
            pl.BlockSpec((DEC_SEQ, W_D), lambda b, r: (b, v_col)),
            pl.BlockSpec((None, PAST_LEN, W_D), lambda b, r: (b, 0, 0)),
            pl.BlockSpec((None, PAST_LEN, W_D), lambda b, r: (b, 0, 0)),
            pl.BlockSpec((qb, W_D), lambda b, r: (b * rows + r, z_col)),
            pl.BlockSpec(table.shape, lambda b, r: (0, 0, 0, 0)),
        ],
        out_specs=pl.BlockSpec((qb, W_D), lambda b, r: (b * rows + r, 0)),
        out_shape=jax.ShapeDtypeStruct((N_LAT, W_D), BF16),
        compiler_params=_cparams(("arbitrary", "arbitrary")),
        name="nbr_attn",
    )(p, p, p, cache_k, cache_v, p, table)


def _lane_col(x, idx):
    lane = lax.broadcasted_iota(jnp.int32, x.shape, 1)
    return jnp.sum(jnp.where(lane == idx, x, 0.0), axis=-1, keepdims=True)


def _short_conv(x_ref, w_ref, pad_ref, n):
    pad = CONV_K // 2
    zeros = jnp.zeros((8, HD), F32)
    pad_ref[0:8, :] = zeros
    pad_ref[n + 8:n + 16, :] = zeros
    pad_ref[8:n + 8, :] = x_ref[...]
    y = functools.reduce(jnp.add, [pad_ref[8 - pad + t:8 - pad + t + n, :] * w_ref[t:t + 1, :]
                                   for t in range(CONV_K)])
    return _silu(y)


def _l2norm(x):
    return x * lax.rsqrt(jnp.sum(x * x, axis=-1, keepdims=True) + EPS)


def _split2(x):
    hi = x.astype(BF16)
    return hi, (x - hi.astype(F32)).astype(BF16)


def _dots_x2(pairs, split_b=True):
    ops = []
    for a, b in pairs:
        a_hi, a_lo = _split2(a)
        b2 = jnp.concatenate(_split2(b), axis=1) if split_b else b.astype(BF16)
        ops.append((jnp.concatenate([a_hi, a_lo], axis=0), b2))
    rs = [jnp.dot(a2, b2, preferred_element_type=F32) for a2, b2 in ops]
    out = []
    for (a, b), r in zip(pairs, rs):
        m, n = a.shape[0], b.shape[1]
        if split_b:
            out.append((r[m:, :n] + r[:m, n:] + r[m:, n:]) + r[:m, :n])
        else:
            out.append(r[m:] + r[:m])
    return out


def _pair_masks():
    c = lax.broadcasted_iota(jnp.int32, (CHUNK, HD), 0)
    l = lax.broadcasted_iota(jnp.int32, (CHUNK, HD), 1)
    left = l < CHUNK
    j = l % CHUNK
    ahead = jnp.where(left, j - c, c - j)
    return left, ahead <= 0, ahead < 0, j == c


def _block_diag(x, left):
    return jnp.concatenate([jnp.where(left, x, 0.0), jnp.where(left, 0.0, x)], axis=0)


def _inv_unit_triangular_pairs(lmats, left, eye):
    mps = [-x for x in lmats]
    ps = [jnp.where(eye, 1.0, 0.0) + m for m in mps]
    mps = _dots_x2([(m, _block_diag(m, left)) for m in mps])
    for _ in range(4):
        rs = _dots_x2([(jnp.concatenate([p, m], axis=0), _block_diag(m, left)) for p, m in zip(ps, mps)])
        ps = [p + r[:CHUNK] for p, r in zip(ps, rs)]
        mps = [r[CHUNK:] for r in rs]
    rs = _dots_x2([(p, _block_diag(m, left)) for p, m in zip(ps, mps)])
    return [p + r for p, r in zip(ps, rs)]


def _chunk_cumsum(x, reverse):
    n = x.shape[0]
    pos = lax.broadcasted_iota(jnp.int32, x.shape, 0) % CHUNK
    k = 1
    while k < CHUNK:
        if reverse:
            x = x + jnp.where(pos < CHUNK - k, pltpu.roll(x, n - k, 0), 0.0)
        else:
            x = x + jnp.where(pos >= k, pltpu.roll(x, k, 0), 0.0)
        k *= 2
    return x


def _delta_prep(items):
    left, tri, strict, eye = _pair_masks()
    zeros = jnp.zeros((CHUNK, HD), F32)
    pre = []
    for q, k, v, cols in items:
        b_f, b_b, g_f, g_b = (cols[:, t:t + 1] for t in range(4))
        gsel = jnp.where(left, g_f, g_b)
        g_row = jnp.sum(jnp.where(eye, gsel, 0.0), axis=0, keepdims=True)
        decay = jnp.exp(jnp.where(tri, gsel - g_row, NEG))
        kb_f, kb_b = k * b_f, k * b_b
        lhs = jnp.concatenate([jnp.concatenate([kb_f, kb_b], axis=1), jnp.concatenate([q, q], axis=1)], axis=0)
        rhs = jnp.concatenate([jnp.concatenate([k, zeros], axis=1), jnp.concatenate([zeros, k], axis=1)], axis=0)
        pre.append((decay, kb_f, kb_b, g_f, g_b, b_f, b_b, lhs, rhs))
    kqs = [_dot_nt(x[7], x[8]) for x in pre]
    lmats = [jnp.where(strict, kq[:CHUNK] * x[0], 0.0) for kq, x in zip(kqs, pre)]
    attns = [jnp.where(tri, kq[CHUNK:] * x[0], 0.0) for kq, x in zip(kqs, pre)]
    tinvs = _inv_unit_triangular_pairs(lmats, left, eye)
    egs, rhs = [], []
    for (q, k, v, cols), x in zip(items, pre):
        _, kb_f, kb_b, g_f, g_b, b_f, b_b = x[:7]
        eg_f = jnp.exp(jnp.broadcast_to(g_f, (CHUNK, HD)))
        eg_b = jnp.exp(jnp.broadcast_to(g_b, (CHUNK, HD)))
        egs.append((eg_f, eg_b))
        rhs.append(jnp.concatenate([jnp.concatenate([v * b_f, kb_f * eg_f, zeros, zeros], axis=1),
                                    jnp.concatenate([zeros, zeros, v * b_b, kb_b * eg_b], axis=1)], axis=0))
    uws = _dots_x2(list(zip(tinvs, rhs)), split_b=False)
    out = []
    for (q, k, v, cols), x, attn, uw, (eg_f, eg_b) in zip(items, pre, attns, uws, egs):
        g_f, g_b = x[3], x[4]
        gl_f = jnp.broadcast_to(g_f[CHUNK - 1:CHUNK, :], (1, HD))
        gl_b = jnp.broadcast_to(g_b[0:1, :], (1, HD))
        kg_t = jnp.concatenate([k * jnp.exp(gl_f - g_f), k * jnp.exp(gl_b - g_b)], axis=1).T
        out.append(dict(u=[uw[:, :HD], uw[:, 2 * HD:3 * HD]], w=[uw[:, HD:2 * HD], uw[:, 3 * HD:]],
                        attn=[attn[:, :CHUNK], attn[:, CHUNK:]], qg=[q * eg_f, q * eg_b],
                        kg_t=[kg_t[:HD], kg_t[HD:]], eg=[jnp.exp(gl_f), jnp.exp(gl_b)]))
    return out


def _delta_kernel(*refs, n, hb, cg, has_s0, has_prev):
    qp_ref, kp_ref, vp_ref, z_ref, sm_ref, cq_ref, ck_ref, cv_ref, al_ref, dt_ref, on_ref = refs[:11]
    rest = refs[11 + int(has_prev):]
    if has_s0:
        s0_ref, o_ref = rest[:2]
        so_ref = None
    else:
        o_ref, so_ref = rest[:2]
    q_s, k_s, v_s, pad_s, col_s, u_s, w_s, at_s, qg_s, kg_s, eg_s, acc_s, st_s, gate_s = rest[2:]
    h0 = pl.program_id(1) * hb
    nc = n // CHUNK

    @pl.when(pl.program_id(1) == 0)
    def _():
        sm = sm_ref[...]
        x = sm + dt_ref[...]
        softplus = jnp.maximum(x, 0.0) + jnp.log(1.0 + jnp.exp(-jnp.abs(x)))
        gates = -jnp.exp(al_ref[...]) * softplus
        gate_s[0] = _sigmoid(sm)
        gate_s[1] = _chunk_cumsum(gates, False)
        gate_s[2] = _chunk_cumsum(gates, True)

    betas, cum_f, cum_b = gate_s[0], gate_s[1], gate_s[2]
    lane = lax.broadcasted_iota(jnp.int32, (n, HD), 1)
    for j in range(hb):
        cols = slice(j * HD, (j + 1) * HD)
        q_s[j] = _l2norm(_short_conv(qp_ref.at[:, cols], cq_ref.at[:, cols], pad_s, n)) * HD ** -0.5
        k_s[j] = _l2norm(_short_conv(kp_ref.at[:, cols], ck_ref.at[:, cols], pad_s, n))
        v_s[j] = _short_conv(vp_ref.at[:, cols], cv_ref.at[:, cols], pad_s, n)
        col_s[j] = jnp.where(lane == 0, _lane_col(betas, h0 + j),
                             jnp.where(lane == 1, _lane_col(betas, H_A + h0 + j),
                                       jnp.where(lane == 2, _lane_col(cum_f, 2 * H_A + h0 + j),
                                                 _lane_col(cum_b, 3 * H_A + h0 + j))))
    acc_s[...] = jnp.zeros((hb, n, HD), F32)
    if has_s0:
        st_s[...] = s0_ref[...]
    else:
        st_s[...] = jnp.zeros((2, hb, HD, HD), F32)

    def prep(ci, carry):
        where = [(j, ci * cg + t) for j in range(hb) for t in range(cg)]
        rows = [pl.ds(pl.multiple_of(c * CHUNK, CHUNK), CHUNK) for _, c in where]
        outs = _delta_prep([(q_s[j, r, :], k_s[j, r, :], v_s[j, r, :], col_s[j, r, :])
                            for (j, _), r in zip(where, rows)])
        for (j, c), r, o in zip(where, rows, outs):
            for d in range(2):
                u_s[d, j, r, :] = o["u"][d]
                w_s[d, j, r, :] = o["w"][d].astype(BF16)
                at_s[d, j, r, :] = o["attn"][d].astype(BF16)
                qg_s[d, j, r, :] = o["qg"][d].astype(BF16)
                kg_s[d, j, c] = o["kg_t"][d].astype(BF16)
                eg_s[d, j, c] = jnp.broadcast_to(o["eg"][d], (8, HD))
        return carry

    lax.fori_loop(0, nc // cg, prep, 0)

    def scan(i, carry):
        chains = [(d, j, (nc - 1 - i) if d else i) for d in range(2) for j in range(hb)]
        rows = [pl.ds(pl.multiple_of(c * CHUNK, CHUNK), CHUNK) for _, _, c in chains]
        ss = [st_s[d, j] for d, j, _ in chains]
        sbs = [s.astype(BF16) for s in ss]
        ws = [jnp.dot(w_s[d, j, r, :], sb, preferred_element_type=F32) for (d, j, _), r, sb in zip(chains, rows, sbs)]
        vns = [(u_s[d, j, r, :] - w).astype(BF16) for (d, j, _), r, w in zip(chains, rows, ws)]
        for (d, j, c), r, s, sb, vn in zip(chains, rows, ss, sbs, vns):
            st_s[d, j] = s * eg_s[d, j, c, 0:1, :] + jnp.dot(kg_s[d, j, c], vn, preferred_element_type=F32)
        for (d, j, c), r, sb, vn in zip(chains, rows, sbs, vns):
            acc_s[j, r, :] += (jnp.dot(qg_s[d, j, r, :], sb, preferred_element_type=F32)
                               + jnp.dot(at_s[d, j, r, :], vn, preferred_element_type=F32))
        return carry

    lax.fori_loop(0, nc, scan, 0)
    for j in range(hb):
        cols = slice(j * HD, (j + 1) * HD)
        o = acc_s[j]
        o = o * lax.rsqrt(jnp.mean(o * o, axis=-1, keepdims=True) + EPS) * on_ref[...]
        o_ref[:, cols] = (o * _silu(z_ref[:, cols])).astype(BF16)
    if so_ref is not None:
        _write_layer_slots(so_ref, st_s[...], has_prev)


def _write_layer_slots(ref, value, has_prev):
    if has_prev:
        ref[...] = value
    else:
        for l in range(ref.shape[0]):
            ref[l] = value


def _delta_mixer(p, small, conv_w, a_row, dt_row, onorm, s0, seq, nb, hb, cg, layers=1, layer=0, prev=None):
    has_s0 = s0 is not None
    wb = hb * HD
    ng = H_A // hb
    nc = seq // CHUNK
    col = lambda off: (lambda b, h: (b, off + h))
    in_specs = [
        pl.BlockSpec((seq, wb), col(0)),
        pl.BlockSpec((seq, wb), col(ng)),
        pl.BlockSpec((seq, wb), col(2 * ng)),
        pl.BlockSpec((seq, wb), col(3 * ng)),
        pl.BlockSpec((seq, LANE), lambda b, h: (b, 0)),
        pl.BlockSpec((CONV_K, wb), lambda b, h: (0, h)),
        pl.BlockSpec((CONV_K, wb), lambda b, h: (0, ng + h)),
        pl.BlockSpec((CONV_K, wb), lambda b, h: (0, 2 * ng + h)),
        pl.BlockSpec((1, LANE), lambda b, h: (0, 0)),
        pl.BlockSpec((1, LANE), lambda b, h: (0, 0)),
        pl.BlockSpec((1, HD), lambda b, h: (0, 0)),
    ]
    args = [p, p, p, p, small, conv_w, conv_w, conv_w, a_row, dt_row, onorm.reshape(1, HD)]
    if has_s0:
        in_specs.append(pl.BlockSpec((None, 2, hb, HD, HD), lambda b, h: (b, 0, h, 0, 0)))
        args.append(s0)
    out_specs = [pl.BlockSpec((seq, wb), lambda b, h: (b, h))]
    out_shape = [jax.ShapeDtypeStruct((nb * seq, W_A), BF16)]
    aliases = {}
    if not has_s0:
        if prev is None:
            out_specs.append(pl.BlockSpec((None, layers, 2, hb, HD, HD), lambda b, h: (b, 0, 0, h, 0, 0)))
        else:
            in_specs.append(pl.BlockSpec(memory_space=pl.ANY))
            args.append(prev)
            aliases = {len(args) - 1: 1}
            out_specs.append(pl.BlockSpec((None, None, 2, hb, HD, HD), lambda b, h: (b, layer, 0, h, 0, 0)))
        out_shape.append(jax.ShapeDtypeStruct((nb, layers, 2, H_A, HD, HD), F32))
    return pl.pallas_call(
        functools.partial(_delta_kernel, n=seq, hb=hb, cg=cg, has_s0=has_s0, has_prev=prev is not None),
        grid=(nb, ng),
        in_specs=in_specs,
        out_specs=out_specs,
        out_shape=out_shape,
        input_output_aliases=aliases,
        scratch_shapes=[
            pltpu.VMEM((hb, seq, HD), F32), pltpu.VMEM((hb, seq, HD), F32), pltpu.VMEM((hb, seq, HD), F32),
            pltpu.VMEM((seq + 16, HD), F32), pltpu.VMEM((hb, seq, HD), F32),
            pltpu.VMEM((2, hb, seq, HD), F32), pltpu.VMEM((2, hb, seq, HD), BF16),
            pltpu.VMEM((2, hb, seq, CHUNK), BF16), pltpu.VMEM((2, hb, seq, HD), BF16),
            pltpu.VMEM((2, hb, nc, HD, CHUNK), BF16), pltpu.VMEM((2, hb, nc, 8, HD), F32),
            pltpu.VMEM((hb, seq, HD), F32), pltpu.VMEM((2, hb, HD, HD), F32),
            pltpu.VMEM((3, seq, LANE), F32)],
        compiler_params=_cparams(("arbitrary", "arbitrary")),
        name="delta_lat" if has_s0 else "delta_ctx",
    )(*args)


_GLA_LEVELS = (32, 16, 8, 4, 2, 1)
GLA_CHUNKS = 4


def _gla_consts(reverse):
    r = np.arange(CHUNK)
    flip = (lambda a: a[::-1, ::-1]) if reverse else (lambda a: a)
    sel, hi, pair = [], [], []
    for m in _GLA_LEVELS:
        mid = (r // (2 * m)) * (2 * m) + m
        is_hi = r >= mid
        sel.append(flip(r[None, :] == mid[:, None] - 1))
        hi.append(flip(is_hi[:, None]))
        pair.append(flip((r[:, None] // (2 * m) == r[None, :] // (2 * m)) & is_hi[:, None] & ~is_hi[None, :]))
    out = [np.concatenate(sel), np.concatenate(hi), np.stack(pair)]
    out = [jnp.asarray(np.ascontiguousarray(a).astype(np.float32)) for a in out]
    return [out[0].astype(BF16), out[1], out[2]]


def _split3(x):
    hi = x.astype(BF16)
    r1 = x - hi.astype(F32)
    mid = r1.astype(BF16)
    return hi, mid, (r1 - mid.astype(F32)).astype(BF16)


def _gla_prep(items):
    nl = len(_GLA_LEVELS)
    r3s = [jnp.dot(consts[0][...], jnp.concatenate(_split3(b), axis=1), preferred_element_type=F32)
           for _, _, _, b, consts, _ in items]
    refs = [(r3[:, 2 * DK_C:] + r3[:, DK_C:2 * DK_C]) + r3[:, :DK_C] for r3 in r3s]
    c = lax.broadcasted_iota(jnp.int32, (CHUNK, CHUNK), 0)
    j = lax.broadcasted_iota(jnp.int32, (CHUNK, CHUNK), 1)
    lvl = []
    for (q, k, v, b, consts, _), ref in zip(items, refs):
        ops = []
        for lv in range(nl):
            rows = slice(lv * CHUNK, (lv + 1) * CHUNK)
            hi = consts[1][rows, :] > 0.5
            t = b - ref[rows]
            e = jnp.exp(jnp.where(hi, t, -t))
            ops.append((jnp.where(hi, q * e, 0.0), jnp.where(hi, 0.0, k * e)))
        lvl.append(ops)
    prods = [[_dot_nt(ql, kl) for ql, kl in ops] for ops in lvl]
    amats = []
    for (q, k, v, b, consts, _), pr in zip(items, prods):
        a = jnp.where(c == j, jnp.sum(q * k, axis=-1, keepdims=True), 0.0)
        for lv in range(nl):
            a = a + pr[lv] * consts[2][lv]
        amats.append(a)
    intras = [_dot(a, it[2]) for a, it in zip(amats, items)]
    r128 = lax.broadcasted_iota(jnp.int32, (DK_C, DK_C), 0)
    c128 = lax.broadcasted_iota(jnp.int32, (DK_C, DK_C), 1)
    out = []
    for (q, k, v, b, consts, reverse), intra in zip(items, intras):
        last = 0 if reverse else CHUNK - 1
        bl = b[last:last + 1, :]
        dec = jnp.sum(jnp.where(r128 == c128, jnp.broadcast_to(jnp.exp(bl), (DK_C, DK_C)), 0.0),
                      axis=-1, keepdims=True)
        out.append((intra, q * jnp.exp(b), dec, _dot_tn(k * jnp.exp(bl - b), v)))
    return out


def _gla_kernel(*refs, n, cg, has_s0, has_prev):
    consts_f, consts_b = refs[8:11], refs[11:14]
    q_ref, k_ref, v_ref, z_ref, sm_ref, wg_ref, bg_ref, on_ref = refs[:8]
    if has_s0:
        s0_ref, o_ref, gk_s, acc_s, st_s = refs[14:]
        so_ref = None
    else:
        o_ref, so_ref, gk_s, acc_s, st_s = refs[14 + int(has_prev):]
    nc = n // CHUNK
    sm = sm_ref[...]
    for d in range(2):
        x = _dot_hi(sm, wg_ref[d]) + bg_ref[d]
        gk = (jnp.minimum(x, 0.0) - jnp.log(1.0 + jnp.exp(-jnp.abs(x)))) / GLA_TAU
        gk_s[d] = _chunk_cumsum(gk, bool(d))
    acc_s[...] = jnp.zeros((n, DV_C), F32)
    if has_s0:
        st_s[...] = s0_ref[...]
    else:
        st_s[...] = jnp.zeros((2, DK_C, DV_C), F32)

    def body(i, carry):
        where = [(d, (nc - 1 - (i * cg + t)) if d else (i * cg + t)) for d in range(2) for t in range(cg)]
        rows = [pl.ds(pl.multiple_of(c * CHUNK, CHUNK), CHUNK) for _, c in where]
        outs = _gla_prep([(q_ref[r, :] * DK_C ** -0.5, k_ref[r, :], v_ref[r, :], gk_s[d, r, :],
                           consts_b if d else consts_f, bool(d)) for (d, _), r in zip(where, rows)])
        for d in range(2):
            s = st_s[d]
            for t in range(cg):
                intra, qe, dec, kv = outs[d * cg + t]
                acc_s[rows[d * cg + t], :] += intra + _dot(qe, s)
                s = s * dec + kv
            st_s[d] = s
        return carry

    lax.fori_loop(0, nc // cg, body, 0)
    o = acc_s[...]
    o = o * lax.rsqrt(jnp.mean(o * o, axis=-1, keepdims=True) + EPS) * on_ref[...]
    o_ref[...] = (o * _silu(z_ref[...])).astype(BF16)
    if so_ref is not None:
        _write_layer_slots(so_ref, st_s[...], has_prev)


def _gla_mixer(p, small, w_gate, b_gate, onorm, s0, seq, nb, layers=1, layer=0, prev=None):
    has_s0 = s0 is not None
    consts = _gla_consts(False) + _gla_consts(True)
    const_specs = [pl.BlockSpec(a.shape, (lambda b, h, nd=a.ndim: (0,) * nd)) for a in consts]
    in_specs = [
        pl.BlockSpec((seq, DK_C), lambda b, h: (b, h)),
        pl.BlockSpec((seq, DK_C), lambda b, h: (b, QK_C // DK_C + h)),
        pl.BlockSpec((seq, DV_C), lambda b, h: (b, 2 * QK_C // DV_C + h)),
        pl.BlockSpec((seq, DV_C), lambda b, h: (b, (2 * QK_C + W_C) // DV_C + h)),
        pl.BlockSpec((seq, LANE), lambda b, h: (b, 0)),
        pl.BlockSpec((2, LANE, DK_C), lambda b, h: (0, 0, h)),
        pl.BlockSpec((2, 1, DK_C), lambda b, h: (0, 0, h)),
        pl.BlockSpec((1, DV_C), lambda b, h: (0, 0)),
    ] + const_specs
    args = [p, p, p, p, small, w_gate, b_gate.reshape(2, 1, QK_C), onorm.reshape(1, DV_C)] + consts
    if has_s0:
        in_specs.append(pl.BlockSpec((None, 2, None, DK_C, DV_C), lambda b, h: (b, 0, h, 0, 0)))
        args.append(s0)
    out_specs = [pl.BlockSpec((seq, DV_C), lambda b, h: (b, h))]
    out_shape = [jax.ShapeDtypeStruct((nb * seq, W_C), BF16)]
    aliases = {}
    if not has_s0:
        if prev is None:
            out_specs.append(pl.BlockSpec((None, layers, 2, None, DK_C, DV_C), lambda b, h: (b, 0, 0, h, 0, 0)))
        else:
            in_specs.append(pl.BlockSpec(memory_space=pl.ANY))
            args.append(prev)
            aliases = {len(args) - 1: 1}
            out_specs.append(pl.BlockSpec((None, None, 2, None, DK_C, DV_C), lambda b, h: (b, layer, 0, h, 0, 0)))
        out_shape.append(jax.ShapeDtypeStruct((nb, layers, 2, H_C, DK_C, DV_C), F32))
    return pl.pallas_call(
        functools.partial(_gla_kernel, n=seq, cg=GLA_CHUNKS, has_s0=has_s0, has_prev=prev is not None),
        grid=(nb, H_C),
        in_specs=in_specs,
        out_specs=out_specs,
        out_shape=out_shape,
        input_output_aliases=aliases,
        scratch_shapes=[pltpu.VMEM((2, seq, DK_C), F32), pltpu.VMEM((seq, DV_C), F32),
                        pltpu.VMEM((2, DK_C, DV_C), F32)],
        compiler_params=_cparams(("arbitrary", "arbitrary")),
        name="gla_lat" if has_s0 else "gla_ctx",
    )(*args)


_EV_TILES = tuple(range(4 * W_A // TN_IN)) + (8, 9, 11, 12, 10)
_EV_ALIGNED = 4 * W_A // TN_IN
_EV_QB = 4 * W_A
_EV_ZB = _EV_QB + W_B
_EV_KB = _EV_ZB + W_B
_EV_VB = _EV_KB + HKV_B * HD
_OD_ALIGNED = (2 * QK_C + 2 * W_C) // TN_IN
_OD_TILES = tuple(range((P_ODD - 2 * GLA_RANK) // TN_IN))
_OD_QD = 2 * QK_C + 2 * W_C
_OD_KD = _OD_QD + W_D
_OD_VD = _OD_KD + W_D
_OD_ZD = _OD_VD + W_D


def _lane_row(v, offset):
    return jnp.pad(v.reshape(1, -1), ((0, 0), (offset, LANE - offset - v.size)))


def _even_layer(xc, xl, e, mod, norm_w, w_in, conv_a, a_log, dt_bias, onorm, sink, w_out, state_delta, cache_kv,
                final_w, prev_st, prev_kv):
    pc, sc = _in_proj(xc, norm_w, mod, w_in, e, _EV_TILES, _EV_ALIGNED, 4 * W_A, False)
    pq, sq = _in_proj(xl, norm_w, mod, w_in, e, _EV_TILES, _EV_ALIGNED, 4 * W_A, True)
    a_row = _lane_row(a_log, 2 * H_A)
    dt_row = _lane_row(dt_bias, 2 * H_A)
    oa_c, st = _delta_mixer(pc, sc, conv_a, a_row, dt_row, onorm, None, SEQ, BATCH, 8, 1, N_EVEN, e, prev_st)
    (oa_l,) = _delta_mixer(pq, sq, conv_a, a_row, dt_row, onorm, state_delta[:, e], DEC_SEQ, DEC_BATCH, 4, 2)
    wkv = HKV_B * HD
    ob_c, kv = _ctx_attn(pc, _EV_QB // W_B, _EV_KB // wkv, _EV_VB // wkv, _EV_ZB // W_B, H_B, HKV_B, sink,
                         N_EVEN, e, prev_kv)
    ck = cache_kv[:, e, 0].reshape(DEC_BATCH, PAST_LEN, wkv)
    cv = cache_kv[:, e, 1].reshape(DEC_BATCH, PAST_LEN, wkv)
    ob_l = _win_attn(pq, ck, cv, sink, _EV_QB // W_B, _EV_KB // wkv, _EV_VB // wkv, _EV_ZB // W_B)
    xc = _out_proj(oa_c, ob_c, w_out, e, xc, mod, False, final_w)
    xl = _out_proj(oa_l, ob_l, w_out, e, xl, mod, True, final_w)
    return xc, xl, st, kv


def _odd_layer(xc, xl, o_i, mod, norm_w, w_in, w_glr, b_glr, onorm, rpb, w_out, state_gla, cache_kv, final_w,
               prev_st, prev_kv):
    lo = 2 * QK_C + 2 * W_C
    pc, sc = _in_proj(xc, norm_w, mod, w_in, o_i, _OD_TILES, _OD_ALIGNED, lo, False)
    pq, sq = _in_proj(xl, norm_w, mod, w_in, o_i, _OD_TILES, _OD_ALIGNED, lo, True)
    w_gate = jnp.stack([jnp.pad(w_glr[0], ((0, LANE - GLA_RANK), (0, 0))),
                        jnp.pad(w_glr[1], ((GLA_RANK, LANE - 2 * GLA_RANK), (0, 0)))])
    oc_c, st = _gla_mixer(pc, sc, w_gate, b_glr, onorm, None, SEQ, BATCH, N_ODD, o_i, prev_st)
    (oc_l,) = _gla_mixer(pq, sq, w_gate, b_glr, onorm, state_gla[:, o_i], DEC_SEQ, DEC_BATCH)
    od_c, kv = _ctx_attn(pc, _OD_QD // W_D, _OD_KD // W_D, _OD_VD // W_D, _OD_ZD // W_D, H_D, H_D, None,
                         N_ODD, o_i, prev_kv)
    ck = cache_kv[:, o_i, 0].reshape(DEC_BATCH, PAST_LEN, W_D)
    cv = cache_kv[:, o_i, 1].reshape(DEC_BATCH, PAST_LEN, W_D)
    od_l = _nbr_attn(pq, ck, cv, _nbr_bias_table(rpb), _OD_QD // W_D, _OD_KD // W_D, _OD_VD // W_D, _OD_ZD // W_D)
    xc = _out_proj(oc_c, od_c, w_out, o_i, xc, mod, False, final_w)
    xl = _out_proj(oc_l, od_l, w_out, o_i, xl, mod, True, final_w)
    return xc, xl, st, kv


def kernel(x_prompt, x_sample, state_delta, cache_kv_win, state_gla, cache_kv_nbr, c, c_ctx, norm_w, w_ada, b_ada, w_in_even, conv_a, a_log_a, dt_bias_a, onorm_a, sink_b, w_out_even, w_in_odd, w_glr_c, b_glr_c, onorm_c, rpb_d, w_out_odd, final_norm_w):
    xc = x_prompt.reshape(N_CTX, D_MODEL)
    xl = x_sample.reshape(N_LAT, D_MODEL)
    cond = jnp.concatenate([c_ctx[None, :], c, jnp.zeros((N_COND - 1 - DEC_BATCH, D_MODEL), F32)], axis=0)
    mods = _ada_mod(cond, w_ada, b_ada).reshape(DEPTH, N_COND, 1, 3 * D_MODEL)
    wo_even, wo_odd = w_out_even.astype(BF16), w_out_odd.astype(BF16)
    w_in_even, w_in_odd = jnp.swapaxes(w_in_even, 1, 2), jnp.swapaxes(w_in_odd, 1, 2)
    new_delta = new_kvw = new_gla = new_kvn = None
    for li in range(DEPTH):
        final_w = final_norm_w if li == DEPTH - 1 else None
        if li % 2 == 0:
            e = li // 2
            xc, xl, new_delta, new_kvw = _even_layer(xc, xl, e, mods[li], norm_w[li], w_in_even, conv_a[e], a_log_a[e],
                                                     dt_bias_a[e], onorm_a[e], sink_b[e], wo_even, state_delta,
                                                     cache_kv_win, final_w, new_delta, new_kvw)
        else:
            o_i = li // 2
            xc, xl, new_gla, new_kvn = _odd_layer(xc, xl, o_i, mods[li], norm_w[li], w_in_odd, w_glr_c[o_i],
                                                  b_glr_c[o_i], onorm_c[o_i], rpb_d[o_i], wo_odd, state_gla,
                                                  cache_kv_nbr, final_w, new_gla, new_kvn)
    return (xc.reshape(BATCH, SEQ, D_MODEL), xl.reshape(DEC_BATCH, DEC_SEQ, D_MODEL),
            new_delta, new_kvw.reshape(BATCH, N_EVEN, 2, SEQ, HKV_B, HD),
            new_gla, new_kvn.reshape(BATCH, N_ODD, 2, SEQ, H_D, HD))
```

```python
import functools

import numpy as np
import jax
import jax.numpy as jnp
from jax import lax
from jax.experimental import pallas as pl
from jax.experimental.pallas import tpu as pltpu

F32 = jnp.float32
BF16 = jnp.bfloat16
HIGHEST = lax.Precision.HIGHEST

D_MODEL = 2048
BATCH = 16
SEQ = 256
DEPTH = 4
DEC_BATCH = 4
DEC_SEQ = 1024
PAST_LEN = 256
GRID_W = 64
HD = 128
EPS = 1e-6
NEG = -1e30
ROPE_THETA = 10000.0
CHUNK = 64
H_A = 8
W_A = H_A * HD
CONV_K = 5
H_B = 8
HKV_B = 2
W_B = H_B * HD
WIN = 128
QBLK = 128
H_C = 4
DK_C = 128
DV_C = 256
QK_C = H_C * DK_C
W_C = H_C * DV_C
GLA_RANK = 16
GLA_TAU = 16.0
H_D = 8
W_D = H_D * HD
NB_H = 8
NB_W = 16
N_EVEN = (DEPTH + 1) // 2
N_ODD = DEPTH // 2
PA_EVEN = 4 * W_A + 4 * H_A
P_EVEN = PA_EVEN + 2 * W_B + 2 * HKV_B * HD
PC_ODD = 2 * QK_C + 2 * W_C + 2 * GLA_RANK
P_ODD = PC_ODD + 4 * W_D

N_CTX = BATCH * SEQ
N_LAT = DEC_BATCH * DEC_SEQ
N_GRP = N_CTX
assert N_LAT == N_GRP
N_COND = 8
LANE = 128
TM_IN = 1024
TN_IN = 512
GATE_COLS = 32
TM_OUT = 512
TN_ADA = 2048
VMEM_LIMIT = 60000 * 1024


def _cparams(sem):
    return pltpu.CompilerParams(dimension_semantics=sem, vmem_limit_bytes=VMEM_LIMIT)


def _sigmoid(x):
    return 1.0 / (1.0 + jnp.exp(-x))


def _silu(x):
    return x * _sigmoid(x)


def _dot(a, b):
    return jnp.dot(a.astype(BF16), b.astype(BF16), preferred_element_type=F32)


def _dot_nt(a, b):
    return lax.dot_general(a.astype(BF16), b.astype(BF16), (((1,), (1,)), ((), ())),
                           preferred_element_type=F32)


def _dot_tn(a, b):
    return lax.dot_general(a.astype(BF16), b.astype(BF16), (((0,), (0,)), ((), ())),
                           preferred_element_type=F32)


def _dot_hi(a, b):
    return jnp.dot(a, b, precision=HIGHEST, preferred_element_type=F32)


def _ada_kernel(c_ref, w_ref, b_ref, o_ref):
    o_ref[...] = _dot(_silu(c_ref[...]), w_ref[...]) + b_ref[...]


def _ada_mod(cond, w_ada, b_ada):
    n3 = 3 * D_MODEL
    return pl.pallas_call(
        _ada_kernel,
        grid=(DEPTH, n3 // TN_ADA),
        in_specs=[
            pl.BlockSpec((N_COND, D_MODEL), lambda l, j: (0, 0)),
            pl.BlockSpec((None, D_MODEL, TN_ADA), lambda l, j: (l, 0, j)),
            pl.BlockSpec((None, 1, TN_ADA), lambda l, j: (l, 0, j)),
        ],
        out_specs=pl.BlockSpec((None, N_COND, TN_ADA), lambda l, j: (l, 0, j)),
        out_shape=jax.ShapeDtypeStruct((DEPTH, N_COND, n3), F32),
        compiler_params=_cparams(("arbitrary", "arbitrary")),
        name="ada_mod",
    )(cond, w_ada, b_ada.reshape(DEPTH, 1, n3))


def _cond_row(i, tm, latent):
    return 1 + i // (DEC_SEQ // tm) if latent else 0


def _inproj_kernel(src_ref, x_ref, nw_ref, shift_ref, scale_ref, wa_ref, wb_ref, ws_ref, o_ref, os_ref,
                   h_ref, w_ref, *, n_aligned):
    j, i = pl.program_id(0), pl.program_id(1)
    rows = pl.ds(pl.multiple_of(i * TM_IN, TM_IN), TM_IN)

    @pl.when(j == 0)
    def _():
        x = x_ref[...]
        gain = nw_ref[...] * (1.0 + scale_ref[...])
        h = (x * lax.rsqrt(jnp.mean(x * x, axis=-1, keepdims=True) + EPS) * gain + shift_ref[...]).astype(BF16)
        h_ref[rows, :] = h
        os_ref[...] = _dot_nt(h, ws_ref[...])

    @pl.when((i == 0) & (j < n_aligned))
    def _():
        w_ref[...] = wa_ref[...].astype(BF16)

    @pl.when((i == 0) & (j >= n_aligned))
    def _():
        w_ref[0:TN_IN - GATE_COLS, :] = wa_ref[GATE_COLS:TN_IN, :].astype(BF16)
        w_ref[TN_IN - GATE_COLS:TN_IN, :] = wb_ref[0:GATE_COLS, :].astype(BF16)

    o_ref[...] = _dot_nt(h_ref[rows, :], w_ref[...])


def _in_proj(x, norm_w, mod, w_t, layer, src_tiles, n_aligned, gate_col, latent):
    nj, ni = len(src_tiles), N_GRP // TM_IN
    sub = TN_IN // LANE
    row = functools.partial(_cond_row, tm=TM_IN, latent=latent)
    tok = lambda j, i, s: (jnp.where(j == 0, i, ni - 1), 0)
    grid_spec = pltpu.PrefetchScalarGridSpec(
        num_scalar_prefetch=1,
        grid=(nj, ni),
        in_specs=[
            pl.BlockSpec((TM_IN, D_MODEL), tok),
            pl.BlockSpec((1, D_MODEL), lambda j, i, s: (0, 0)),
            pl.BlockSpec((None, 1, D_MODEL), lambda j, i, s: (row(jnp.where(j == 0, i, ni - 1)), 0, 0)),
            pl.BlockSpec((None, 1, D_MODEL), lambda j, i, s: (row(jnp.where(j == 0, i, ni - 1)), 0, 1)),
            pl.BlockSpec((None, TN_IN, D_MODEL), lambda j, i, s: (layer, s[j], 0)),
            pl.BlockSpec((None, LANE, D_MODEL), lambda j, i, s: (layer, (s[j] + 1) * sub, 0)),
            pl.BlockSpec((None, LANE, D_MODEL), lambda j, i, s: (layer, gate_col // LANE, 0)),
        ],
        out_specs=[
            pl.BlockSpec((TM_IN, TN_IN), lambda j, i, s: (i, j)),
            pl.BlockSpec((TM_IN, LANE), tok),
        ],
        scratch_shapes=[pltpu.VMEM((N_GRP, D_MODEL), BF16), pltpu.VMEM((TN_IN, D_MODEL), BF16)],
    )
    return pl.pallas_call(
        functools.partial(_inproj_kernel, n_aligned=n_aligned),
        grid_spec=grid_spec,
        out_shape=[jax.ShapeDtypeStruct((N_GRP, nj * TN_IN), F32),
                   jax.ShapeDtypeStruct((N_GRP, LANE), F32)],
        compiler_params=_cparams(("arbitrary", "arbitrary")),
        name="in_proj_lat" if latent else "in_proj_ctx",
    )(jnp.asarray(src_tiles, jnp.int32), x, norm_w.reshape(1, D_MODEL), mod, mod, w_t, w_t, w_t)


def _outproj_kernel(oa_ref, ob_ref, wa_ref, wb_ref, x_ref, g_ref, *rest, final):
    acc = _dot(oa_ref[...], wa_ref[...]) + _dot(ob_ref[...], wb_ref[...])
    xn = x_ref[...] + g_ref[...] * acc
    if final:
        fw_ref, y_ref = rest
        y_ref[...] = xn * lax.rsqrt(jnp.mean(xn * xn, axis=-1, keepdims=True) + EPS) * fw_ref[...]
    else:
        (y_ref,) = rest
        y_ref[...] = xn


def _out_proj(o_a, o_b, w_out, layer, x, mod, latent, final_w=None):
    ka, kb = o_a.shape[1], o_b.shape[1]
    assert ka == kb and w_out.shape[1] == ka + kb
    row = functools.partial(_cond_row, tm=TM_OUT, latent=latent)
    final = final_w is not None
    in_specs = [
        pl.BlockSpec((TM_OUT, ka), lambda i: (i, 0)),
        pl.BlockSpec((TM_OUT, kb), lambda i: (i, 0)),
        pl.BlockSpec((None, ka, D_MODEL), lambda i: (layer, 0, 0)),
        pl.BlockSpec((None, kb, D_MODEL), lambda i: (layer, 1, 0)),
        pl.BlockSpec((TM_OUT, D_MODEL), lambda i: (i, 0)),
        pl.BlockSpec((None, 1, D_MODEL), lambda i: (row(i), 0, 2)),
    ]
    args = [o_a, o_b, w_out, w_out, x, mod]
    if final:
        in_specs.append(pl.BlockSpec((1, D_MODEL), lambda i: (0, 0)))
        args.append(final_w.reshape(1, D_MODEL))
    return pl.pallas_call(
        functools.partial(_outproj_kernel, final=final),
        grid=(N_GRP // TM_OUT,),
        in_specs=in_specs,
        out_specs=pl.BlockSpec((TM_OUT, D_MODEL), lambda i: (i, 0)),
        out_shape=jax.ShapeDtypeStruct((N_GRP, D_MODEL), F32),
        compiler_params=_cparams(("arbitrary",)),
        name="out_proj_final" if final else "out_proj",
    )(*args)


def _attend(problems, scale):
    scores = [[_dot_nt(q, k) for k in ks] for q, ks, _, _, _, _ in problems]
    outs = []
    parts = []
    for (q, ks, vs, masks, biases, sink), raw in zip(problems, scores):
        ss = []
        for s, m, bias in zip(raw, masks, biases):
            s = s * scale
            if bias is not None:
                s = s + bias
            if m is not None:
                s = jnp.where(m, s, NEG)
            ss.append(s)
        mx = functools.reduce(jnp.maximum, [jnp.max(s, axis=-1, keepdims=True) for s in ss])
        if sink is not None:
            mx = jnp.maximum(mx, sink)
        es = [jnp.exp(s - mx) for s in ss]
        den = functools.reduce(jnp.add, [jnp.sum(e, axis=-1, keepdims=True) for e in es])
        if sink is not None:
            den = den + jnp.exp(sink - mx)
        parts.append((es, den))
    pvs = [[_dot(e, v) for e, v in zip(es, vs)] for (es, _), (_, _, vs, _, _, _) in zip(parts, problems)]
    for pv, (_, den) in zip(pvs, parts):
        outs.append(functools.reduce(jnp.add, pv) / den)
    return outs


def _head(ref, h, rows=None):
    if rows is None:
        return ref[:, h * HD:(h + 1) * HD]
    return ref[rows, h * HD:(h + 1) * HD]


def _ctx_attn_kernel(*refs, heads, kv_heads, use_sink, has_prev):
    q_ref, k_ref, v_ref, z_ref = refs[:4]
    sink_ref = refs[4] if use_sink else None
    o_ref, kv_ref = refs[-2:]
    g = heads // kv_heads
    n = q_ref.shape[0]
    problems = []
    for j in range(kv_heads):
        q = jnp.concatenate([_head(q_ref, j * g + t) for t in range(g)], axis=0)
        sink = None
        if use_sink:
            sink = jnp.concatenate([jnp.full((n, 1), sink_ref[j * g + t], F32) for t in range(g)], axis=0)
        problems.append((q, [_head(k_ref, j)], [_head(v_ref, j)], [None], [None], sink))
    outs = _attend(problems, HD ** -0.5)
    for j, o in enumerate(outs):
        for t in range(g):
            h = j * g + t
            o_ref[:, h * HD:(h + 1) * HD] = (o[t * n:(t + 1) * n] * _silu(_head(z_ref, h))).astype(BF16)
    slots = [kv_ref] if has_prev else [kv_ref.at[l] for l in range(kv_ref.shape[0])]
    for slot in slots:
        slot[0] = k_ref[...]
        slot[1] = v_ref[...]


def _ctx_attn(p, q_col, k_col, v_col, z_col, heads, kv_heads, sink, layers, layer, prev):
    wq, wkv = heads * HD, kv_heads * HD
    use_sink = sink is not None
    in_specs = [
        pl.BlockSpec((SEQ, wq), lambda b: (b, q_col)),
        pl.BlockSpec((SEQ, wkv), lambda b: (b, k_col)),
        pl.BlockSpec((SEQ, wkv), lambda b: (b, v_col)),
        pl.BlockSpec((SEQ, wq), lambda b: (b, z_col)),
    ]
    args = [p, p, p, p]
    if use_sink:
        in_specs.append(pl.BlockSpec(memory_space=pltpu.SMEM))
        args.append(sink)
    aliases = {}
    if prev is None:
        kv_spec = pl.BlockSpec((None, layers, 2, SEQ, wkv), lambda b: (b, 0, 0, 0, 0))
    else:
        in_specs.append(pl.BlockSpec(memory_space=pl.ANY))
        args.append(prev)
        aliases = {len(args) - 1: 1}
        kv_spec = pl.BlockSpec((None, None, 2, SEQ, wkv), lambda b: (b, layer, 0, 0, 0))
    return pl.pallas_call(
        functools.partial(_ctx_attn_kernel, heads=heads, kv_heads=kv_heads, use_sink=use_sink,
                          has_prev=prev is not None),
        grid=(BATCH,),
        in_specs=in_specs,
        out_specs=[pl.BlockSpec((SEQ, wq), lambda b: (b, 0)), kv_spec],
        out_shape=[jax.ShapeDtypeStruct((N_CTX, wq), BF16),
                   jax.ShapeDtypeStruct((BATCH, layers, 2, SEQ, wkv), F32)],
        input_output_aliases=aliases,
        compiler_params=_cparams(("arbitrary",)),
        name="ctx_attn_sink" if use_sink else "ctx_attn",
    )(*args)


def _rope_tables():
    half = HD // 4
    freq = (ROPE_THETA ** (-np.arange(half, dtype=np.float32) / half)).astype(np.float32)
    t = np.arange(DEC_SEQ)
    ang_r = (t // GRID_W).astype(np.float32)[:, None] * freq[None, :]
    ang_c = (t % GRID_W).astype(np.float32)[:, None] * freq[None, :]
    cos = np.concatenate([np.cos(ang_r)] * 2 + [np.cos(ang_c)] * 2, axis=1).astype(np.float32)
    sin_r, sin_c, zero = np.sin(ang_r), np.sin(ang_c), np.zeros_like(ang_r)
    s_up = np.concatenate([-sin_r, zero, -sin_c, zero], axis=1).astype(np.float32)
    s_dn = np.concatenate([zero, sin_r, zero, sin_c], axis=1).astype(np.float32)
    return jnp.asarray(cos), jnp.asarray(s_up), jnp.asarray(s_dn)


def _rope(x, cos, s_up, s_dn):
    return x * cos + pltpu.roll(x, HD - HD // 4, 1) * s_up + pltpu.roll(x, HD // 4, 1) * s_dn


WIN_BLOCKS = 2


def _win_attn_kernel(q_ref, k_ref, v_ref, kc_ref, vc_ref, z_ref, cos_ref, sup_ref, sdn_ref, sink_ref, o_ref):
    g = H_B // HKV_B
    span = QBLK + 2 * WIN
    problems = []
    for s in range(WIN_BLOCKS):
        i = pl.program_id(1) * WIN_BLOCKS + s
        start = pl.multiple_of(jnp.clip(i * QBLK - WIN, 0, DEC_SEQ - span), QBLK)
        qrows = pl.ds(pl.multiple_of(i * QBLK, QBLK), QBLK)
        krows = pl.ds(start, span)
        blk = slice(s * QBLK, (s + 1) * QBLK)
        cq, uq, dq = cos_ref[qrows, :], sup_ref[qrows, :], sdn_ref[qrows, :]
        ck, uk, dk = cos_ref[krows, :], sup_ref[krows, :], sdn_ref[krows, :]
        qpos = i * QBLK + lax.broadcasted_iota(jnp.int32, (g * QBLK, span), 0) % QBLK
        kpos = start + lax.broadcasted_iota(jnp.int32, (g * QBLK, span), 1)
        band = jnp.abs(qpos - kpos) <= WIN
        for j in range(HKV_B):
            q = jnp.concatenate([_rope(_head(q_ref, j * g + t, blk), cq, uq, dq) for t in range(g)], axis=0)
            kw = _rope(_head(k_ref, j, krows), ck, uk, dk)
            vw = _head(v_ref, j, krows)
            sink = jnp.concatenate([jnp.full((QBLK, 1), sink_ref[j * g + t], F32) for t in range(g)], axis=0)
            problems.append((q, [kw, _head(kc_ref, j)], [vw, _head(vc_ref, j)], [band, None], [None, None], sink))
    outs = _attend(problems, HD ** -0.5)
    for n, o in enumerate(outs):
        s, j = divmod(n, HKV_B)
        blk = slice(s * QBLK, (s + 1) * QBLK)
        for t in range(g):
            h = j * g + t
            o_ref[blk, h * HD:(h + 1) * HD] = (o[t * QBLK:(t + 1) * QBLK] * _silu(_head(z_ref, h, blk))).astype(BF16)


def _win_attn(p, cache_k, cache_v, sink, q_col, k_col, v_col, z_col):
    wq, wkv = H_B * HD, HKV_B * HD
    nq = DEC_SEQ // QBLK // WIN_BLOCKS
    qb = WIN_BLOCKS * QBLK
    cos, s_up, s_dn = _rope_tables()
    full = pl.BlockSpec((DEC_SEQ, HD), lambda b, i: (0, 0))
    return pl.pallas_call(
        _win_attn_kernel,
        grid=(DEC_BATCH, nq),
        in_specs=[
            pl.BlockSpec((qb, wq), lambda b, i: (b * nq + i, q_col)),
            pl.BlockSpec((DEC_SEQ, wkv), lambda b, i: (b, k_col)),
            pl.BlockSpec((DEC_SEQ, wkv), lambda b, i: (b, v_col)),
            pl.BlockSpec((None, PAST_LEN, wkv), lambda b, i: (b, 0, 0)),
            pl.BlockSpec((None, PAST_LEN, wkv), lambda b, i: (b, 0, 0)),
            pl.BlockSpec((qb, wq), lambda b, i: (b * nq + i, z_col)),
            full, full, full,
            pl.BlockSpec(memory_space=pltpu.SMEM),
        ],
        out_specs=pl.BlockSpec((qb, wq), lambda b, i: (b * nq + i, 0)),
        out_shape=jax.ShapeDtypeStruct((N_LAT, wq), BF16),
        compiler_params=_cparams(("arbitrary", "arbitrary")),
        name="win_attn",
    )(p, p, p, cache_k, cache_v, p, cos, s_up, s_dn, sink)


def _nbr_onehot():
    qc = np.arange(GRID_W)[:, None]
    kc = np.arange(GRID_W)[None, :]
    idx = np.clip(kc - qc, -(NB_W - 1), NB_W - 1) + NB_W - 1
    e = (np.arange(2 * NB_W)[:, None, None] == idx[None]).astype(np.float32)
    return jnp.asarray(e.reshape(2 * NB_W, GRID_W * GRID_W))


def _bias_expand_kernel(r_ref, e_ref, o_ref):
    o_ref[...] = _dot_hi(r_ref[...], e_ref[...])


def _nbr_bias_table(rpb):
    rows = H_D * (2 * NB_H - 1)
    r = jnp.pad(rpb.reshape(rows, 2 * NB_W - 1), ((0, 128 - rows), (0, 1)))
    t = pl.pallas_call(
        _bias_expand_kernel,
        out_shape=jax.ShapeDtypeStruct((128, GRID_W * GRID_W), F32),
        name="nbr_bias_expand",
    )(r, _nbr_onehot())
    t = t[:rows].reshape(H_D, 2 * NB_H - 1, GRID_W, GRID_W)
    return jnp.concatenate([t[:, :-1], t[:, 1:]], axis=-1)


NBR_ROWS = 2


def _nbr_attn_kernel(q_ref, k_ref, v_ref, kc_ref, vc_ref, z_ref, t_ref, o_ref):
    rows = DEC_SEQ // GRID_W
    nk = NB_H * GRID_W
    qc = lax.broadcasted_iota(jnp.int32, (GRID_W, nk), 0)
    kc = lax.broadcasted_iota(jnp.int32, (GRID_W, nk), 1) % GRID_W
    cstart = jnp.clip(qc - NB_W // 2, 0, GRID_W - NB_W)
    ok = (kc >= cstart) & (kc < cstart + NB_W)
    problems = []
    for t in range(NBR_ROWS):
        r = pl.program_id(1) * NBR_ROWS + t
        rs = jnp.clip(r - NB_H // 2, 0, rows - NB_H)
        dr0 = rs - r + NB_H - 1
        krows = pl.ds(pl.multiple_of(rs * GRID_W, GRID_W), nk)
        qrows = slice(t * GRID_W, (t + 1) * GRID_W)
        for h in range(H_D):
            bias = jnp.concatenate([t_ref[h, dr0 + 2 * m] for m in range(nk // LANE)], axis=1)
            problems.append((_head(q_ref, h, qrows), [_head(k_ref, h, krows), _head(kc_ref, h)],
                             [_head(v_ref, h, krows), _head(vc_ref, h)], [ok, None], [bias, None], None))
    outs = _attend(problems, HD ** -0.5)
    for i, o in enumerate(outs):
        t, h = divmod(i, H_D)
        qrows = slice(t * GRID_W, (t + 1) * GRID_W)
        o_ref[qrows, h * HD:(h + 1) * HD] = (o * _silu(_head(z_ref, h, qrows))).astype(BF16)


def _nbr_attn(p, cache_k, cache_v, table, q_col, k_col, v_col, z_col):
    rows = DEC_SEQ // GRID_W // NBR_ROWS
    qb = NBR_ROWS * GRID_W
    return pl.pallas_call(
        _nbr_attn_kernel,
        grid=(DEC_BATCH, rows),
        in_specs=[
            pl.BlockSpec((qb, W_D), lambda b, r: (b * rows + r, q_col)),
            pl.BlockSpec((DEC_SEQ, W_D), lambda b, r: (b, k_col)),
            pl.BlockSpec((DEC_SEQ, W_D), lambda b, r: (b, v_col)),
            pl.BlockSpec((None, PAST_LEN, W_D), lambda b, r: (b, 0, 0)),
            pl.BlockSpec((None, PAST_LEN, W_D), lambda b, r: (b, 0, 0)),
            pl.BlockSpec((qb, W_D), lambda b, r: (b * rows + r, z_col)),
            pl.BlockSpec(table.shape, lambda b, r: (0, 0, 0, 0)),
        ],
        out_specs=pl.BlockSpec((qb, W_D), lambda b, r: (b * rows + r, 0)),
        out_shape=jax.ShapeDtypeStruct((N_LAT, W_D), BF16),
        compiler_params=_cparams(("arbitrary", "arbitrary")),
        name="nbr_attn",
    )(p, p, p, cache_k, cache_v, p, table)


def _lane_col(x, idx):
    lane = lax.broadcasted_iota(jnp.int32, x.shape, 1)
    return jnp.sum(jnp.where(lane == idx, x, 0.0), axis=-1, keepdims=True)


def _short_conv(x_ref, w_ref, pad_ref, n):
    pad = CONV_K // 2
    zeros = jnp.zeros((8, HD), F32)
    pad_ref[0:8, :] = zeros
    pad_ref[n + 8:n + 16, :] = zeros
    pad_ref[8:n + 8, :] = x_ref[...]
    y = functools.reduce(jnp.add, [pad_ref[8 - pad + t:8 - pad + t + n, :] * w_ref[t:t + 1, :]
                                   for t in range(CONV_K)])
    return _silu(y)


def _l2norm(x):
    return x * lax.rsqrt(jnp.sum(x * x, axis=-1, keepdims=True) + EPS)


def _split2(x):
    hi = x.astype(BF16)
    return hi, (x - hi.astype(F32)).astype(BF16)


def _dots_x2(pairs, split_b=True):
    ops = []
    for a, b in pairs:
        a_hi, a_lo = _split2(a)
        b2 = jnp.concatenate(_split2(b), axis=1) if split_b else b.astype(BF16)
        ops.append((jnp.concatenate([a_hi, a_lo], axis=0), b2))
    rs = [jnp.dot(a2, b2, preferred_element_type=F32) for a2, b2 in ops]
    out = []
    for (a, b), r in zip(pairs, rs):
        m, n = a.shape[0], b.shape[1]
        if split_b:
            out.append((r[m:, :n] + r[:m, n:] + r[m:, n:]) + r[:m, :n])
        else:
            out.append(r[m:] + r[:m])
    return out


def _pair_masks():
    c = lax.broadcasted_iota(jnp.int32, (CHUNK, HD), 0)
    l = lax.broadcasted_iota(jnp.int32, (CHUNK, HD), 1)
    left = l < CHUNK
    j = l % CHUNK
    ahead = jnp.where(left, j - c, c - j)
    return left, ahead <= 0, ahead < 0, j == c


def _block_diag(x, left):
    return jnp.concatenate([jnp.where(left, x, 0.0), jnp.where(left, 0.0, x)], axis=0)


def _inv_unit_triangular_pairs(lmats, left, eye):
    mps = [-x for x in lmats]
    ps = [jnp.where(eye, 1.0, 0.0) + m for m in mps]
    mps = _dots_x2([(m, _block_diag(m, left)) for m in mps])
    for _ in range(4):
        rs = _dots_x2([(jnp.concatenate([p, m], axis=0), _block_diag(m, left)) for p, m in zip(ps, mps)])
        ps = [p + r[:CHUNK] for p, r in zip(ps, rs)]
        mps = [r[CHUNK:] for r in rs]
    rs = _dots_x2([(p, _block_diag(m, left)) for p, m in zip(ps, mps)])
    return [p + r for p, r in zip(ps, rs)]


def _chunk_cumsum(x, reverse):
    n = x.shape[0]
    pos = lax.broadcasted_iota(jnp.int32, x.shape, 0) % CHUNK
    k = 1
    while k < CHUNK:
        if reverse:
            x = x + jnp.where(pos < CHUNK - k, pltpu.roll(x, n - k, 0), 0.0)
        else:
            x = x + jnp.where(pos >= k, pltpu.roll(x, k, 0), 0.0)
        k *= 2
    return x


def _delta_prep(items):
    left, tri, strict, eye = _pair_masks()
    zeros = jnp.zeros((CHUNK, HD), F32)
    pre = []
    for q, k, v, cols in items:
        b_f, b_b, g_f, g_b = (cols[:, t:t + 1] for t in range(4))
        gsel = jnp.where(left, g_f, g_b)
        g_row = jnp.sum(jnp.where(eye, gsel, 0.0), axis=0, keepdims=True)
        decay = jnp.exp(jnp.where(tri, gsel - g_row, NEG))
        kb_f, kb_b = k * b_f, k * b_b
        lhs = jnp.concatenate([jnp.concatenate([kb_f, kb_b], axis=1), jnp.concatenate([q, q], axis=1)], axis=0)
        rhs = jnp.concatenate([jnp.concatenate([k, zeros], axis=1), jnp.concatenate([zeros, k], axis=1)], axis=0)
        pre.append((decay, kb_f, kb_b, g_f, g_b, b_f, b_b, lhs, rhs))
    kqs = [_dot_nt(x[7], x[8]) for x in pre]
    lmats = [jnp.where(strict, kq[:CHUNK] * x[0], 0.0) for kq, x in zip(kqs, pre)]
    attns = [jnp.where(tri, kq[CHUNK:] * x[0], 0.0) for kq, x in zip(kqs, pre)]
    tinvs = _inv_unit_triangular_pairs(lmats, left, eye)
    egs, rhs = [], []
    for (q, k, v, cols), x in zip(items, pre):
        _, kb_f, kb_b, g_f, g_b, b_f, b_b = x[:7]
        eg_f = jnp.exp(jnp.broadcast_to(g_f, (CHUNK, HD)))
        eg_b = jnp.exp(jnp.broadcast_to(g_b, (CHUNK, HD)))
        egs.append((eg_f, eg_b))
        rhs.append(jnp.concatenate([jnp.concatenate([v * b_f, kb_f * eg_f, zeros, zeros], axis=1),
                                    jnp.concatenate([zeros, zeros, v * b_b, kb_b * eg_b], axis=1)], axis=0))
    uws = _dots_x2(list(zip(tinvs, rhs)), split_b=False)
    out = []
    for (q, k, v, cols), x, attn, uw, (eg_f, eg_b) in zip(items, pre, attns, uws, egs):
        g_f, g_b = x[3], x[4]
        gl_f = jnp.broadcast_to(g_f[CHUNK - 1:CHUNK, :], (1, HD))
        gl_b = jnp.broadcast_to(g_b[0:1, :], (1, HD))
        kg_t = jnp.concatenate([k * jnp.exp(gl_f - g_f), k * jnp.exp(gl_b - g_b)], axis=1).T
        out.append(dict(u=[uw[:, :HD], uw[:, 2 * HD:3 * HD]], w=[uw[:, HD:2 * HD], uw[:, 3 * HD:]],
                        attn=[attn[:, :CHUNK], attn[:, CHUNK:]], qg=[q * eg_f, q * eg_b],
                        kg_t=[kg_t[:HD], kg_t[HD:]], eg=[jnp.exp(gl_f), jnp.exp(gl_b)]))
    return out


def _delta_kernel(*refs, n, hb, cg, has_s0, has_prev):
    qp_ref, kp_ref, vp_ref, z_ref, sm_ref, cq_ref, ck_ref, cv_ref, al_ref, dt_ref, on_ref = refs[:11]
    rest = refs[11 + int(has_prev):]
    if has_s0:
        s0_ref, o_ref = rest[:2]
        so_ref = None
    else:
        o_ref, so_ref = rest[:2]
    q_s, k_s, v_s, pad_s, col_s, u_s, w_s, at_s, qg_s, kg_s, eg_s, acc_s, st_s, gate_s = rest[2:]
    h0 = pl.program_id(1) * hb
    nc = n // CHUNK

    @pl.when(pl.program_id(1) == 0)
    def _():
        sm = sm_ref[...]
        x = sm + dt_ref[...]
        softplus = jnp.maximum(x, 0.0) + jnp.log(1.0 + jnp.exp(-jnp.abs(x)))
        gates = -jnp.exp(al_ref[...]) * softplus
        gate_s[0] = _sigmoid(sm)
        gate_s[1] = _chunk_cumsum(gates, False)
        gate_s[2] = _chunk_cumsum(gates, True)

    betas, cum_f, cum_b = gate_s[0], gate_s[1], gate_s[2]
    lane = lax.broadcasted_iota(jnp.int32, (n, HD), 1)
    for j in range(hb):
        cols = slice(j * HD, (j + 1) * HD)
        q_s[j] = _l2norm(_short_conv(qp_ref.at[:, cols], cq_ref.at[:, cols], pad_s, n)) * HD ** -0.5
        k_s[j] = _l2norm(_short_conv(kp_ref.at[:, cols], ck_ref.at[:, cols], pad_s, n))
        v_s[j] = _short_conv(vp_ref.at[:, cols], cv_ref.at[:, cols], pad_s, n)
        col_s[j] = jnp.where(lane == 0, _lane_col(betas, h0 + j),
                             jnp.where(lane == 1, _lane_col(betas, H_A + h0 + j),
                                       jnp.where(lane == 2, _lane_col(cum_f, 2 * H_A + h0 + j),
                                                 _lane_col(cum_b, 3 * H_A + h0 + j))))
    acc_s[...] = jnp.zeros((hb, n, HD), F32)
    if has_s0:
        st_s[...] = s0_ref[...]
    else:
        st_s[...] = jnp.zeros((2, hb, HD, HD), F32)

    def prep(ci, carry):
        where = [(j, ci * cg + t) for j in range(hb) for t in range(cg)]
        rows = [pl.ds(pl.multiple_of(c * CHUNK, CHUNK), CHUNK) for _, c in where]
        outs = _delta_prep([(q_s[j, r, :], k_s[j, r, :], v_s[j, r, :], col_s[j, r, :])
                            for (j, _), r in zip(where, rows)])
        for (j, c), r, o in zip(where, rows, outs):
            for d in range(2):
                u_s[d, j, r, :] = o["u"][d]
                w_s[d, j, r, :] = o["w"][d].astype(BF16)
                at_s[d, j, r, :] = o["attn"][d].astype(BF16)
                qg_s[d, j, r, :] = o["qg"][d].astype(BF16)
                kg_s[d, j, c] = o["kg_t"][d].astype(BF16)
                eg_s[d, j, c] = jnp.broadcast_to(o["eg"][d], (8, HD))
        return carry

    lax.fori_loop(0, nc // cg, prep, 0)

    def scan(i, carry):
        chains = [(d, j, (nc - 1 - i) if d else i) for d in range(2) for j in range(hb)]
        rows = [pl.ds(pl.multiple_of(c * CHUNK, CHUNK), CHUNK) for _, _, c in chains]
        ss = [st_s[d, j] for d, j, _ in chains]
        sbs = [s.astype(BF16) for s in ss]
        ws = [jnp.dot(w_s[d, j, r, :], sb, preferred_element_type=F32) for (d, j, _), r, sb in zip(chains, rows, sbs)]
        vns = [(u_s[d, j, r, :] - w).astype(BF16) for (d, j, _), r, w in zip(chains, rows, ws)]
        for (d, j, c), r, s, sb, vn in zip(chains, rows, ss, sbs, vns):
            st_s[d, j] = s * eg_s[d, j, c, 0:1, :] + jnp.dot(kg_s[d, j, c], vn, preferred_element_type=F32)
        for (d, j, c), r, sb, vn in zip(chains, rows, sbs, vns):
            acc_s[j, r, :] += (jnp.dot(qg_s[d, j, r, :], sb, preferred_element_type=F32)
                               + jnp.dot(at_s[d, j, r, :], vn, preferred_element_type=F32))
        return carry

    lax.fori_loop(0, nc, scan, 0)
    for j in range(hb):
        cols = slice(j * HD, (j + 1) * HD)
        o = acc_s[j]
        o = o * lax.rsqrt(jnp.mean(o * o, axis=-1, keepdims=True) + EPS) * on_ref[...]
        o_ref[:, cols] = (o * _silu(z_ref[:, cols])).astype(BF16)
    if so_ref is not None:
        _write_layer_slots(so_ref, st_s[...], has_prev)


def _write_layer_slots(ref, value, has_prev):
    if has_prev:
        ref[...] = value
    else:
        for l in range(ref.shape[0]):
            ref[l] = value


def _delta_mixer(p, small, conv_w, a_row, dt_row, onorm, s0, seq, nb, hb, cg, layers=1, layer=0, prev=None):
    has_s0 = s0 is not None
    wb = hb * HD
    ng = H_A // hb
    nc = seq // CHUNK
    col = lambda off: (lambda b, h: (b, off + h))
    in_specs = [
        pl.BlockSpec((seq, wb), col(0)),
        pl.BlockSpec((seq, wb), col(ng)),
        pl.BlockSpec((seq, wb), col(2 * ng)),
        pl.BlockSpec((seq, wb), col(3 * ng)),
        pl.BlockSpec((seq, LANE), lambda b, h: (b, 0)),
        pl.BlockSpec((CONV_K, wb), lambda b, h: (0, h)),
        pl.BlockSpec((CONV_K, wb), lambda b, h: (0, ng + h)),
        pl.BlockSpec((CONV_K, wb), lambda b, h: (0, 2 * ng + h)),
        pl.BlockSpec((1, LANE), lambda b, h: (0, 0)),
        pl.BlockSpec((1, LANE), lambda b, h: (0, 0)),
        pl.BlockSpec((1, HD), lambda b, h: (0, 0)),
    ]
    args = [p, p, p, p, small, conv_w, conv_w, conv_w, a_row, dt_row, onorm.reshape(1, HD)]
    if has_s0:
        in_specs.append(pl.BlockSpec((None, 2, hb, HD, HD), lambda b, h: (b, 0, h, 0, 0)))
        args.append(s0)
    out_specs = [pl.BlockSpec((seq, wb), lambda b, h: (b, h))]
    out_shape = [jax.ShapeDtypeStruct((nb * seq, W_A), BF16)]
    aliases = {}
    if not has_s0:
        if prev is None:
            out_specs.append(pl.BlockSpec((None, layers, 2, hb, HD, HD), lambda b, h: (b, 0, 0, h, 0, 0)))
        else:
            in_specs.append(pl.BlockSpec(memory_space=pl.ANY))
            args.append(prev)
            aliases = {len(args) - 1: 1}
            out_specs.append(pl.BlockSpec((None, None, 2, hb, HD, HD), lambda b, h: (b, layer, 0, h, 0, 0)))
        out_shape.append(jax.ShapeDtypeStruct((nb, layers, 2, H_A, HD, HD), F32))
    return pl.pallas_call(
        functools.partial(_delta_kernel, n=seq, hb=hb, cg=cg, has_s0=has_s0, has_prev=prev is not None),
        grid=(nb, ng),
        in_specs=in_specs,
        out_specs=out_specs,
        out_shape=out_shape,
        input_output_aliases=aliases,
        scratch_shapes=[
            pltpu.VMEM((hb, seq, HD), F32), pltpu.VMEM((hb, seq, HD), F32), pltpu.VMEM((hb, seq, HD), F32),
            pltpu.VMEM((seq + 16, HD), F32), pltpu.VMEM((hb, seq, HD), F32),
            pltpu.VMEM((2, hb, seq, HD), F32), pltpu.VMEM((2, hb, seq, HD), BF16),
            pltpu.VMEM((2, hb, seq, CHUNK), BF16), pltpu.VMEM((2, hb, seq, HD), BF16),
            pltpu.VMEM((2, hb, nc, HD, CHUNK), BF16), pltpu.VMEM((2, hb, nc, 8, HD), F32),
            pltpu.VMEM((hb, seq, HD), F32), pltpu.VMEM((2, hb, HD, HD), F32),
            pltpu.VMEM((3, seq, LANE), F32)],
        compiler_params=_cparams(("arbitrary", "arbitrary")),
        name="delta_lat" if has_s0 else "delta_ctx",
    )(*args)


_GLA_LEVELS = (32, 16, 8, 4, 2, 1)
GLA_CHUNKS = 4


def _gla_consts(reverse):
    r = np.arange(CHUNK)
    flip = (lambda a: a[::-1, ::-1]) if reverse else (lambda a: a)
    sel, hi, pair = [], [], []
    for m in _GLA_LEVELS:
        mid = (r // (2 * m)) * (2 * m) + m
        is_hi = r >= mid
        sel.append(flip(r[None, :] == mid[:, None] - 1))
        hi.append(flip(is_hi[:, None]))
        pair.append(flip((r[:, None] // (2 * m) == r[None, :] // (2 * m)) & is_hi[:, None] & ~is_hi[None, :]))
    out = [np.concatenate(sel), np.concatenate(hi), np.stack(pair)]
    out = [jnp.asarray(np.ascontiguousarray(a).astype(np.float32)) for a in out]
    return [out[0].astype(BF16), out[1], out[2]]


def _split3(x):
    hi = x.astype(BF16)
    r1 = x - hi.astype(F32)
    mid = r1.astype(BF16)
    return hi, mid, (r1 - mid.astype(F32)).astype(BF16)


def _gla_prep(items):
    nl = len(_GLA_LEVELS)
    r3s = [jnp.dot(consts[0][...], jnp.concatenate(_split3(b), axis=1), preferred_element_type=F32)
           for _, _, _, b, consts, _ in items]
    refs = [(r3[:, 2 * DK_C:] + r3[:, DK_C:2 * DK_C]) + r3[:, :DK_C] for r3 in r3s]
    c = lax.broadcasted_iota(jnp.int32, (CHUNK, CHUNK), 0)
    j = lax.broadcasted_iota(jnp.int32, (CHUNK, CHUNK), 1)
    lvl = []
    for (q, k, v, b, consts, _), ref in zip(items, refs):
        ops = []
        for lv in range(nl):
            rows = slice(lv * CHUNK, (lv + 1) * CHUNK)
            hi = consts[1][rows, :] > 0.5
            t = b - ref[rows]
            e = jnp.exp(jnp.where(hi, t, -t))
            ops.append((jnp.where(hi, q * e, 0.0), jnp.where(hi, 0.0, k * e)))
        lvl.append(ops)
    prods = [[_dot_nt(ql, kl) for ql, kl in ops] for ops in lvl]
    amats = []
    for (q, k, v, b, consts, _), pr in zip(items, prods):
        a = jnp.where(c == j, jnp.sum(q * k, axis=-1, keepdims=True), 0.0)
        for lv in range(nl):
            a = a + pr[lv] * consts[2][lv]
        amats.append(a)
    intras = [_dot(a, it[2]) for a, it in zip(amats, items)]
    r128 = lax.broadcasted_iota(jnp.int32, (DK_C, DK_C), 0)
    c128 = lax.broadcasted_iota(jnp.int32, (DK_C, DK_C), 1)
    out = []
    for (q, k, v, b, consts, reverse), intra in zip(items, intras):
        last = 0 if reverse else CHUNK - 1
        bl = b[last:last + 1, :]
        dec = jnp.sum(jnp.where(r128 == c128, jnp.broadcast_to(jnp.exp(bl), (DK_C, DK_C)), 0.0),
                      axis=-1, keepdims=True)
        out.append((intra, q * jnp.exp(b), dec, _dot_tn(k * jnp.exp(bl - b), v)))
    return out


def _gla_kernel(*refs, n, cg, has_s0, has_prev):
    consts_f, consts_b = refs[8:11], refs[11:14]
    q_ref, k_ref, v_ref, z_ref, sm_ref, wg_ref, bg_ref, on_ref = refs[:8]
    if has_s0:
        s0_ref, o_ref, gk_s, acc_s, st_s = refs[14:]
        so_ref = None
    else:
        o_ref, so_ref, gk_s, acc_s, st_s = refs[14 + int(has_prev):]
    nc = n // CHUNK
    sm = sm_ref[...]
    for d in range(2):
        x = _dot_hi(sm, wg_ref[d]) + bg_ref[d]
        gk = (jnp.minimum(x, 0.0) - jnp.log(1.0 + jnp.exp(-jnp.abs(x)))) / GLA_TAU
        gk_s[d] = _chunk_cumsum(gk, bool(d))
    acc_s[...] = jnp.zeros((n, DV_C), F32)
    if has_s0:
        st_s[...] = s0_ref[...]
    else:
        st_s[...] = jnp.zeros((2, DK_C, DV_C), F32)

    def body(i, carry):
        where = [(d, (nc - 1 - (i * cg + t)) if d else (i * cg + t)) for d in range(2) for t in range(cg)]
        rows = [pl.ds(pl.multiple_of(c * CHUNK, CHUNK), CHUNK) for _, c in where]
        outs = _gla_prep([(q_ref[r, :] * DK_C ** -0.5, k_ref[r, :], v_ref[r, :], gk_s[d, r, :],
                           consts_b if d else consts_f, bool(d)) for (d, _), r in zip(where, rows)])
        for d in range(2):
            s = st_s[d]
            for t in range(cg):
                intra, qe, dec, kv = outs[d * cg + t]
                acc_s[rows[d * cg + t], :] += intra + _dot(qe, s)
                s = s * dec + kv
            st_s[d] = s
        return carry

    lax.fori_loop(0, nc // cg, body, 0)
    o = acc_s[...]
    o = o * lax.rsqrt(jnp.mean(o * o, axis=-1, keepdims=True) + EPS) * on_ref[...]
    o_ref[...] = (o * _silu(z_ref[...])).astype(BF16)
    if so_ref is not None:
        _write_layer_slots(so_ref, st_s[...], has_prev)


def _gla_mixer(p, small, w_gate, b_gate, onorm, s0, seq, nb, layers=1, layer=0, prev=None):
    has_s0 = s0 is not None
    consts = _gla_consts(False) + _gla_consts(True)
    const_specs = [pl.BlockSpec(a.shape, (lambda b, h, nd=a.ndim: (0,) * nd)) for a in consts]
    in_specs = [
        pl.BlockSpec((seq, DK_C), lambda b, h: (b, h)),
        pl.BlockSpec((seq, DK_C), lambda b, h: (b, QK_C // DK_C + h)),
        pl.BlockSpec((seq, DV_C), lambda b, h: (b, 2 * QK_C // DV_C + h)),
        pl.BlockSpec((seq, DV_C), lambda b, h: (b, (2 * QK_C + W_C) // DV_C + h)),
        pl.BlockSpec((seq, LANE), lambda b, h: (b, 0)),
        pl.BlockSpec((2, LANE, DK_C), lambda b, h: (0, 0, h)),
        pl.BlockSpec((2, 1, DK_C), lambda b, h: (0, 0, h)),
        pl.BlockSpec((1, DV_C), lambda b, h: (0, 0)),
    ] + const_specs
    args = [p, p, p, p, small, w_gate, b_gate.reshape(2, 1, QK_C), onorm.reshape(1, DV_C)] + consts
    if has_s0:
        in_specs.append(pl.BlockSpec((None, 2, None, DK_C, DV_C), lambda b, h: (b, 0, h, 0, 0)))
        args.append(s0)
    out_specs = [pl.BlockSpec((seq, DV_C), lambda b, h: (b, h))]
    out_shape = [jax.ShapeDtypeStruct((nb * seq, W_C), BF16)]
    aliases = {}
    if not has_s0:
        if prev is None:
            out_specs.append(pl.BlockSpec((None, layers, 2, None, DK_C, DV_C), lambda b, h: (b, 0, 0, h, 0, 0)))
        else:
            in_specs.append(pl.BlockSpec(memory_space=pl.ANY))
            args.append(prev)
            aliases = {len(args) - 1: 1}
            out_specs.append(pl.BlockSpec((None, None, 2, None, DK_C, DV_C), lambda b, h: (b, layer, 0, h, 0, 0)))
        out_shape.append(jax.ShapeDtypeStruct((nb, layers, 2, H_C, DK_C, DV_C), F32))
    return pl.pallas_call(
        functools.partial(_gla_kernel, n=seq, cg=GLA_CHUNKS, has_s0=has_s0, has_prev=prev is not None),
        grid=(nb, H_C),
        in_specs=in_specs,
        out_specs=out_specs,
        out_shape=out_shape,
        input_output_aliases=aliases,
        scratch_shapes=[pltpu.VMEM((2, seq, DK_C), F32), pltpu.VMEM((seq, DV_C), F32),
                        pltpu.VMEM((2, DK_C, DV_C), F32)],
        compiler_params=_cparams(("arbitrary", "arbitrary")),
        name="gla_lat" if has_s0 else "gla_ctx",
    )(*args)


_EV_TILES = tuple(range(4 * W_A // TN_IN)) + (8, 9, 11, 12, 10)
_EV_ALIGNED = 4 * W_A // TN_IN
_EV_QB = 4 * W_A
_EV_ZB = _EV_QB + W_B
_EV_KB = _EV_ZB + W_B
_EV_VB = _EV_KB + HKV_B * HD
_OD_ALIGNED = (2 * QK_C + 2 * W_C) // TN_IN
_OD_TILES = tuple(range((P_ODD - 2 * GLA_RANK) // TN_IN))
_OD_QD = 2 * QK_C + 2 * W_C
_OD_KD = _OD_QD + W_D
_OD_VD = _OD_KD + W_D
_OD_ZD = _OD_VD + W_D


def _lane_row(v, offset):
    return jnp.pad(v.reshape(1, -1), ((0, 0), (offset, LANE - offset - v.size)))


def _even_layer(xc, xl, e, mod, norm_w, w_in, conv_a, a_log, dt_bias, onorm, sink, w_out, state_delta, cache_kv,
                final_w, prev_st, prev_kv):
    pc, sc = _in_proj(xc, norm_w, mod, w_in, e, _EV_TILES, _EV_ALIGNED, 4 * W_A, False)
    pq, sq = _in_proj(xl, norm_w, mod, w_in, e, _EV_TILES, _EV_ALIGNED, 4 * W_A, True)
    a_row = _lane_row(a_log, 2 * H_A)
    dt_row = _lane_row(dt_bias, 2 * H_A)
    oa_c, st = _delta_mixer(pc, sc, conv_a, a_row, dt_row, onorm, None, SEQ, BATCH, 8, 1, N_EVEN, e, prev_st)
    (oa_l,) = _delta_mixer(pq, sq, conv_a, a_row, dt_row, onorm, state_delta[:, e], DEC_SEQ, DEC_BATCH, 4, 2)
    wkv = HKV_B * HD
    ob_c, kv = _ctx_attn(pc, _EV_QB // W_B, _EV_KB // wkv, _EV_VB // wkv, _EV_ZB // W_B, H_B, HKV_B, sink,
                         N_EVEN, e, prev_kv)
    ck = cache_kv[:, e, 0].reshape(DEC_BATCH, PAST_LEN, wkv)
    cv = cache_kv[:, e, 1].reshape(DEC_BATCH, PAST_LEN, wkv)
    ob_l = _win_attn(pq, ck, cv, sink, _EV_QB // W_B, _EV_KB // wkv, _EV_VB // wkv, _EV_ZB // W_B)
    xc = _out_proj(oa_c, ob_c, w_out, e, xc, mod, False, final_w)
    xl = _out_proj(oa_l, ob_l, w_out, e, xl, mod, True, final_w)
    return xc, xl, st, kv


def _odd_layer(xc, xl, o_i, mod, norm_w, w_in, w_glr, b_glr, onorm, rpb, w_out, state_gla, cache_kv, final_w,
               prev_st, prev_kv):
    lo = 2 * QK_C + 2 * W_C
    pc, sc = _in_proj(xc, norm_w, mod, w_in, o_i, _OD_TILES, _OD_ALIGNED, lo, False)
    pq, sq = _in_proj(xl, norm_w, mod, w_in, o_i, _OD_TILES, _OD_ALIGNED, lo, True)
    w_gate = jnp.stack([jnp.pad(w_glr[0], ((0, LANE - GLA_RANK), (0, 0))),
                        jnp.pad(w_glr[1], ((GLA_RANK, LANE - 2 * GLA_RANK), (0, 0)))])
    oc_c, st = _gla_mixer(pc, sc, w_gate, b_glr, onorm, None, SEQ, BATCH, N_ODD, o_i, prev_st)
    (oc_l,) = _gla_mixer(pq, sq, w_gate, b_glr, onorm, state_gla[:, o_i], DEC_SEQ, DEC_BATCH)
    od_c, kv = _ctx_attn(pc, _OD_QD // W_D, _OD_KD // W_D, _OD_VD // W_D, _OD_ZD // W_D, H_D, H_D, None,
                         N_ODD, o_i, prev_kv)
    ck = cache_kv[:, o_i, 0].reshape(DEC_BATCH, PAST_LEN, W_D)
    cv = cache_kv[:, o_i, 1].reshape(DEC_BATCH, PAST_LEN, W_D)
    od_l = _nbr_attn(pq, ck, cv, _nbr_bias_table(rpb), _OD_QD // W_D, _OD_KD // W_D, _OD_VD // W_D, _OD_ZD // W_D)
    xc = _out_proj(oc_c, od_c, w_out, o_i, xc, mod, False, final_w)
    xl = _out_proj(oc_l, od_l, w_out, o_i, xl, mod, True, final_w)
    return xc, xl, st, kv


def kernel(x_prompt, x_sample, state_delta, cache_kv_win, state_gla, cache_kv_nbr, c, c_ctx, norm_w, w_ada, b_ada, w_in_even, conv_a, a_log_a, dt_bias_a, onorm_a, sink_b, w_out_even, w_in_odd, w_glr_c, b_glr_c, onorm_c, rpb_d, w_out_odd, final_norm_w):
    xc = x_prompt.reshape(N_CTX, D_MODEL)
    xl = x_sample.reshape(N_LAT, D_MODEL)
    cond = jnp.concatenate([c_ctx[None, :], c, jnp.zeros((N_COND - 1 - DEC_BATCH, D_MODEL), F32)], axis=0)
    mods = _ada_mod(cond, w_ada, b_ada).reshape(DEPTH, N_COND, 1, 3 * D_MODEL)
    wo_even, wo_odd = w_out_even.astype(BF16), w_out_odd.astype(BF16)
    w_in_even, w_in_odd = jnp.swapaxes(w_in_even, 1, 2), jnp.swapaxes(w_in_odd, 1, 2)
    new_delta = new_kvw = new_gla = new_kvn = None
    for li in range(DEPTH):
        final_w = final_norm_w if li == DEPTH - 1 else None
        if li % 2 == 0:
            e = li // 2
            xc, xl, new_delta, new_kvw = _even_layer(xc, xl, e, mods[li], norm_w[li], w_in_even, conv_a[e], a_log_a[e],
                                                     dt_bias_a[e], onorm_a[e], sink_b[e], wo_even, state_delta,
                                                     cache_kv_win, final_w, new_delta, new_kvw)
        else:
            o_i = li // 2
            xc, xl, new_gla, new_kvn = _odd_layer(xc, xl, o_i, mods[li], norm_w[li], w_in_odd, w_glr_c[o_i],
                                                  b_glr_c[o_i], onorm_c[o_i], rpb_d[o_i], wo_odd, state_gla,
                                                  cache_kv_nbr, final_w, new_gla, new_kvn)
    return (xc.reshape(BATCH, SEQ, D_MODEL), xl.reshape(DEC_BATCH, DEC_SEQ, D_MODEL),
            new_delta, new_kvw.reshape(BATCH, N_EVEN, 2, SEQ, HKV_B, HD),
            new_gla, new_kvn.reshape(BATCH, N_ODD, 2, SEQ, H_D, HD))
```

```python
import functools

import numpy as np
import jax
import jax.numpy as jnp
from jax import lax
from jax.experimental import pallas as pl
from jax.experimental.pallas import tpu as pltpu

F32 = jnp.float32
BF16 = jnp.bfloat16
HIGHEST = lax.Precision.HIGHEST

D_MODEL = 2048
BATCH = 16
SEQ = 256
DEPTH = 4
DEC_BATCH = 4
DEC_SEQ = 1024
PAST_LEN = 256
GRID_W = 64
HD = 128
EPS = 1e-6
NEG = -1e30
ROPE_THETA = 10000.0
CHUNK = 64
H_A = 8
W_A = H_A * HD
CONV_K = 5
H_B = 8
HKV_B = 2
W_B = H_B * HD
WIN = 128
QBLK = 128
H_C = 4
DK_C = 128
DV_C = 256
QK_C = H_C * DK_C
W_C = H_C * DV_C
GLA_RANK = 16
GLA_TAU = 16.0
H_D = 8
W_D = H_D * HD
NB_H = 8
NB_W = 16
N_EVEN = (DEPTH + 1) // 2
N_ODD = DEPTH // 2
PA_EVEN = 4 * W_A + 4 * H_A
P_EVEN = PA_EVEN + 2 * W_B + 2 * HKV_B * HD
PC_ODD = 2 * QK_C + 2 * W_C + 2 * GLA_RANK
P_ODD = PC_ODD + 4 * W_D

N_CTX = BATCH * SEQ
N_LAT = DEC_BATCH * DEC_SEQ
N_GRP = N_CTX
assert N_LAT == N_GRP
N_COND = 8
LANE = 128
TM_IN = 1024
TN_IN = 512
GATE_COLS = 32
TM_OUT = 512
TN_ADA = 2048
VMEM_LIMIT = 60000 * 1024


def _cparams(sem):
    return pltpu.CompilerParams(dimension_semantics=sem, vmem_limit_bytes=VMEM_LIMIT)


def _sigmoid(x):
    return 1.0 / (1.0 + jnp.exp(-x))


def _silu(x):
    return x * _sigmoid(x)


def _dot(a, b):
    return jnp.dot(a.astype(BF16), b.astype(BF16), preferred_element_type=F32)


def _dot_nt(a, b):
    return lax.dot_general(a.astype(BF16), b.astype(BF16), (((1,), (1,)), ((), ())),
                           preferred_element_type=F32)


def _dot_tn(a, b):
    return lax.dot_general(a.astype(BF16), b.astype(BF16), (((0,), (0,)), ((), ())),
                           preferred_element_type=F32)


def _dot_hi(a, b):
    return jnp.dot(a, b, precision=HIGHEST, preferred_element_type=F32)


def _ada_kernel(c_ref, w_ref, b_ref, o_ref):
    o_ref[...] = _dot(_silu(c_ref[...]), w_ref[...]) + b_ref[...]


def _ada_mod(cond, w_ada, b_ada):
    n3 = 3 * D_MODEL
    return pl.pallas_call(
        _ada_kernel,
        grid=(DEPTH, n3 // TN_ADA),
        in_specs=[
            pl.BlockSpec((N_COND, D_MODEL), lambda l, j: (0, 0)),
            pl.BlockSpec((None, D_MODEL, TN_ADA), lambda l, j: (l, 0, j)),
            pl.BlockSpec((None, 1, TN_ADA), lambda l, j: (l, 0, j)),
        ],
        out_specs=pl.BlockSpec((None, N_COND, TN_ADA), lambda l, j: (l, 0, j)),
        out_shape=jax.ShapeDtypeStruct((DEPTH, N_COND, n3), F32),
        compiler_params=_cparams(("arbitrary", "arbitrary")),
        name="ada_mod",
    )(cond, w_ada, b_ada.reshape(DEPTH, 1, n3))


def _cond_row(i, tm, latent):
    return 1 + i // (DEC_SEQ // tm) if latent else 0


def _inproj_kernel(src_ref, x_ref, nw_ref, shift_ref, scale_ref, wa_ref, wb_ref, ws_ref, o_ref, os_ref,
                   h_ref, w_ref, *, n_aligned):
    j, i = pl.program_id(0), pl.program_id(1)
    rows = pl.ds(pl.multiple_of(i * TM_IN, TM_IN), TM_IN)

    @pl.when(j == 0)
    def _():
        x = x_ref[...]
        gain = nw_ref[...] * (1.0 + scale_ref[...])
        h = (x * lax.rsqrt(jnp.mean(x * x, axis=-1, keepdims=True) + EPS) * gain + shift_ref[...]).astype(BF16)
        h_ref[rows, :] = h
        os_ref[...] = _dot_nt(h, ws_ref[...])

    @pl.when((i == 0) & (j < n_aligned))
    def _():
        w_ref[...] = wa_ref[...].astype(BF16)

    @pl.when((i == 0) & (j >= n_aligned))
    def _():
        w_ref[0:TN_IN - GATE_COLS, :] = wa_ref[GATE_COLS:TN_IN, :].astype(BF16)
        w_ref[TN_IN - GATE_COLS:TN_IN, :] = wb_ref[0:GATE_COLS, :].astype(BF16)

    o_ref[...] = _dot_nt(h_ref[rows, :], w_ref[...])


def _in_proj(x, norm_w, mod, w_t, layer, src_tiles, n_aligned, gate_col, latent):
    nj, ni = len(src_tiles), N_GRP // TM_IN
    sub = TN_IN // LANE
    row = functools.partial(_cond_row, tm=TM_IN, latent=latent)
    tok = lambda j, i, s: (jnp.where(j == 0, i, ni - 1), 0)
    grid_spec = pltpu.PrefetchScalarGridSpec(
        num_scalar_prefetch=1,
        grid=(nj, ni),
        in_specs=[
            pl.BlockSpec((TM_IN, D_MODEL), tok),
            pl.BlockSpec((1, D_MODEL), lambda j, i, s: (0, 0)),
            pl.BlockSpec((None, 1, D_MODEL), lambda j, i, s: (row(jnp.where(j == 0, i, ni - 1)), 0, 0)),
            pl.BlockSpec((None, 1, D_MODEL), lambda j, i, s: (row(jnp.where(j == 0, i, ni - 1)), 0, 1)),
            pl.BlockSpec((None, TN_IN, D_MODEL), lambda j, i, s: (layer, s[j], 0)),
            pl.BlockSpec((None, LANE, D_MODEL), lambda j, i, s: (layer, (s[j] + 1) * sub, 0)),
            pl.BlockSpec((None, LANE, D_MODEL), lambda j, i, s: (layer, gate_col // LANE, 0)),
        ],
        out_specs=[
            pl.BlockSpec((TM_IN, TN_IN), lambda j, i, s: (i, j)),
            pl.BlockSpec((TM_IN, LANE), tok),
        ],
        scratch_shapes=[pltpu.VMEM((N_GRP, D_MODEL), BF16), pltpu.VMEM((TN_IN, D_MODEL), BF16)],
    )
    return pl.pallas_call(
        functools.partial(_inproj_kernel, n_aligned=n_aligned),
        grid_spec=grid_spec,
        out_shape=[jax.ShapeDtypeStruct((N_GRP, nj * TN_IN), F32),
                   jax.ShapeDtypeStruct((N_GRP, LANE), F32)],
        compiler_params=_cparams(("arbitrary", "arbitrary")),
        name="in_proj_lat" if latent else "in_proj_ctx",
    )(jnp.asarray(src_tiles, jnp.int32), x, norm_w.reshape(1, D_MODEL), mod, mod, w_t, w_t, w_t)


def _outproj_kernel(oa_ref, ob_ref, wa_ref, wb_ref, x_ref, g_ref, *rest, final):
    acc = _dot(oa_ref[...], wa_ref[...]) + _dot(ob_ref[...], wb_ref[...])
    xn = x_ref[...] + g_ref[...] * acc
    if final:
        fw_ref, y_ref = rest
        y_ref[...] = xn * lax.rsqrt(jnp.mean(xn * xn, axis=-1, keepdims=True) + EPS) * fw_ref[...]
    else:
        (y_ref,) = rest
        y_ref[...] = xn


def _out_proj(o_a, o_b, w_out, layer, x, mod, latent, final_w=None):
    ka, kb = o_a.shape[1], o_b.shape[1]
    assert ka == kb and w_out.shape[1] == ka + kb
    row = functools.partial(_cond_row, tm=TM_OUT, latent=latent)
    final = final_w is not None
    in_specs = [
        pl.BlockSpec((TM_OUT, ka), lambda i: (i, 0)),
        pl.BlockSpec((TM_OUT, kb), lambda i: (i, 0)),
        pl.BlockSpec((None, ka, D_MODEL), lambda i: (layer, 0, 0)),
        pl.BlockSpec((None, kb, D_MODEL), lambda i: (layer, 1, 0)),
        pl.BlockSpec((TM_OUT, D_MODEL), lambda i: (i, 0)),
        pl.BlockSpec((None, 1, D_MODEL), lambda i: (row(i), 0, 2)),
    ]
    args = [o_a, o_b, w_out, w_out, x, mod]
    if final:
        in_specs.append(pl.BlockSpec((1, D_MODEL), lambda i: (0, 0)))
        args.append(final_w.reshape(1, D_MODEL))
    return pl.pallas_call(
        functools.partial(_outproj_kernel, final=final),
        grid=(N_GRP // TM_OUT,),
        in_specs=in_specs,
        out_specs=pl.BlockSpec((TM_OUT, D_MODEL), lambda i: (i, 0)),
        out_shape=jax.ShapeDtypeStruct((N_GRP, D_MODEL), F32),
        compiler_params=_cparams(("arbitrary",)),
        name="out_proj_final" if final else "out_proj",
    )(*args)


def _attend(problems, scale, mxu_sums=False):
    scores = [[_dot_nt(q, k) for k in ks] for q, ks, _, _, _, _ in problems]
    outs = []
    parts = []
    for (q, ks, vs, masks, biases, sink), raw in zip(problems, scores):
        ss = []
        for s, m, bias in zip(raw, masks, biases):
            s = s * scale
            if bias is not None:
                s = s + bias
            if m is not None:
                s = jnp.where(m, s, NEG)
            ss.append(s)
        mx = functools.reduce(jnp.maximum, [jnp.max(s, axis=-1, keepdims=True) for s in ss])
        if sink is not None:
            mx = jnp.maximum(mx, sink)
        es = [jnp.exp(s - mx) for s in ss]
        if mxu_sums:
            es = [e.astype(BF16) for e in es]
        parts.append((es, None if sink is None else jnp.exp(sink - mx)))
    pvs = [[_dot(e, v) for e, v in zip(es, vs)] for (es, _), (_, _, vs, _, _, _) in zip(parts, problems)]
    if mxu_sums:
        sums = [[_dot(e, jnp.ones((e.shape[1], HD), BF16)) for e in es] for es, _ in parts]
    else:
        sums = [[jnp.sum(e, axis=-1, keepdims=True) for e in es] for es, _ in parts]
    for pv, sm, (_, tail) in zip(pvs, sums, parts):
        den = functools.reduce(jnp.add, sm)
        if tail is not None:
            den = den + tail
        outs.append(functools.reduce(jnp.add, pv) / den)
    return outs


def _head(ref, h, rows=None):
    if rows is None:
        return ref[:, h * HD:(h + 1) * HD]
    return ref[rows, h * HD:(h + 1) * HD]


def _ctx_attn_kernel(*refs, heads, kv_heads, use_sink, has_prev):
    q_ref, k_ref, v_ref, z_ref = refs[:4]
    sink_ref = refs[4] if use_sink else None
    o_ref, kv_ref = refs[-2:]
    g = heads // kv_heads
    n = q_ref.shape[0]
    problems = []
    for j in range(kv_heads):
        q = jnp.concatenate([_head(q_ref, j * g + t) for t in range(g)], axis=0)
        sink = None
        if use_sink:
            sink = jnp.concatenate([jnp.full((n, 1), sink_ref[j * g + t], F32) for t in range(g)], axis=0)
        problems.append((q, [_head(k_ref, j)], [_head(v_ref, j)], [None], [None], sink))
    outs = _attend(problems, HD ** -0.5, mxu_sums=kv_heads < heads)
    for j, o in enumerate(outs):
        for t in range(g):
            h = j * g + t
            o_ref[:, h * HD:(h + 1) * HD] = (o[t * n:(t + 1) * n] * _silu(_head(z_ref, h))).astype(BF16)
    slots = [kv_ref] if has_prev else [kv_ref.at[l] for l in range(kv_ref.shape[0])]
    for slot in slots:
        slot[0] = k_ref[...]
        slot[1] = v_ref[...]


def _ctx_attn(p, q_col, k_col, v_col, z_col, heads, kv_heads, sink, layers, layer, prev):
    wq, wkv = heads * HD, kv_heads * HD
    use_sink = sink is not None
    in_specs = [
        pl.BlockSpec((SEQ, wq), lambda b: (b, q_col)),
        pl.BlockSpec((SEQ, wkv), lambda b: (b, k_col)),
        pl.BlockSpec((SEQ, wkv), lambda b: (b, v_col)),
        pl.BlockSpec((SEQ, wq), lambda b: (b, z_col)),
    ]
    args = [p, p, p, p]
    if use_sink:
        in_specs.append(pl.BlockSpec(memory_space=pltpu.SMEM))
        args.append(sink)
    aliases = {}
    if prev is None:
        kv_spec = pl.BlockSpec((None, layers, 2, SEQ, wkv), lambda b: (b, 0, 0, 0, 0))
    else:
        in_specs.append(pl.BlockSpec(memory_space=pl.ANY))
        args.append(prev)
        aliases = {len(args) - 1: 1}
        kv_spec = pl.BlockSpec((None, None, 2, SEQ, wkv), lambda b: (b, layer, 0, 0, 0))
    return pl.pallas_call(
        functools.partial(_ctx_attn_kernel, heads=heads, kv_heads=kv_heads, use_sink=use_sink,
                          has_prev=prev is not None),
        grid=(BATCH,),
        in_specs=in_specs,
        out_specs=[pl.BlockSpec((SEQ, wq), lambda b: (b, 0)), kv_spec],
        out_shape=[jax.ShapeDtypeStruct((N_CTX, wq), BF16),
                   jax.ShapeDtypeStruct((BATCH, layers, 2, SEQ, wkv), F32)],
        input_output_aliases=aliases,
        compiler_params=_cparams(("arbitrary",)),
        name="ctx_attn_sink" if use_sink else "ctx_attn",
    )(*args)


def _rope_tables():
    half = HD // 4
    freq = (ROPE_THETA ** (-np.arange(half, dtype=np.float32) / half)).astype(np.float32)
    t = np.arange(DEC_SEQ)
    ang_r = (t // GRID_W).astype(np.float32)[:, None] * freq[None, :]
    ang_c = (t % GRID_W).astype(np.float32)[:, None] * freq[None, :]
    cos = np.concatenate([np.cos(ang_r)] * 2 + [np.cos(ang_c)] * 2, axis=1).astype(np.float32)
    sin_r, sin_c, zero = np.sin(ang_r), np.sin(ang_c), np.zeros_like(ang_r)
    s_up = np.concatenate([-sin_r, zero, -sin_c, zero], axis=1).astype(np.float32)
    s_dn = np.concatenate([zero, sin_r, zero, sin_c], axis=1).astype(np.float32)
    return jnp.asarray(cos), jnp.asarray(s_up), jnp.asarray(s_dn)


def _rope(x, cos, s_up, s_dn):
    return x * cos + pltpu.roll(x, HD - HD // 4, 1) * s_up + pltpu.roll(x, HD // 4, 1) * s_dn


def _win_attn_kernel(q_ref, k_ref, v_ref, kc_ref, vc_ref, z_ref, cos_ref, sup_ref, sdn_ref, sink_ref, o_ref):
    i = pl.program_id(1)
    g = H_B // HKV_B
    span = QBLK + 2 * WIN
    start = pl.multiple_of(jnp.clip(i * QBLK - WIN, 0, DEC_SEQ - span), QBLK)
    qrows = pl.ds(pl.multiple_of(i * QBLK, QBLK), QBLK)
    krows = pl.ds(start, span)
    cq, uq, dq = cos_ref[qrows, :], sup_ref[qrows, :], sdn_ref[qrows, :]
    ck, uk, dk = cos_ref[krows, :], sup_ref[krows, :], sdn_ref[krows, :]
    qpos = i * QBLK + lax.broadcasted_iota(jnp.int32, (g * QBLK, span), 0) % QBLK
    kpos = start + lax.broadcasted_iota(jnp.int32, (g * QBLK, span), 1)
    band = jnp.abs(qpos - kpos) <= WIN
    problems = []
    for j in range(HKV_B):
        q = jnp.concatenate([_rope(_head(q_ref, j * g + t), cq, uq, dq) for t in range(g)], axis=0)
        kw = _rope(_head(k_ref, j, krows), ck, uk, dk)
        vw = _head(v_ref, j, krows)
        sink = jnp.concatenate([jnp.full((QBLK, 1), sink_ref[j * g + t], F32) for t in range(g)], axis=0)
        problems.append((q, [kw, _head(kc_ref, j)], [vw, _head(vc_ref, j)], [band, None], [None, None], sink))
    outs = _attend(problems, HD ** -0.5)
    for j, o in enumerate(outs):
        for t in range(g):
            h = j * g + t
            o_ref[:, h * HD:(h + 1) * HD] = (o[t * QBLK:(t + 1) * QBLK] * _silu(_head(z_ref, h))).astype(BF16)


def _win_attn(p, cache_k, cache_v, sink, q_col, k_col, v_col, z_col):
    wq, wkv = H_B * HD, HKV_B * HD
    nq = DEC_SEQ // QBLK
    cos, s_up, s_dn = _rope_tables()
    full = pl.BlockSpec((DEC_SEQ, HD), lambda b, i: (0, 0))
    return pl.pallas_call(
        _win_attn_kernel,
        grid=(DEC_BATCH, nq),
        in_specs=[
            pl.BlockSpec((QBLK, wq), lambda b, i: (b * nq + i, q_col)),
            pl.BlockSpec((DEC_SEQ, wkv), lambda b, i: (b, k_col)),
            pl.BlockSpec((DEC_SEQ, wkv), lambda b, i: (b, v_col)),
            pl.BlockSpec((None, PAST_LEN, wkv), lambda b, i: (b, 0, 0)),
            pl.BlockSpec((None, PAST_LEN, wkv), lambda b, i: (b, 0, 0)),
            pl.BlockSpec((QBLK, wq), lambda b, i: (b * nq + i, z_col)),
            full, full, full,
            pl.BlockSpec(memory_space=pltpu.SMEM),
        ],
        out_specs=pl.BlockSpec((QBLK, wq), lambda b, i: (b * nq + i, 0)),
        out_shape=jax.ShapeDtypeStruct((N_LAT, wq), BF16),
        compiler_params=_cparams(("arbitrary", "arbitrary")),
        name="win_attn",
    )(p, p, p, cache_k, cache_v, p, cos, s_up, s_dn, sink)


def _nbr_onehot():
    qc = np.arange(GRID_W)[:, None]
    kc = np.arange(GRID_W)[None, :]
    idx = np.clip(kc - qc, -(NB_W - 1), NB_W - 1) + NB_W - 1
    e = (np.arange(2 * NB_W)[:, None, None] == idx[None]).astype(np.float32)
    return jnp.asarray(e.reshape(2 * NB_W, GRID_W * GRID_W))


def _bias_expand_kernel(r_ref, e_ref, o_ref):
    o_ref[...] = _dot_hi(r_ref[...], e_ref[...])


def _nbr_bias_table(rpb):
    rows = H_D * (2 * NB_H - 1)
    r = jnp.pad(rpb.reshape(rows, 2 * NB_W - 1), ((0, 128 - rows), (0, 1)))
    t = pl.pallas_call(
        _bias_expand_kernel,
        out_shape=jax.ShapeDtypeStruct((128, GRID_W * GRID_W), F32),
        name="nbr_bias_expand",
    )(r, _nbr_onehot())
    t = t[:rows].reshape(H_D, 2 * NB_H - 1, GRID_W, GRID_W)
    return jnp.concatenate([t[:, :-1], t[:, 1:]], axis=-1)


NBR_ROWS = 2


def _nbr_attn_kernel(q_ref, k_ref, v_ref, kc_ref, vc_ref, z_ref, t_ref, o_ref):
    rows = DEC_SEQ // GRID_W
    nk = NB_H * GRID_W
    qc = lax.broadcasted_iota(jnp.int32, (GRID_W, nk), 0)
    kc = lax.broadcasted_iota(jnp.int32, (GRID_W, nk), 1) % GRID_W
    cstart = jnp.clip(qc - NB_W // 2, 0, GRID_W - NB_W)
    ok = (kc >= cstart) & (kc < cstart + NB_W)
    problems = []
    for t in range(NBR_ROWS):
        r = pl.program_id(1) * NBR_ROWS + t
        rs = jnp.clip(r - NB_H // 2, 0, rows - NB_H)
        dr0 = rs - r + NB_H - 1
        krows = pl.ds(pl.multiple_of(rs * GRID_W, GRID_W), nk)
        qrows = slice(t * GRID_W, (t + 1) * GRID_W)
        for h in range(H_D):
            bias = jnp.concatenate([t_ref[h, dr0 + 2 * m] for m in range(nk // LANE)], axis=1)
            problems.append((_head(q_ref, h, qrows), [_head(k_ref, h, krows), _head(kc_ref, h)],
                             [_head(v_ref, h, krows), _head(vc_ref, h)], [ok, None], [bias, None], None))
    outs = _attend(problems, HD ** -0.5)
    for i, o in enumerate(outs):
        t, h = divmod(i, H_D)
        qrows = slice(t * GRID_W, (t + 1) * GRID_W)
        o_ref[qrows, h * HD:(h + 1) * HD] = (o * _silu(_head(z_ref, h, qrows))).astype(BF16)


def _nbr_attn(p, cache_k, cache_v, table, q_col, k_col, v_col, z_col):
    rows = DEC_SEQ // GRID_W // NBR_ROWS
    qb = NBR_ROWS * GRID_W
    return pl.pallas_call(
        _nbr_attn_kernel,
        grid=(DEC_BATCH, rows),
        in_specs=[
            pl.BlockSpec((qb, W_D), lambda b, r: (b * rows + r, q_col)),
            pl.BlockSpec((DEC_SEQ, W_D), lambda b, r: (b, k_col)),
            pl.BlockSpec((DEC_SEQ, W_D), lambda b, r: (b, v_col)),
            pl.BlockSpec((None, PAST_LEN, W_D), lambda b, r: (b, 0, 0)),
            pl.BlockSpec((None, PAST_LEN, W_D), lambda b, r: (b, 0, 0)),
            pl.BlockSpec((qb, W_D), lambda b, r: (b * rows + r, z_col)),
            pl.BlockSpec(table.shape, lambda b, r: (0, 0, 0, 0)),
        ],
        out_specs=pl.BlockSpec((qb, W_D), lambda b, r: (b * rows + r, 0)),
        out_shape=jax.ShapeDtypeStruct((N_LAT, W_D), BF16),
        compiler_params=_cparams(("arbitrary", "arbitrary")),
        name="nbr_attn",
    )(p, p, p, cache_k, cache_v, p, table)


def _lane_col(x, idx):
    lane = lax.broadcasted_iota(jnp.int32, x.shape, 1)
    return jnp.sum(jnp.where(lane == idx, x, 0.0), axis=-1, keepdims=True)


def _short_conv(x_ref, w_ref, pad_ref, n):
    pad = CONV_K // 2
    zeros = jnp.zeros((8, HD), F32)
    pad_ref[0:8, :] = zeros
    pad_ref[n + 8:n + 16, :] = zeros
    pad_ref[8:n + 8, :] = x_ref[...]
    y = functools.reduce(jnp.add, [pad_ref[8 - pad + t:8 - pad + t + n, :] * w_ref[t:t + 1, :]
                                   for t in range(CONV_K)])
    return _silu(y)


def _l2norm(x):
    return x * lax.rsqrt(jnp.sum(x * x, axis=-1, keepdims=True) + EPS)


def _split2(x):
    hi = x.astype(BF16)
    return hi, (x - hi.astype(F32)).astype(BF16)


def _dots_x2(pairs, split_b=True):
    ops = []
    for a, b in pairs:
        a_hi, a_lo = _split2(a)
        b2 = jnp.concatenate(_split2(b), axis=1) if split_b else b.astype(BF16)
        ops.append((jnp.concatenate([a_hi, a_lo], axis=0), b2))
    rs = [jnp.dot(a2, b2, preferred_element_type=F32) for a2, b2 in ops]
    out = []
    for (a, b), r in zip(pairs, rs):
        m, n = a.shape[0], b.shape[1]
        if split_b:
            out.append((r[m:, :n] + r[:m, n:] + r[m:, n:]) + r[:m, :n])
        else:
            out.append(r[m:] + r[:m])
    return out


def _pair_masks():
    c = lax.broadcasted_iota(jnp.int32, (CHUNK, HD), 0)
    l = lax.broadcasted_iota(jnp.int32, (CHUNK, HD), 1)
    left = l < CHUNK
    j = l % CHUNK
    ahead = jnp.where(left, j - c, c - j)
    return left, ahead <= 0, ahead < 0, j == c


def _block_diag(x, left):
    return jnp.concatenate([jnp.where(left, x, 0.0), jnp.where(left, 0.0, x)], axis=0)


def _inv_unit_triangular_pairs(lmats, left, eye):
    mps = [-x for x in lmats]
    ps = [jnp.where(eye, 1.0, 0.0) + m for m in mps]
    mps = _dots_x2([(m, _block_diag(m, left)) for m in mps])
    for _ in range(4):
        rs = _dots_x2([(jnp.concatenate([p, m], axis=0), _block_diag(m, left)) for p, m in zip(ps, mps)])
        ps = [p + r[:CHUNK] for p, r in zip(ps, rs)]
        mps = [r[CHUNK:] for r in rs]
    rs = _dots_x2([(p, _block_diag(m, left)) for p, m in zip(ps, mps)])
    return [p + r for p, r in zip(ps, rs)]


def _chunk_cumsum(x, reverse):
    n = x.shape[0]
    pos = lax.broadcasted_iota(jnp.int32, x.shape, 0) % CHUNK
    k = 1
    while k < CHUNK:
        if reverse:
            x = x + jnp.where(pos < CHUNK - k, pltpu.roll(x, n - k, 0), 0.0)
        else:
            x = x + jnp.where(pos >= k, pltpu.roll(x, k, 0), 0.0)
        k *= 2
    return x


def _delta_prep(items):
    left, tri, strict, eye = _pair_masks()
    zeros = jnp.zeros((CHUNK, HD), F32)
    pre = []
    for q, k, v, cols in items:
        b_f, b_b, g_f, g_b = (cols[:, t:t + 1] for t in range(4))
        gsel = jnp.where(left, g_f, g_b)
        g_row = jnp.sum(jnp.where(eye, gsel, 0.0), axis=0, keepdims=True)
        decay = jnp.exp(jnp.where(tri, gsel - g_row, NEG))
        kb_f, kb_b = k * b_f, k * b_b
        lhs = jnp.concatenate([jnp.concatenate([kb_f, kb_b], axis=1), jnp.concatenate([q, q], axis=1)], axis=0)
        rhs = jnp.concatenate([jnp.concatenate([k, zeros], axis=1), jnp.concatenate([zeros, k], axis=1)], axis=0)
        pre.append((decay, kb_f, kb_b, g_f, g_b, b_f, b_b, lhs, rhs))
    kqs = [_dot_nt(x[7], x[8]) for x in pre]
    lmats = [jnp.where(strict, kq[:CHUNK] * x[0], 0.0) for kq, x in zip(kqs, pre)]
    attns = [jnp.where(tri, kq[CHUNK:] * x[0], 0.0) for kq, x in zip(kqs, pre)]
    tinvs = _inv_unit_triangular_pairs(lmats, left, eye)
    egs, rhs = [], []
    for (q, k, v, cols), x in zip(items, pre):
        _, kb_f, kb_b, g_f, g_b, b_f, b_b = x[:7]
        eg_f = jnp.exp(jnp.broadcast_to(g_f, (CHUNK, HD)))
        eg_b = jnp.exp(jnp.broadcast_to(g_b, (CHUNK, HD)))
        egs.append((eg_f, eg_b))
        rhs.append(jnp.concatenate([jnp.concatenate([v * b_f, kb_f * eg_f, zeros, zeros], axis=1),
                                    jnp.concatenate([zeros, zeros, v * b_b, kb_b * eg_b], axis=1)], axis=0))
    uws = _dots_x2(list(zip(tinvs, rhs)), split_b=False)
    out = []
    for (q, k, v, cols), x, attn, uw, (eg_f, eg_b) in zip(items, pre, attns, uws, egs):
        g_f, g_b = x[3], x[4]
        gl_f = jnp.broadcast_to(g_f[CHUNK - 1:CHUNK, :], (1, HD))
        gl_b = jnp.broadcast_to(g_b[0:1, :], (1, HD))
        kg_t = jnp.concatenate([k * jnp.exp(gl_f - g_f), k * jnp.exp(gl_b - g_b)], axis=1).T
        out.append(dict(u=[uw[:, :HD], uw[:, 2 * HD:3 * HD]], w=[uw[:, HD:2 * HD], uw[:, 3 * HD:]],
                        attn=[attn[:, :CHUNK], attn[:, CHUNK:]], qg=[q * eg_f, q * eg_b],
                        kg_t=[kg_t[:HD], kg_t[HD:]], eg=[jnp.exp(gl_f), jnp.exp(gl_b)]))
    return out


def _delta_kernel(*refs, n, hb, cg, has_s0, has_prev):
    qp_ref, kp_ref, vp_ref, z_ref, sm_ref, cq_ref, ck_ref, cv_ref, al_ref, dt_ref, on_ref = refs[:11]
    rest = refs[11 + int(has_prev):]
    if has_s0:
        s0_ref, o_ref = rest[:2]
        so_ref = None
    else:
        o_ref, so_ref = rest[:2]
    q_s, k_s, v_s, pad_s, col_s, u_s, w_s, at_s, qg_s, kg_s, eg_s, acc_s, st_s, gate_s = rest[2:]
    h0 = pl.program_id(1) * hb
    nc = n // CHUNK

    @pl.when(pl.program_id(1) == 0)
    def _():
        sm = sm_ref[...]
        x = sm + dt_ref[...]
        softplus = jnp.maximum(x, 0.0) + jnp.log(1.0 + jnp.exp(-jnp.abs(x)))
        gates = -jnp.exp(al_ref[...]) * softplus
        gate_s[0] = _sigmoid(sm)
        gate_s[1] = _chunk_cumsum(gates, False)
        gate_s[2] = _chunk_cumsum(gates, True)

    betas, cum_f, cum_b = gate_s[0], gate_s[1], gate_s[2]
    lane = lax.broadcasted_iota(jnp.int32, (n, HD), 1)
    for j in range(hb):
        cols = slice(j * HD, (j + 1) * HD)
        q_s[j] = _l2norm(_short_conv(qp_ref.at[:, cols], cq_ref.at[:, cols], pad_s, n)) * HD ** -0.5
        k_s[j] = _l2norm(_short_conv(kp_ref.at[:, cols], ck_ref.at[:, cols], pad_s, n))
        v_s[j] = _short_conv(vp_ref.at[:, cols], cv_ref.at[:, cols], pad_s, n)
        col_s[j] = jnp.where(lane == 0, _lane_col(betas, h0 + j),
                             jnp.where(lane == 1, _lane_col(betas, H_A + h0 + j),
                                       jnp.where(lane == 2, _lane_col(cum_f, 2 * H_A + h0 + j),
                                                 _lane_col(cum_b, 3 * H_A + h0 + j))))
    acc_s[...] = jnp.zeros((hb, n, HD), F32)
    if has_s0:
        st_s[...] = s0_ref[...]
    else:
        st_s[...] = jnp.zeros((2, hb, HD, HD), F32)

    def prep(ci, carry):
        where = [(j, ci * cg + t) for j in range(hb) for t in range(cg)]
        rows = [pl.ds(pl.multiple_of(c * CHUNK, CHUNK), CHUNK) for _, c in where]
        outs = _delta_prep([(q_s[j, r, :], k_s[j, r, :], v_s[j, r, :], col_s[j, r, :])
                            for (j, _), r in zip(where, rows)])
        for (j, c), r, o in zip(where, rows, outs):
            for d in range(2):
                u_s[d, j, r, :] = o["u"][d]
                w_s[d, j, r, :] = o["w"][d].astype(BF16)
                at_s[d, j, r, :] = o["attn"][d].astype(BF16)
                qg_s[d, j, r, :] = o["qg"][d].astype(BF16)
                kg_s[d, j, c] = o["kg_t"][d].astype(BF16)
                eg_s[d, j, c] = jnp.broadcast_to(o["eg"][d], (8, HD))
        return carry

    lax.fori_loop(0, nc // cg, prep, 0)

    def scan(i, carry):
        chains = [(d, j, (nc - 1 - i) if d else i) for d in range(2) for j in range(hb)]
        rows = [pl.ds(pl.multiple_of(c * CHUNK, CHUNK), CHUNK) for _, _, c in chains]
        ss = [st_s[d, j] for d, j, _ in chains]
        sbs = [s.astype(BF16) for s in ss]
        ws = [jnp.dot(w_s[d, j, r, :], sb, preferred_element_type=F32) for (d, j, _), r, sb in zip(chains, rows, sbs)]
        vns = [(u_s[d, j, r, :] - w).astype(BF16) for (d, j, _), r, w in zip(chains, rows, ws)]
        for (d, j, c), r, s, sb, vn in zip(chains, rows, ss, sbs, vns):
            st_s[d, j] = s * eg_s[d, j, c, 0:1, :] + jnp.dot(kg_s[d, j, c], vn, preferred_element_type=F32)
        for (d, j, c), r, sb, vn in zip(chains, rows, sbs, vns):
            acc_s[j, r, :] += (jnp.dot(qg_s[d, j, r, :], sb, preferred_element_type=F32)
                               + jnp.dot(at_s[d, j, r, :], vn, preferred_element_type=F32))
        return carry

    lax.fori_loop(0, nc, scan, 0)
    for j in range(hb):
        cols = slice(j * HD, (j + 1) * HD)
        o = acc_s[j]
        o = o * lax.rsqrt(jnp.mean(o * o, axis=-1, keepdims=True) + EPS) * on_ref[...]
        o_ref[:, cols] = (o * _silu(z_ref[:, cols])).astype(BF16)
    if so_ref is not None:
        _write_layer_slots(so_ref, st_s[...], has_prev)


def _write_layer_slots(ref, value, has_prev):
    if has_prev:
        ref[...] = value
    else:
        for l in range(ref.shape[0]):
            ref[l] = value


def _delta_mixer(p, small, conv_w, a_row, dt_row, onorm, s0, seq, nb, hb, cg, layers=1, layer=0, prev=None):
    has_s0 = s0 is not None
    wb = hb * HD
    ng = H_A // hb
    nc = seq // CHUNK
    col = lambda off: (lambda b, h: (b, off + h))
    in_specs = [
        pl.BlockSpec((seq, wb), col(0)),
        pl.BlockSpec((seq, wb), col(ng)),
        pl.BlockSpec((seq, wb), col(2 * ng)),
        pl.BlockSpec((seq, wb), col(3 * ng)),
        pl.BlockSpec((seq, LANE), lambda b, h: (b, 0)),
        pl.BlockSpec((CONV_K, wb), lambda b, h: (0, h)),
        pl.BlockSpec((CONV_K, wb), lambda b, h: (0, ng + h)),
        pl.BlockSpec((CONV_K, wb), lambda b, h: (0, 2 * ng + h)),
        pl.BlockSpec((1, LANE), lambda b, h: (0, 0)),
        pl.BlockSpec((1, LANE), lambda b, h: (0, 0)),
        pl.BlockSpec((1, HD), lambda b, h: (0, 0)),
    ]
    args = [p, p, p, p, small, conv_w, conv_w, conv_w, a_row, dt_row, onorm.reshape(1, HD)]
    if has_s0:
        in_specs.append(pl.BlockSpec((None, 2, hb, HD, HD), lambda b, h: (b, 0, h, 0, 0)))
        args.append(s0)
    out_specs = [pl.BlockSpec((seq, wb), lambda b, h: (b, h))]
    out_shape = [jax.ShapeDtypeStruct((nb * seq, W_A), BF16)]
    aliases = {}
    if not has_s0:
        if prev is None:
            out_specs.append(pl.BlockSpec((None, layers, 2, hb, HD, HD), lambda b, h: (b, 0, 0, h, 0, 0)))
        else:
            in_specs.append(pl.BlockSpec(memory_space=pl.ANY))
            args.append(prev)
            aliases = {len(args) - 1: 1}
            out_specs.append(pl.BlockSpec((None, None, 2, hb, HD, HD), lambda b, h: (b, layer, 0, h, 0, 0)))
        out_shape.append(jax.ShapeDtypeStruct((nb, layers, 2, H_A, HD, HD), F32))
    return pl.pallas_call(
        functools.partial(_delta_kernel, n=seq, hb=hb, cg=cg, has_s0=has_s0, has_prev=prev is not None),
        grid=(nb, ng),
        in_specs=in_specs,
        out_specs=out_specs,
        out_shape=out_shape,
        input_output_aliases=aliases,
        scratch_shapes=[
            pltpu.VMEM((hb, seq, HD), F32), pltpu.VMEM((hb, seq, HD), F32), pltpu.VMEM((hb, seq, HD), F32),
            pltpu.VMEM((seq + 16, HD), F32), pltpu.VMEM((hb, seq, HD), F32),
            pltpu.VMEM((2, hb, seq, HD), F32), pltpu.VMEM((2, hb, seq, HD), BF16),
            pltpu.VMEM((2, hb, seq, CHUNK), BF16), pltpu.VMEM((2, hb, seq, HD), BF16),
            pltpu.VMEM((2, hb, nc, HD, CHUNK), BF16), pltpu.VMEM((2, hb, nc, 8, HD), F32),
            pltpu.VMEM((hb, seq, HD), F32), pltpu.VMEM((2, hb, HD, HD), F32),
            pltpu.VMEM((3, seq, LANE), F32)],
        compiler_params=_cparams(("arbitrary", "arbitrary")),
        name="delta_lat" if has_s0 else "delta_ctx",
    )(*args)


_GLA_LEVELS = (32, 16, 8, 4, 2, 1)
GLA_CHUNKS = 4


def _gla_consts(reverse):
    r = np.arange(CHUNK)
    flip = (lambda a: a[::-1, ::-1]) if reverse else (lambda a: a)
    sel, hi, pair = [], [], []
    for m in _GLA_LEVELS:
        mid = (r // (2 * m)) * (2 * m) + m
        is_hi = r >= mid
        sel.append(flip(r[None, :] == mid[:, None] - 1))
        hi.append(flip(is_hi[:, None]))
        pair.append(flip((r[:, None] // (2 * m) == r[None, :] // (2 * m)) & is_hi[:, None] & ~is_hi[None, :]))
    out = [np.concatenate(sel), np.concatenate(hi), np.stack(pair)]
    out = [jnp.asarray(np.ascontiguousarray(a).astype(np.float32)) for a in out]
    return [out[0].astype(BF16), out[1], out[2]]


def _split3(x):
    hi = x.astype(BF16)
    r1 = x - hi.astype(F32)
    mid = r1.astype(BF16)
    return hi, mid, (r1 - mid.astype(F32)).astype(BF16)


def _gla_prep(items):
    nl = len(_GLA_LEVELS)
    r3s = [jnp.dot(consts[0][...], jnp.concatenate(_split3(b), axis=1), preferred_element_type=F32)
           for _, _, _, b, consts, _ in items]
    refs = [(r3[:, 2 * DK_C:] + r3[:, DK_C:2 * DK_C]) + r3[:, :DK_C] for r3 in r3s]
    c = lax.broadcasted_iota(jnp.int32, (CHUNK, CHUNK), 0)
    j = lax.broadcasted_iota(jnp.int32, (CHUNK, CHUNK), 1)
    lvl = []
    for (q, k, v, b, consts, _), ref in zip(items, refs):
        ops = []
        for lv in range(nl):
            rows = slice(lv * CHUNK, (lv + 1) * CHUNK)
            hi = consts[1][rows, :] > 0.5
            t = b - ref[rows]
            e = jnp.exp(jnp.where(hi, t, -t))
            ops.append((jnp.where(hi, q * e, 0.0), jnp.where(hi, 0.0, k * e)))
        lvl.append(ops)
    prods = [[_dot_nt(ql, kl) for ql, kl in ops] for ops in lvl]
    amats = []
    for (q, k, v, b, consts, _), pr in zip(items, prods):
        a = jnp.where(c == j, jnp.sum(q * k, axis=-1, keepdims=True), 0.0)
        for lv in range(nl):
            a = a + pr[lv] * consts[2][lv]
        amats.append(a)
    intras = [_dot(a, it[2]) for a, it in zip(amats, items)]
    r128 = lax.broadcasted_iota(jnp.int32, (DK_C, DK_C), 0)
    c128 = lax.broadcasted_iota(jnp.int32, (DK_C, DK_C), 1)
    out = []
    for (q, k, v, b, consts, reverse), intra in zip(items, intras):
        last = 0 if reverse else CHUNK - 1
        bl = b[last:last + 1, :]
        dec = jnp.sum(jnp.where(r128 == c128, jnp.broadcast_to(jnp.exp(bl), (DK_C, DK_C)), 0.0),
                      axis=-1, keepdims=True)
        out.append((intra, q * jnp.exp(b), dec, _dot_tn(k * jnp.exp(bl - b), v)))
    return out


def _gla_kernel(*refs, n, cg, has_s0, has_prev):
    consts_f, consts_b = refs[8:11], refs[11:14]
    q_ref, k_ref, v_ref, z_ref, sm_ref, wg_ref, bg_ref, on_ref = refs[:8]
    if has_s0:
        s0_ref, o_ref, gk_s, acc_s, st_s = refs[14:]
        so_ref = None
    else:
        o_ref, so_ref, gk_s, acc_s, st_s = refs[14 + int(has_prev):]
    nc = n // CHUNK
    sm = sm_ref[...]
    for d in range(2):
        x = _dot_hi(sm, wg_ref[d]) + bg_ref[d]
        gk = (jnp.minimum(x, 0.0) - jnp.log(1.0 + jnp.exp(-jnp.abs(x)))) / GLA_TAU
        gk_s[d] = _chunk_cumsum(gk, bool(d))
    acc_s[...] = jnp.zeros((n, DV_C), F32)
    if has_s0:
        st_s[...] = s0_ref[...]
    else:
        st_s[...] = jnp.zeros((2, DK_C, DV_C), F32)

    def body(i, carry):
        where = [(d, (nc - 1 - (i * cg + t)) if d else (i * cg + t)) for d in range(2) for t in range(cg)]
        rows = [pl.ds(pl.multiple_of(c * CHUNK, CHUNK), CHUNK) for _, c in where]
        outs = _gla_prep([(q_ref[r, :] * DK_C ** -0.5, k_ref[r, :], v_ref[r, :], gk_s[d, r, :],
                           consts_b if d else consts_f, bool(d)) for (d, _), r in zip(where, rows)])
        for d in range(2):
            s = st_s[d]
            for t in range(cg):
                intra, qe, dec, kv = outs[d * cg + t]
                acc_s[rows[d * cg + t], :] += intra + _dot(qe, s)
                s = s * dec + kv
            st_s[d] = s
        return carry

    lax.fori_loop(0, nc // cg, body, 0)
    o = acc_s[...]
    o = o * lax.rsqrt(jnp.mean(o * o, axis=-1, keepdims=True) + EPS) * on_ref[...]
    o_ref[...] = (o * _silu(z_ref[...])).astype(BF16)
    if so_ref is not None:
        _write_layer_slots(so_ref, st_s[...], has_prev)


def _gla_mixer(p, small, w_gate, b_gate, onorm, s0, seq, nb, layers=1, layer=0, prev=None):
    has_s0 = s0 is not None
    consts = _gla_consts(False) + _gla_consts(True)
    const_specs = [pl.BlockSpec(a.shape, (lambda b, h, nd=a.ndim: (0,) * nd)) for a in consts]
    in_specs = [
        pl.BlockSpec((seq, DK_C), lambda b, h: (b, h)),
        pl.BlockSpec((seq, DK_C), lambda b, h: (b, QK_C // DK_C + h)),
        pl.BlockSpec((seq, DV_C), lambda b, h: (b, 2 * QK_C // DV_C + h)),
        pl.BlockSpec((seq, DV_C), lambda b, h: (b, (2 * QK_C + W_C) // DV_C + h)),
        pl.BlockSpec((seq, LANE), lambda b, h: (b, 0)),
        pl.BlockSpec((2, LANE, DK_C), lambda b, h: (0, 0, h)),
        pl.BlockSpec((2, 1, DK_C), lambda b, h: (0, 0, h)),
        pl.BlockSpec((1, DV_C), lambda b, h: (0, 0)),
    ] + const_specs
    args = [p, p, p, p, small, w_gate, b_gate.reshape(2, 1, QK_C), onorm.reshape(1, DV_C)] + consts
    if has_s0:
        in_specs.append(pl.BlockSpec((None, 2, None, DK_C, DV_C), lambda b, h: (b, 0, h, 0, 0)))
        args.append(s0)
    out_specs = [pl.BlockSpec((seq, DV_C), lambda b, h: (b, h))]
    out_shape = [jax.ShapeDtypeStruct((nb * seq, W_C), BF16)]
    aliases = {}
    if not has_s0:
        if prev is None:
            out_specs.append(pl.BlockSpec((None, layers, 2, None, DK_C, DV_C), lambda b, h: (b, 0, 0, h, 0, 0)))
        else:
            in_specs.append(pl.BlockSpec(memory_space=pl.ANY))
            args.append(prev)
            aliases = {len(args) - 1: 1}
            out_specs.append(pl.BlockSpec((None, None, 2, None, DK_C, DV_C), lambda b, h: (b, layer, 0, h, 0, 0)))
        out_shape.append(jax.ShapeDtypeStruct((nb, layers, 2, H_C, DK_C, DV_C), F32))
    return pl.pallas_call(
        functools.partial(_gla_kernel, n=seq, cg=GLA_CHUNKS, has_s0=has_s0, has_prev=prev is not None),
        grid=(nb, H_C),
        in_specs=in_specs,
        out_specs=out_specs,
        out_shape=out_shape,
        input_output_aliases=aliases,
        scratch_shapes=[pltpu.VMEM((2, seq, DK_C), F32), pltpu.VMEM((seq, DV_C), F32),
                        pltpu.VMEM((2, DK_C, DV_C), F32)],
        compiler_params=_cparams(("arbitrary", "arbitrary")),
        name="gla_lat" if has_s0 else "gla_ctx",
    )(*args)


_EV_TILES = tuple(range(4 * W_A // TN_IN)) + (8, 9, 11, 12, 10)
_EV_ALIGNED = 4 * W_A // TN_IN
_EV_QB = 4 * W_A
_EV_ZB = _EV_QB + W_B
_EV_KB = _EV_ZB + W_B
_EV_VB = _EV_KB + HKV_B * HD
_OD_ALIGNED = (2 * QK_C + 2 * W_C) // TN_IN
_OD_TILES = tuple(range((P_ODD - 2 * GLA_RANK) // TN_IN))
_OD_QD = 2 * QK_C + 2 * W_C
_OD_KD = _OD_QD + W_D
_OD_VD = _OD_KD + W_D
_OD_ZD = _OD_VD + W_D


def _lane_row(v, offset):
    return jnp.pad(v.reshape(1, -1), ((0, 0), (offset, LANE - offset - v.size)))


def _even_layer(xc, xl, e, mod, norm_w, w_in, conv_a, a_log, dt_bias, onorm, sink, w_out, state_delta, cache_kv,
                final_w, prev_st, prev_kv):
    pc, sc = _in_proj(xc, norm_w, mod, w_in, e, _EV_TILES, _EV_ALIGNED, 4 * W_A, False)
    pq, sq = _in_proj(xl, norm_w, mod, w_in, e, _EV_TILES, _EV_ALIGNED, 4 * W_A, True)
    a_row = _lane_row(a_log, 2 * H_A)
    dt_row = _lane_row(dt_bias, 2 * H_A)
    oa_c, st = _delta_mixer(pc, sc, conv_a, a_row, dt_row, onorm, None, SEQ, BATCH, 8, 1, N_EVEN, e, prev_st)
    (oa_l,) = _delta_mixer(pq, sq, conv_a, a_row, dt_row, onorm, state_delta[:, e], DEC_SEQ, DEC_BATCH, 4, 2)
    wkv = HKV_B * HD
    ob_c, kv = _ctx_attn(pc, _EV_QB // W_B, _EV_KB // wkv, _EV_VB // wkv, _EV_ZB // W_B, H_B, HKV_B, sink,
                         N_EVEN, e, prev_kv)
    ck = cache_kv[:, e, 0].reshape(DEC_BATCH, PAST_LEN, wkv)
    cv = cache_kv[:, e, 1].reshape(DEC_BATCH, PAST_LEN, wkv)
    ob_l = _win_attn(pq, ck, cv, sink, _EV_QB // W_B, _EV_KB // wkv, _EV_VB // wkv, _EV_ZB // W_B)
    xc = _out_proj(oa_c, ob_c, w_out, e, xc, mod, False, final_w)
    xl = _out_proj(oa_l, ob_l, w_out, e, xl, mod, True, final_w)
    return xc, xl, st, kv


def _odd_layer(xc, xl, o_i, mod, norm_w, w_in, w_glr, b_glr, onorm, rpb, w_out, state_gla, cache_kv, final_w,
               prev_st, prev_kv):
    lo = 2 * QK_C + 2 * W_C
    pc, sc = _in_proj(xc, norm_w, mod, w_in, o_i, _OD_TILES, _OD_ALIGNED, lo, False)
    pq, sq = _in_proj(xl, norm_w, mod, w_in, o_i, _OD_TILES, _OD_ALIGNED, lo, True)
    w_gate = jnp.stack([jnp.pad(w_glr[0], ((0, LANE - GLA_RANK), (0, 0))),
                        jnp.pad(w_glr[1], ((GLA_RANK, LANE - 2 * GLA_RANK), (0, 0)))])
    oc_c, st = _gla_mixer(pc, sc, w_gate, b_glr, onorm, None, SEQ, BATCH, N_ODD, o_i, prev_st)
    (oc_l,) = _gla_mixer(pq, sq, w_gate, b_glr, onorm, state_gla[:, o_i], DEC_SEQ, DEC_BATCH)
    od_c, kv = _ctx_attn(pc, _OD_QD // W_D, _OD_KD // W_D, _OD_VD // W_D, _OD_ZD // W_D, H_D, H_D, None,
                         N_ODD, o_i, prev_kv)
    ck = cache_kv[:, o_i, 0].reshape(DEC_BATCH, PAST_LEN, W_D)
    cv = cache_kv[:, o_i, 1].reshape(DEC_BATCH, PAST_LEN, W_D)
    od_l = _nbr_attn(pq, ck, cv, _nbr_bias_table(rpb), _OD_QD // W_D, _OD_KD // W_D, _OD_VD // W_D, _OD_ZD // W_D)
    xc = _out_proj(oc_c, od_c, w_out, o_i, xc, mod, False, final_w)
    xl = _out_proj(oc_l, od_l, w_out, o_i, xl, mod, True, final_w)
    return xc, xl, st, kv


def kernel(x_prompt, x_sample, state_delta, cache_kv_win, state_gla, cache_kv_nbr, c, c_ctx, norm_w, w_ada, b_ada, w_in_even, conv_a, a_log_a, dt_bias_a, onorm_a, sink_b, w_out_even, w_in_odd, w_glr_c, b_glr_c, onorm_c, rpb_d, w_out_odd, final_norm_w):
    xc = x_prompt.reshape(N_CTX, D_MODEL)
    xl = x_sample.reshape(N_LAT, D_MODEL)
    cond = jnp.concatenate([c_ctx[None, :], c, jnp.zeros((N_COND - 1 - DEC_BATCH, D_MODEL), F32)], axis=0)
    mods = _ada_mod(cond, w_ada, b_ada).reshape(DEPTH, N_COND, 1, 3 * D_MODEL)
    wo_even, wo_odd = w_out_even.astype(BF16), w_out_odd.astype(BF16)
    w_in_even, w_in_odd = jnp.swapaxes(w_in_even, 1, 2), jnp.swapaxes(w_in_odd, 1, 2)
    new_delta = new_kvw = new_gla = new_kvn = None
    for li in range(DEPTH):
        final_w = final_norm_w if li == DEPTH - 1 else None
        if li % 2 == 0:
            e = li // 2
            xc, xl, new_delta, new_kvw = _even_layer(xc, xl, e, mods[li], norm_w[li], w_in_even, conv_a[e], a_log_a[e],
                                                     dt_bias_a[e], onorm_a[e], sink_b[e], wo_even, state_delta,
                                                     cache_kv_win, final_w, new_delta, new_kvw)
        else:
            o_i = li // 2
            xc, xl, new_gla, new_kvn = _odd_layer(xc, xl, o_i, mods[li], norm_w[li], w_in_odd, w_glr_c[o_i],
                                                  b_glr_c[o_i], onorm_c[o_i], rpb_d[o_i], wo_odd, state_gla,
                                                  cache_kv_nbr, final_w, new_gla, new_kvn)
    return (xc.reshape(BATCH, SEQ, D_MODEL), xl.reshape(DEC_BATCH, DEC_SEQ, D_MODEL),
            new_delta, new_kvw.reshape(BATCH, N_EVEN, 2, SEQ, HKV_B, HD),
            new_gla, new_kvn.reshape(BATCH, N_ODD, 2, SEQ, H_D, HD))
```
